```python
import math
import jax, jax.numpy as jnp
from jax import lax
import numpy as np


D_MODEL = 2048
BATCH = 4
SEQ = 4096
DEPTH = 4

GRID_W = 64
CTX_LEN = 256
N_MIXERS = 2
N_LAYERS_A = (DEPTH + N_MIXERS - 1) // N_MIXERS
N_LAYERS_B = DEPTH // N_MIXERS

MLSTM_HEADS = 8
MLSTM_DK = D_MODEL // (2 * MLSTM_HEADS)
MLSTM_DV = D_MODEL // MLSTM_HEADS
MLSTM_CHUNK = 64
GATE_CAP = 15.0
MLSTM_SPLITS = (MLSTM_HEADS * MLSTM_DK, 2 * MLSTM_HEADS * MLSTM_DK, 2 * MLSTM_HEADS * MLSTM_DK + MLSTM_HEADS * MLSTM_DV, 2 * MLSTM_HEADS * MLSTM_DK + 2 * MLSTM_HEADS * MLSTM_DV)
MLSTM_IN = MLSTM_SPLITS[-1] + 4 * MLSTM_HEADS

DIFF_HEADS = 8
DIFF_DH = D_MODEL // (2 * DIFF_HEADS)
Q_BLOCK = 128
ROPE_BASE = 10000.0
LAMBDA_STD = 0.1

FFN_HIDDEN = -(-(8 * D_MODEL) // (3 * 256)) * 256
EPS = 1e-6

kernel_name = 'hybrid_mlstm_diffattn_block'


def rms_norm(x, g):
    xf = x.astype(jnp.float32)
    y = xf * lax.rsqrt(jnp.mean(xf * xf, axis=-1, keepdims=True) + EPS)
    return (y * g.astype(jnp.float32)).astype(x.dtype)


def modulate(h, shift, scale):
    return h * (1.0 + scale) + shift


def softcap(x, cap):
    return cap * jnp.tanh(x / cap)


def rope_2d(x, cos, sin):
    Bn, S, Hn, dh = x.shape
    xr = x.astype(jnp.float32).reshape(Bn, S, Hn, 2, 2, dh // 4)
    x1 = xr[..., 0, :]
    x2 = xr[..., 1, :]
    c = cos[None, :, None]
    s = sin[None, :, None]
    out = jnp.stack([x1 * c - x2 * s, x2 * c + x1 * s], axis=-2)
    return out.reshape(Bn, S, Hn, dh).astype(x.dtype)


def swiglu(h, w_gu, w_down):
    g, u = jnp.split(h @ w_gu, 2, axis=-1)
    return (jax.nn.silu(g) * u) @ w_down


def mlstm_scan(q, k, v, ig, lf, state):
    Bn, Hn, S, _ = q.shape
    dv = v.shape[-1]
    L = MLSTM_CHUNK
    nc = S // L

    def chunks(a):
        return jnp.moveaxis(a.reshape(Bn, Hn, nc, L, *a.shape[3:]), 2, 0)

    xs = (chunks(q), chunks(k), chunks(v), chunks(ig), chunks(lf))
    mask = jnp.tril(jnp.ones((L, L), dtype=bool))

    def step(carry, inp):
        C, n, m = carry
        qc, kc, vc, ic, fc = inp
        b = jnp.cumsum(fc, axis=-1)
        dmat = b[..., :, None] - b[..., None, :] + ic[..., None, :]
        dmat = jnp.where(mask, dmat, -jnp.inf)
        inter = b + m[..., None]
        m_t = jnp.maximum(inter, jnp.max(dmat, axis=-1))
        w_intra = jnp.exp(dmat - m_t[..., None])
        w_inter = jnp.exp(inter - m_t)
        s = jnp.einsum('bhtd,bhsd->bhts', qc, kc) * w_intra
        num = w_inter[..., None] * jnp.einsum('bhtd,bhde->bhte', qc, C) + jnp.einsum('bhts,bhse->bhte', s, vc)
        den = w_inter * jnp.einsum('bhtd,bhd->bht', qc, n) + jnp.sum(s, axis=-1)
        h = num / jnp.maximum(jnp.abs(den), jnp.exp(-m_t))[..., None]
        bL = b[..., -1]
        a = bL[..., None] - b + ic
        m_new = jnp.maximum(bL + m, jnp.max(a, axis=-1))
        wk = jnp.exp(a - m_new[..., None])
        dec = jnp.exp(bL + m - m_new)
        kw = kc * wk[..., None]
        C_new = dec[..., None, None] * C + jnp.einsum('bhsd,bhse->bhde', kw, vc)
        n_new = dec[..., None] * n + jnp.sum(kw, axis=2)
        return (C_new, n_new, m_new), h

    state, h = lax.scan(step, state, xs)
    h = jnp.moveaxis(h, 0, 2).reshape(Bn, Hn, S, dv)
    return state, h


def mlstm_mixer(hx, hc, w_in, gate_b, head_g, w_out, update_ctx):
    H, DK, DV = MLSTM_HEADS, MLSTM_DK, MLSTM_DV

    def project(h):
        Bn, S, _ = h.shape
        q, k, v, o, g = jnp.split(h @ w_in, MLSTM_SPLITS, axis=-1)

        def heads(a, d):
            return a.reshape(Bn, S, H, d).transpose(0, 2, 1, 3).astype(jnp.float32)

        q = heads(q, DK) * (DK ** -0.5)
        k = heads(k, DK)
        v = heads(v, DV)
        g = softcap(g.astype(jnp.float32) + gate_b.astype(jnp.float32), GATE_CAP)
        g = g.reshape(Bn, S, 4, H).transpose(2, 0, 3, 1)
        gates = (g[0], jax.nn.log_sigmoid(g[1]), g[2], jax.nn.log_sigmoid(g[3]))
        return q, k, v, o, gates

    qx, kx, vx, ox_gate, gx = project(hx)
    qc, kc, vc, oc_gate, gc = project(hc)
    Bn = hx.shape[0]
    state0 = (jnp.zeros((Bn, H, DK, DV), jnp.float32), jnp.zeros((Bn, H, DK), jnp.float32), jnp.zeros((Bn, H), jnp.float32))

    def flip(a):
        return jnp.flip(a, axis=2)

    st_f, hc_f = mlstm_scan(qc, kc, vc, gc[0], gc[1], state0)
    _, hx_f = mlstm_scan(qx, kx, vx, gx[0], gx[1], st_f)
    st_b, hc_b = mlstm_scan(flip(qc), flip(kc), flip(vc), flip(gc[2]), flip(gc[3]), state0)
    _, hx_b = mlstm_scan(flip(qx), flip(kx), flip(vx), flip(gx[2]), flip(gx[3]), st_b)

    def finish(hf, hb, o):
        Bs, _, S, _ = hf.shape
        h = (hf + hb).transpose(0, 2, 1, 3).astype(o.dtype)
        h = rms_norm(h, head_g.reshape(H, DV)).reshape(Bs, S, H * DV)
        return (h * jax.nn.sigmoid(o)) @ w_out

    ox = finish(hx_f, flip(hx_b), ox_gate)
    oc = finish(hc_f, flip(hc_b), oc_gate) if update_ctx else None
    return ox, oc


def diff_attn_mixer(hx, hc, w_in, w_out, q_g, k_g, lq1, lk1, lq2, lk2, subln_g, lam_init, cos, sin, update_ctx):
    H, DH = DIFF_HEADS, DIFF_DH
    scale = DH ** -0.5

    def project(h, rope):
        Bn, S, _ = h.shape
        q, k, v = jnp.split(h @ w_in, 3, axis=-1)
        q = rms_norm(q.reshape(Bn, S, 2 * H, DH), q_g)
        k = rms_norm(k.reshape(Bn, S, 2 * H, DH), k_g)
        if rope:
            q = rope_2d(q, cos, sin)
            k = rope_2d(k, cos, sin)
        v = v.reshape(Bn, S, H, 2 * DH)
        return q.transpose(0, 2, 1, 3), k.transpose(0, 2, 1, 3), v.transpose(0, 2, 1, 3)

    qx, kx, vx = project(hx, True)
    qc, kc, vc = project(hc, False)
    lam = (jnp.exp(jnp.sum(lq1.astype(jnp.float32) * lk1.astype(jnp.float32)))
           - jnp.exp(jnp.sum(lq2.astype(jnp.float32) * lk2.astype(jnp.float32))) + lam_init)

    def attend(q, k, v):
        Bn, _, Q, _ = q.shape
        s = jnp.einsum('bhqd,bhkd->bhqk', q, k).astype(jnp.float32) * scale
        p = jax.nn.softmax(s, axis=-1).reshape(Bn, H, 2, Q, k.shape[2])
        a = p[:, :, 0] - lam * p[:, :, 1]
        return jnp.einsum('bhqk,bhke->bhqe', a.astype(v.dtype), v)

    def finish(o):
        Bn, S = o.shape[0], o.shape[1]
        o = rms_norm(o, subln_g) * (1.0 - lam_init)
        return o.reshape(Bn, S, H * 2 * DH) @ w_out

    k_all = jnp.concatenate([kx, kc], axis=2)
    v_all = jnp.concatenate([vx, vc], axis=2)
    Bn, _, S, _ = qx.shape
    nqb = S // Q_BLOCK
    qb = qx.reshape(Bn, 2 * H, nqb, Q_BLOCK, DH).transpose(2, 0, 1, 3, 4)
    ob = lax.map(lambda q: attend(q, k_all, v_all), qb)
    ox = ob.transpose(1, 0, 3, 2, 4).reshape(Bn, S, H, 2 * DH)
    ox = finish(ox)
    oc = finish(attend(qc, kc, vc).transpose(0, 2, 1, 3)) if update_ctx else None
    return ox, oc


def setup_inputs(seed: int = 0) -> dict:
    key = jax.random.key(seed)
    ks = jax.random.split(key, 24)
    f32 = jnp.float32

    def nrm(k, shape, s):
        return jax.random.normal(k, shape, f32) * s

    H = MLSTM_HEADS
    gate_base = jnp.concatenate([jnp.zeros((H,), f32), jnp.linspace(3.0, 6.0, H, dtype=f32),
                                 jnp.zeros((H,), f32), jnp.linspace(3.0, 6.0, H, dtype=f32)])
    return {
        'x': nrm(ks[0], (BATCH, SEQ, D_MODEL), 1.0),
        'c': nrm(ks[1], (BATCH, D_MODEL), 1.0),
        'ctx': nrm(ks[2], (BATCH, CTX_LEN, D_MODEL), 1.0),
        'c_ctx': nrm(ks[3], (D_MODEL,), 1.0),
        'ada_w': nrm(ks[4], (DEPTH, D_MODEL, 6 * D_MODEL), 0.5 * D_MODEL ** -0.5),
        'ada_b': nrm(ks[5], (DEPTH, 6 * D_MODEL), 0.02),
        'norm_g': 1.0 + nrm(ks[6], (DEPTH, 2, D_MODEL), 0.02),
        'mlstm_w_in': nrm(ks[7], (N_LAYERS_A, D_MODEL, MLSTM_IN), D_MODEL ** -0.5),
        'mlstm_gate_b': gate_base + nrm(ks[8], (N_LAYERS_A, 4 * H), 0.1),
        'mlstm_head_g': 1.0 + nrm(ks[9], (N_LAYERS_A, H * MLSTM_DV), 0.02),
        'mlstm_w_out': nrm(ks[10], (N_LAYERS_A, H * MLSTM_DV, D_MODEL), (H * MLSTM_DV) ** -0.5),
        'diff_w_in': nrm(ks[11], (N_LAYERS_B, D_MODEL, 3 * D_MODEL), D_MODEL ** -0.5),
        'diff_w_out': nrm(ks[12], (N_LAYERS_B, D_MODEL, D_MODEL), D_MODEL ** -0.5),
        'diff_q_g': 1.0 + nrm(ks[13], (N_LAYERS_B, DIFF_DH), 0.02),
        'diff_k_g': 1.0 + nrm(ks[14], (N_LAYERS_B, DIFF_DH), 0.02),
        'diff_lq1': nrm(ks[15], (N_LAYERS_B, DIFF_DH), LAMBDA_STD),
        'diff_lk1': nrm(ks[16], (N_LAYERS_B, DIFF_DH), LAMBDA_STD),
        'diff_lq2': nrm(ks[17], (N_LAYERS_B, DIFF_DH), LAMBDA_STD),
        'diff_lk2': nrm(ks[18], (N_LAYERS_B, DIFF_DH), LAMBDA_STD),
        'diff_subln_g': 1.0 + nrm(ks[19], (N_LAYERS_B, 2 * DIFF_DH), 0.02),
        'ffn_w_gu': nrm(ks[20], (DEPTH, D_MODEL, 2 * FFN_HIDDEN), D_MODEL ** -0.5),
        'ffn_w_down': nrm(ks[21], (DEPTH, FFN_HIDDEN, D_MODEL), FFN_HIDDEN ** -0.5),
    }


def reference(x, c, ctx, c_ctx, ada_w, ada_b, norm_g, mlstm_w_in, mlstm_gate_b, mlstm_head_g, mlstm_w_out,
              diff_w_in, diff_w_out, diff_q_g, diff_k_g, diff_lq1, diff_lk1, diff_lq2, diff_lk2, diff_subln_g,
              ffn_w_gu, ffn_w_down):
    S = x.shape[1]
    ROWS = S // GRID_W
    row = jnp.repeat(jnp.arange(ROWS), GRID_W)
    col = jnp.tile(jnp.arange(GRID_W), ROWS)
    n_freq = DIFF_DH // 4
    freqs = ROPE_BASE ** (-jnp.arange(n_freq, dtype=jnp.float32) / n_freq)
    ang = jnp.stack([row, col], axis=-1).astype(jnp.float32)[:, :, None] * freqs
    cos, sin = jnp.cos(ang), jnp.sin(ang)

    for i in range(DEPTH):
        update_ctx = i < DEPTH - 1
        mx = [m[:, None, :] for m in jnp.split(jax.nn.silu(c) @ ada_w[i] + ada_b[i], 6, axis=-1)]
        mc = jnp.split(jax.nn.silu(c_ctx) @ ada_w[i] + ada_b[i], 6, axis=-1)
        hx = modulate(rms_norm(x, norm_g[i, 0]), mx[0], mx[1])
        hc = modulate(rms_norm(ctx, norm_g[i, 0]), mc[0], mc[1])
        j = i // N_MIXERS
        if i % N_MIXERS == 0:
            ox, oc = mlstm_mixer(hx, hc, mlstm_w_in[j], mlstm_gate_b[j], mlstm_head_g[j], mlstm_w_out[j], update_ctx)
        else:
            lam_init = 0.8 - 0.6 * math.exp(-0.3 * i)
            ox, oc = diff_attn_mixer(hx, hc, diff_w_in[j], diff_w_out[j], diff_q_g[j], diff_k_g[j],
                                     diff_lq1[j], diff_lk1[j], diff_lq2[j], diff_lk2[j], diff_subln_g[j],
                                     lam_init, cos, sin, update_ctx)
        x = x + mx[2] * ox
        hx2 = modulate(rms_norm(x, norm_g[i, 1]), mx[3], mx[4])
        x = x + mx[5] * swiglu(hx2, ffn_w_gu[i], ffn_w_down[i])
        if update_ctx:
            ctx = ctx + mc[2] * oc
            hc2 = modulate(rms_norm(ctx, norm_g[i, 1]), mc[3], mc[4])
            ctx = ctx + mc[5] * swiglu(hc2, ffn_w_gu[i], ffn_w_down[i])
    return x
```

```python
import functools
import math

import jax
import jax.numpy as jnp
from jax import lax
from jax.experimental import pallas as pl
from jax.experimental.pallas import tpu as pltpu

F32 = jnp.float32
BF16 = jnp.bfloat16

D_MODEL = 2048
BATCH = 4
SEQ = 4096
DEPTH = 4
GRID_W = 64
CTX_LEN = 256
N_MIXERS = 2

MLSTM_HEADS = 8
MLSTM_DK = D_MODEL // (2 * MLSTM_HEADS)
MLSTM_DV = D_MODEL // MLSTM_HEADS
GATE_CAP = 15.0
MLSTM_QK = MLSTM_HEADS * MLSTM_DK
MLSTM_MAIN = 2 * MLSTM_QK + 2 * D_MODEL
N_GATES = 4 * MLSTM_HEADS

DIFF_HEADS = 8
DIFF_DH = D_MODEL // (2 * DIFF_HEADS)
ROPE_BASE = 10000.0

FFN_HIDDEN = -(-(8 * D_MODEL) // (3 * 256)) * 256
EPS = 1e-6

N_LAT = BATCH * SEQ
N_CTX = BATCH * CTX_LEN
NT = N_LAT + N_CTX

LANES = 128
TM = 512
TN = 512
TH = 512
CHUNK = 256
TQ = 512
TK = 512
VMEM_LIMIT = 56 * 1024 * 1024

NT_DIMS = (((1,), (1,)), ((), ()))
TN_DIMS = (((0,), (0,)), ((), ()))


def _params(*sem):
    return pltpu.CompilerParams(dimension_semantics=sem, vmem_limit_bytes=VMEM_LIMIT)


def _mod_row(i):
    return jnp.minimum(i // (SEQ // TM), BATCH)


def _mod_spec(k):
    return pl.BlockSpec((None, 1, D_MODEL), lambda i, j: (_mod_row(i), 0, k))


def _norm_modulate(x, g, shift, scale):
    y = x * lax.rsqrt(jnp.mean(x * x, axis=-1, keepdims=True) + EPS) * g
    return y * (1.0 + scale) + shift


def _ada_kernel(c_ref, w_ref, b_ref, o_ref):
    c = c_ref[...]
    a = (c * jax.nn.sigmoid(c)).astype(BF16)
    o_ref[...] = jnp.dot(a, w_ref[...].astype(BF16), preferred_element_type=F32) + b_ref[...]


def _ada(cc, ada_w, ada_b):
    tn = 1024
    return pl.pallas_call(
        _ada_kernel,
        grid=(DEPTH, 6 * D_MODEL // tn),
        in_specs=[
            pl.BlockSpec((8, D_MODEL), lambda l, j: (0, 0)),
            pl.BlockSpec((None, D_MODEL, tn), lambda l, j: (l, 0, j)),
            pl.BlockSpec((None, 1, tn), lambda l, j: (l, 0, j)),
        ],
        out_specs=pl.BlockSpec((None, 8, tn), lambda l, j: (l, 0, j)),
        out_shape=jax.ShapeDtypeStruct((DEPTH, 8, 6 * D_MODEL), F32),
        compiler_params=_params("arbitrary", "arbitrary"),
        name="ada_mod",
    )(cc, ada_w, ada_b.reshape(DEPTH, 1, 6 * D_MODEL))


def _inproj_mlstm_kernel(x_ref, g_ref, shift_ref, scale_ref, w_ref, wg_ref, out_ref, gate_ref, h_ref):
    j = pl.program_id(1)

    @pl.when(j == 0)
    def _():
        h = _norm_modulate(x_ref[...], g_ref[...], shift_ref[...], scale_ref[...]).astype(BF16)
        h_ref[...] = h
        gate_ref[...] = jnp.dot(h, wg_ref[...], preferred_element_type=F32)

    acc = jnp.dot(h_ref[...], w_ref[...], preferred_element_type=F32)
    qscale = jnp.where(j < MLSTM_QK // TN, MLSTM_DK ** -0.5, 1.0).astype(F32)
    out_ref[...] = (acc * qscale).astype(BF16)


def _inproj_mlstm(xs, g, mods, w_main, w_gate):
    return pl.pallas_call(
        _inproj_mlstm_kernel,
        grid=(NT // TM, MLSTM_MAIN // TN),
        in_specs=[
            pl.BlockSpec((TM, D_MODEL), lambda i, j: (i, 0)),
            pl.BlockSpec((1, D_MODEL), lambda i, j: (0, 0)),
            _mod_spec(0),
            _mod_spec(1),
            pl.BlockSpec((D_MODEL, TN), lambda i, j: (0, j)),
            pl.BlockSpec((D_MODEL, LANES), lambda i, j: (0, 0)),
        ],
        out_specs=[
            pl.BlockSpec((TM, TN), lambda i, j: (i, j)),
            pl.BlockSpec((TM, LANES), lambda i, j: (i, 0)),
        ],
        out_shape=[
            jax.ShapeDtypeStruct((NT, MLSTM_MAIN), BF16),
            jax.ShapeDtypeStruct((NT, LANES), F32),
        ],
        scratch_shapes=[pltpu.VMEM((TM, D_MODEL), BF16)],
        compiler_params=_params("arbitrary", "arbitrary"),
        name="inproj_mlstm",
    )(xs, g, mods, mods, w_main, w_gate)


def _mlstm_scan_kernel(*refs, reverse, final):
    if final:
        (q_ref, k_ref, v_ref, gate_ref, gb_ref, o_ref, hf_ref, hg_ref,
         out_ref, c_ref, n_ref, m_ref) = refs
    else:
        q_ref, k_ref, v_ref, gate_ref, gb_ref, out_ref, c_ref, n_ref, m_ref = refs
    L = CHUNK

    @pl.when(pl.program_id(1) == 0)
    def _():
        c_ref[...] = jnp.zeros_like(c_ref)
        n_ref[...] = jnp.zeros_like(n_ref)
        m_ref[...] = jnp.zeros_like(m_ref)

    gg = GATE_CAP * jnp.tanh((gate_ref[...] + gb_ref[...]) / GATE_CAP)
    lsig = jax.nn.log_sigmoid(gg)
    t_idx = lax.broadcasted_iota(jnp.int32, (L, L), 0)
    s_idx = lax.broadcasted_iota(jnp.int32, (L, L), 1)
    mask = (s_idx >= t_idx) if reverse else (s_idx <= t_idx)
    bcum = jnp.dot(mask.astype(F32), lsig, precision=lax.Precision.HIGHEST,
                   preferred_element_type=F32)
    bal = pltpu.roll(bcum, LANES - MLSTM_HEADS, 1)
    r = gg - bal
    r_t = r.T
    base = 2 * MLSTM_HEADS if reverse else 0
    last = 0 if reverse else L - 1

    for h in range(MLSTM_HEADS):
        col = base + h
        b_col = bal[:, col:col + 1]
        r_col = r[:, col:col + 1]
        r_row = r_t[col:col + 1, :]
        b_last = bal[last:last + 1, col:col + 1]
        m = m_ref[h]

        dmat = jnp.where(mask, b_col + r_row, -jnp.inf)
        inter = b_col + m
        m_t = jnp.maximum(inter, jnp.max(dmat, axis=-1, keepdims=True))
        w_intra = jnp.exp(dmat - m_t)
        w_inter = jnp.exp(inter - m_t)

        qh = q_ref[:, h * MLSTM_DK:(h + 1) * MLSTM_DK]
        kh = k_ref[:, h * MLSTM_DK:(h + 1) * MLSTM_DK]
        vh = v_ref[:, h * MLSTM_DV:(h + 1) * MLSTM_DV]
        s = lax.dot_general(qh, kh, NT_DIMS, preferred_element_type=F32) * w_intra
        c_old = c_ref[h]
        num = (w_inter * jnp.dot(qh, c_old.astype(BF16), preferred_element_type=F32)
               + jnp.dot(s.astype(BF16), vh, preferred_element_type=F32))
        n_old = n_ref[h]
        qn = jnp.sum(qh.astype(F32) * n_old.astype(BF16).astype(F32), axis=-1, keepdims=True)
        den = w_inter * qn + jnp.sum(s, axis=-1, keepdims=True)
        hout = num / jnp.maximum(jnp.abs(den), jnp.exp(-m_t))

        a_col = b_last + r_col
        m_new = jnp.maximum(b_last + m, jnp.max(a_col, axis=0, keepdims=True))
        wk = jnp.exp(a_col - m_new)
        dec = jnp.exp(b_last + m - m_new)
        kw = kh.astype(F32) * wk
        c_ref[h] = dec * c_old + lax.dot_general(kw.astype(BF16), vh, TN_DIMS,
                                                 preferred_element_type=F32)
        n_ref[h] = dec * n_old + jnp.sum(kw, axis=0, keepdims=True)
        m_ref[h] = m_new

        sl = slice(h * MLSTM_DV, (h + 1) * MLSTM_DV)
        if final:
            hs = hf_ref[:, sl] + hout
            y = hs * lax.rsqrt(jnp.mean(hs * hs, axis=-1, keepdims=True) + EPS) * hg_ref[:, sl]
            out_ref[:, sl] = (y * jax.nn.sigmoid(o_ref[:, sl].astype(F32))).astype(BF16)
        else:
            out_ref[:, sl] = hout


def _mlstm_scan(qkvo, gates, gate_b, head_g, hf, *, reverse):
    final = hf is not None
    n_lat_chunks = SEQ // CHUNK
    ctx_blk0 = N_LAT // CHUNK

    def row_blk(b, c):
        j = (n_lat_chunks - c) if reverse else (c - 1)
        return jnp.where(c == 0, ctx_blk0 + b, b * n_lat_chunks + j)

    in_specs = [
        pl.BlockSpec((CHUNK, MLSTM_QK), lambda b, c: (row_blk(b, c), 0)),
        pl.BlockSpec((CHUNK, MLSTM_QK), lambda b, c: (row_blk(b, c), 1)),
        pl.BlockSpec((CHUNK, D_MODEL), lambda b, c: (row_blk(b, c), 1)),
        pl.BlockSpec((CHUNK, LANES), lambda b, c: (row_blk(b, c), 0)),
        pl.BlockSpec((1, LANES), lambda b, c: (0, 0)),
    ]
    args = [qkvo, qkvo, qkvo, gates, gate_b]
    if final:
        in_specs += [
            pl.BlockSpec((CHUNK, D_MODEL), lambda b, c: (row_blk(b, c), 2)),
            pl.BlockSpec((CHUNK, D_MODEL), lambda b, c: (row_blk(b, c), 0)),
            pl.BlockSpec((1, D_MODEL), lambda b, c: (0, 0)),
        ]
        args += [qkvo, hf, head_g]
    return pl.pallas_call(
        functools.partial(_mlstm_scan_kernel, reverse=reverse, final=final),
        grid=(BATCH, 1 + n_lat_chunks),
        in_specs=in_specs,
        out_specs=pl.BlockSpec((CHUNK, D_MODEL), lambda b, c: (row_blk(b, c), 0)),
        out_shape=jax.ShapeDtypeStruct((NT, D_MODEL), BF16 if final else F32),
        scratch_shapes=[
            pltpu.VMEM((MLSTM_HEADS, MLSTM_DK, MLSTM_DV), F32),
            pltpu.VMEM((MLSTM_HEADS, 1, MLSTM_DK), F32),
            pltpu.VMEM((MLSTM_HEADS, 1, 1), F32),
        ],
        compiler_params=_params("arbitrary", "arbitrary"),
        name="mlstm_scan_bwd" if reverse else "mlstm_scan_fwd",
    )(*args)


def _inproj_diff_kernel(x_ref, g_ref, shift_ref, scale_ref, w_ref, qkg_ref, cos_ref, sin_ref,
                        out_ref, h_ref):
    j = pl.program_id(1)
    n_qk_blocks = 2 * D_MODEL // TN

    @pl.when(j == 0)
    def _():
        h_ref[...] = _norm_modulate(x_ref[...], g_ref[...], shift_ref[...], scale_ref[...]).astype(BF16)

    acc = jnp.dot(h_ref[...], w_ref[...], preferred_element_type=F32)

    @pl.when(j < n_qk_blocks)
    def _():
        cos = cos_ref[...]
        sin = sin_ref[...]
        lane = lax.broadcasted_iota(jnp.int32, (TM, LANES), 1)
        first = (lane & (DIFF_DH // 4)) == 0
        for grp in range(TN // LANES):
            sl = slice(grp * LANES, (grp + 1) * LANES)
            xg = acc[:, sl]
            y = xg * lax.rsqrt(jnp.mean(xg * xg, axis=-1, keepdims=True) + EPS) * qkg_ref[:, sl]
            partner = jnp.where(first, pltpu.roll(y, LANES - DIFF_DH // 4, 1),
                                pltpu.roll(y, DIFF_DH // 4, 1))
            out_ref[:, sl] = (y * cos + partner * sin).astype(BF16)

    @pl.when(j >= n_qk_blocks)
    def _():
        out_ref[...] = acc.astype(BF16)


def _inproj_diff(xs, g, mods, w, qkg, cos_t, sin_t):
    n_qk_blocks = 2 * D_MODEL // TN
    lat_tiles = SEQ // TM

    def rope_blk(i, j):
        return (jnp.where(i < N_LAT // TM, i % lat_tiles, lat_tiles), 0)

    return pl.pallas_call(
        _inproj_diff_kernel,
        grid=(NT // TM, 3 * D_MODEL // TN),
        in_specs=[
            pl.BlockSpec((TM, D_MODEL), lambda i, j: (i, 0)),
            pl.BlockSpec((1, D_MODEL), lambda i, j: (0, 0)),
            _mod_spec(0),
            _mod_spec(1),
            pl.BlockSpec((D_MODEL, TN), lambda i, j: (0, j)),
            pl.BlockSpec((1, TN), lambda i, j: (0, jnp.minimum(j, n_qk_blocks - 1))),
            pl.BlockSpec((TM, LANES), rope_blk),
            pl.BlockSpec((TM, LANES), rope_blk),
        ],
        out_specs=pl.BlockSpec((TM, TN), lambda i, j: (i, j)),
        out_shape=jax.ShapeDtypeStruct((NT, 3 * D_MODEL), BF16),
        scratch_shapes=[pltpu.VMEM((TM, D_MODEL), BF16)],
        compiler_params=_params("arbitrary", "arbitrary"),
        name="inproj_diff",
    )(xs, g, mods, mods, w, qkg, cos_t, sin_t)


def _attn_kernel(*refs, n_lat_chunks, lam_init):
    if n_lat_chunks:
        (q_ref, kl_ref, vl_ref, kc_ref, vc_ref, lq1_ref, lk1_ref, lq2_ref, lk2_ref, sg_ref,
         out_ref, acc_ref, m_ref, l_ref) = refs
    else:
        (q_ref, kc_ref, vc_ref, lq1_ref, lk1_ref, lq2_ref, lk2_ref, sg_ref, _,
         out_ref, acc_ref, m_ref, l_ref) = refs
    scale = DIFF_DH ** -0.5
    q = q_ref[...]
    m_ref[...] = jnp.full_like(m_ref, -jnp.inf)
    l_ref[...] = jnp.zeros_like(l_ref)
    acc_ref[...] = jnp.zeros_like(acc_ref)

    def update(kblk, vblk):
        for t in range(2):
            sl = slice(t * DIFF_DH, (t + 1) * DIFF_DH)
            s = lax.dot_general(q[:, sl], kblk[:, sl], NT_DIMS, preferred_element_type=F32) * scale
            m_old = m_ref[t]
            m_new = jnp.maximum(m_old, jnp.max(s, axis=-1, keepdims=True))
            p = jnp.exp(s - m_new)
            alpha = jnp.exp(m_old - m_new)
            l_ref[t] = alpha * l_ref[t] + jnp.sum(p, axis=-1, keepdims=True)
            acc_ref[t] = alpha * acc_ref[t] + jnp.dot(p.astype(BF16), vblk,
                                                      preferred_element_type=F32)
            m_ref[t] = m_new

    if n_lat_chunks:
        def body(c, carry):
            off = pl.multiple_of(c * TK, TK)
            update(kl_ref[pl.ds(off, TK), :], vl_ref[pl.ds(off, TK), :])
            return carry
        lax.fori_loop(0, n_lat_chunks, body, 0)
    update(kc_ref[...], vc_ref[...])

    lam = (jnp.exp(jnp.sum(lq1_ref[...] * lk1_ref[...], axis=-1, keepdims=True))
           - jnp.exp(jnp.sum(lq2_ref[...] * lk2_ref[...], axis=-1, keepdims=True)) + lam_init)
    o = acc_ref[0] / l_ref[0] - lam * (acc_ref[1] / l_ref[1])
    y = o * lax.rsqrt(jnp.mean(o * o, axis=-1, keepdims=True) + EPS) * sg_ref[...]
    out_ref[...] = (y * (1.0 - lam_init)).astype(BF16)


def _attention(qkv, lams, subln_g, lam_init, prev_out):
    dv = 2 * DIFF_DH
    kcol = D_MODEL // dv
    vcol = 2 * D_MODEL // dv
    ctx_blk0 = N_LAT // CTX_LEN
    small = [pl.BlockSpec((1, DIFF_DH), lambda b, h, i: (0, 0))] * 4
    small.append(pl.BlockSpec((1, dv), lambda b, h, i: (0, 0)))
    ctx_specs = [
        pl.BlockSpec((CTX_LEN, dv), lambda b, h, i: (ctx_blk0 + b, kcol + h)),
        pl.BlockSpec((CTX_LEN, dv), lambda b, h, i: (ctx_blk0 + b, vcol + h)),
    ]
    if prev_out is None:
        tq, nq = TQ, SEQ // TQ
        q_map = lambda b, h, i: (b * nq + i, h)
        in_specs = [
            pl.BlockSpec((tq, dv), q_map),
            pl.BlockSpec((SEQ, dv), lambda b, h, i: (b, kcol + h)),
            pl.BlockSpec((SEQ, dv), lambda b, h, i: (b, vcol + h)),
        ] + ctx_specs + small
        args = [qkv, qkv, qkv, qkv, qkv] + list(lams) + [subln_g]
        aliases = {}
        n_lat_chunks = SEQ // TK
    else:
        tq, nq = CTX_LEN, 1
        q_map = lambda b, h, i: (ctx_blk0 + b, h)
        in_specs = [pl.BlockSpec((tq, dv), q_map)] + ctx_specs + small
        in_specs.append(pl.BlockSpec(memory_space=pl.ANY))
        args = [qkv, qkv, qkv] + list(lams) + [subln_g, prev_out]
        aliases = {len(args) - 1: 0}
        n_lat_chunks = 0
    return pl.pallas_call(
        functools.partial(_attn_kernel, n_lat_chunks=n_lat_chunks, lam_init=lam_init),
        grid=(BATCH, DIFF_HEADS, nq),
        in_specs=in_specs,
        out_specs=pl.BlockSpec((tq, dv), q_map),
        out_shape=jax.ShapeDtypeStruct((NT, D_MODEL), BF16),
        scratch_shapes=[
            pltpu.VMEM((2, tq, dv), F32),
            pltpu.VMEM((2, tq, 1), F32),
            pltpu.VMEM((2, tq, 1), F32),
        ],
        input_output_aliases=aliases,
        compiler_params=_params("arbitrary", "arbitrary", "arbitrary"),
        name="diff_attn_ctx" if n_lat_chunks == 0 else "diff_attn",
    )(*args)


def _outproj_kernel(a_ref, w_ref, x_ref, gate_ref, out_ref):
    y = jnp.dot(a_ref[...], w_ref[...], preferred_element_type=F32)
    out_ref[...] = x_ref[...] + gate_ref[...] * y


def _outproj(a, w, xs, mods, n_tiles):
    nj = D_MODEL // TN
    return pl.pallas_call(
        _outproj_kernel,
        grid=(n_tiles, nj),
        in_specs=[
            pl.BlockSpec((TM, D_MODEL), lambda i, j: (i, 0)),
            pl.BlockSpec((D_MODEL, TN), lambda i, j: (0, j)),
            pl.BlockSpec((TM, TN), lambda i, j: (i, j)),
            pl.BlockSpec((None, 1, TN), lambda i, j: (_mod_row(i), 0, 2 * nj + j)),
        ],
        out_specs=pl.BlockSpec((TM, TN), lambda i, j: (i, j)),
        out_shape=jax.ShapeDtypeStruct((NT, D_MODEL), F32),
        input_output_aliases={2: 0},
        compiler_params=_params("arbitrary", "arbitrary"),
        name="outproj",
    )(a, w, xs, mods)


def _ffn_kernel(x_ref, g_ref, shift_ref, scale_ref, gate_ref, wg_ref, wu_ref, wd_ref, out_ref, h_ref):
    k = pl.program_id(1)

    @pl.when(k == 0)
    def _():
        h_ref[...] = _norm_modulate(x_ref[...], g_ref[...], shift_ref[...], scale_ref[...]).astype(BF16)

    h = h_ref[...]
    gt = jnp.dot(h, wg_ref[...], preferred_element_type=F32)
    up = jnp.dot(h, wu_ref[...], preferred_element_type=F32)
    act = (gt * jax.nn.sigmoid(gt) * up).astype(BF16)
    part = jnp.dot(act, wd_ref[...], preferred_element_type=F32)

    @pl.when(k == 0)
    def _():
        out_ref[...] = part

    @pl.when(k > 0)
    def _():
        out_ref[...] += part

    @pl.when(k == pl.num_programs(1) - 1)
    def _():
        out_ref[...] = x_ref[...] + gate_ref[...] * out_ref[...]


def _ffn(xs, g, mods, w_gu, w_down, n_tiles):
    nh = FFN_HIDDEN // TH
    return pl.pallas_call(
        _ffn_kernel,
        grid=(n_tiles, nh),
        in_specs=[
            pl.BlockSpec((TM, D_MODEL), lambda i, k: (i, 0)),
            pl.BlockSpec((1, D_MODEL), lambda i, k: (0, 0)),
            _mod_spec(3),
            _mod_spec(4),
            _mod_spec(5),
            pl.BlockSpec((D_MODEL, TH), lambda i, k: (0, k)),
            pl.BlockSpec((D_MODEL, TH), lambda i, k: (0, nh + k)),
            pl.BlockSpec((TH, D_MODEL), lambda i, k: (k, 0)),
        ],
        out_specs=pl.BlockSpec((TM, D_MODEL), lambda i, k: (i, 0)),
        out_shape=jax.ShapeDtypeStruct((n_tiles * TM, D_MODEL), F32),
        scratch_shapes=[pltpu.VMEM((TM, D_MODEL), BF16)],
        compiler_params=_params("arbitrary", "arbitrary"),
        name="ffn",
    )(xs, g, mods, mods, mods, w_gu, w_gu, w_down)


def _rope_tables():
    n_freq = DIFF_DH // 4
    pos = jnp.arange(SEQ)
    freqs = ROPE_BASE ** (-jnp.arange(n_freq, dtype=F32) / n_freq)
    ang = jnp.stack([pos // GRID_W, pos % GRID_W], axis=-1).astype(F32)[:, :, None] * freqs
    cos, sin = jnp.cos(ang), jnp.sin(ang)
    cos_t = jnp.concatenate([cos[:, 0], cos[:, 0], cos[:, 1], cos[:, 1]], axis=-1)
    sin_t = jnp.concatenate([-sin[:, 0], sin[:, 0], -sin[:, 1], sin[:, 1]], axis=-1)
    cos_t = jnp.concatenate([cos_t, jnp.ones((TM, LANES), F32)], axis=0)
    sin_t = jnp.concatenate([sin_t, jnp.zeros((TM, LANES), F32)], axis=0)
    return cos_t, sin_t


def kernel(x, c, ctx, c_ctx, ada_w, ada_b, norm_g, mlstm_w_in, mlstm_gate_b, mlstm_head_g, mlstm_w_out,
           diff_w_in, diff_w_out, diff_q_g, diff_k_g, diff_lq1, diff_lk1, diff_lq2, diff_lk2, diff_subln_g,
           ffn_w_gu, ffn_w_down):
    assert x.shape == (BATCH, SEQ, D_MODEL) and ctx.shape == (BATCH, CTX_LEN, D_MODEL)
    xs = jnp.concatenate([x.reshape(N_LAT, D_MODEL), ctx.reshape(N_CTX, D_MODEL)], axis=0)
    cc = jnp.concatenate([c, c_ctx[None], jnp.zeros((8 - BATCH - 1, D_MODEL), F32)], axis=0)
    mods_all = _ada(cc, ada_w, ada_b).reshape(DEPTH, 8, 1, 6 * D_MODEL)
    cos_t, sin_t = _rope_tables()
    row = lambda v: v.reshape(1, -1)

    for i in range(DEPTH):
        last = i == DEPTH - 1
        n_tiles = (N_LAT if last else NT) // TM
        mods = mods_all[i]
        j = i // N_MIXERS
        if i % N_MIXERS == 0:
            w_in = mlstm_w_in[j]
            w_main = w_in[:, :MLSTM_MAIN].astype(BF16)
            w_gate = jnp.pad(w_in[:, MLSTM_MAIN:], ((0, 0), (0, LANES - N_GATES))).astype(BF16)
            gate_b = jnp.pad(mlstm_gate_b[j], (0, LANES - N_GATES)).reshape(1, LANES)
            qkvo, gates = _inproj_mlstm(xs, row(norm_g[i, 0]), mods, w_main, w_gate)
            hf = _mlstm_scan(qkvo, gates, gate_b, None, None, reverse=False)
            a = _mlstm_scan(qkvo, gates, gate_b, row(mlstm_head_g[j]), hf, reverse=True)
            w_out = mlstm_w_out[j].astype(BF16)
        else:
            lam_init = 0.8 - 0.6 * math.exp(-0.3 * i)
            qkg = jnp.concatenate([jnp.tile(diff_q_g[j], 2 * DIFF_HEADS),
                                   jnp.tile(diff_k_g[j], 2 * DIFF_HEADS)]).reshape(1, 2 * D_MODEL)
            qkv = _inproj_diff(xs, row(norm_g[i, 0]), mods, diff_w_in[j].astype(BF16), qkg, cos_t, sin_t)
            lams = [row(diff_lq1[j]), row(diff_lk1[j]), row(diff_lq2[j]), row(diff_lk2[j])]
            a = _attention(qkv, lams, row(diff_subln_g[j]), lam_init, None)
            if not last:
                a = _attention(qkv, lams, row(diff_subln_g[j]), lam_init, a)
            w_out = diff_w_out[j].astype(BF16)
        xs = _outproj(a, w_out, xs, mods, n_tiles)
        xs = _ffn(xs, row(norm_g[i, 1]), mods, ffn_w_gu[i].astype(BF16), ffn_w_down[i].astype(BF16), n_tiles)
    return xs.reshape(BATCH, SEQ, D_MODEL)
```

```python
import functools
import math

import jax
import jax.numpy as jnp
from jax import lax
from jax.experimental import pallas as pl
from jax.experimental.pallas import tpu as pltpu

F32 = jnp.float32
BF16 = jnp.bfloat16

D_MODEL = 2048
BATCH = 4
SEQ = 4096
DEPTH = 4
GRID_W = 64
CTX_LEN = 256
N_MIXERS = 2

MLSTM_HEADS = 8
MLSTM_DK = D_MODEL // (2 * MLSTM_HEADS)
MLSTM_DV = D_MODEL // MLSTM_HEADS
GATE_CAP = 15.0
MLSTM_QK = MLSTM_HEADS * MLSTM_DK
MLSTM_MAIN = 2 * MLSTM_QK + 2 * D_MODEL
N_GATES = 4 * MLSTM_HEADS

DIFF_HEADS = 8
DIFF_DH = D_MODEL // (2 * DIFF_HEADS)
ROPE_BASE = 10000.0

FFN_HIDDEN = -(-(8 * D_MODEL) // (3 * 256)) * 256
EPS = 1e-6

N_LAT = BATCH * SEQ
N_CTX = BATCH * CTX_LEN
NT = N_LAT + N_CTX

LANES = 128
TM = 1024
TN = 1024
TMF = 512
TH = 512
CHUNK = 256
TQ = 512
TK = 512
VMEM_LIMIT = 56 * 1024 * 1024

NT_DIMS = (((1,), (1,)), ((), ()))
TN_DIMS = (((0,), (0,)), ((), ()))


def _params(*sem):
    return pltpu.CompilerParams(dimension_semantics=sem, vmem_limit_bytes=VMEM_LIMIT)


def _mod_row(i, tm):
    return jnp.minimum(i // (SEQ // tm), BATCH)


def _mod_spec(k, tm=TM):
    return pl.BlockSpec((None, 1, D_MODEL), lambda i, j: (_mod_row(i, tm), 0, k))


def _norm_modulate(x, g, shift, scale):
    y = x * lax.rsqrt(jnp.mean(x * x, axis=-1, keepdims=True) + EPS) * g
    return y * (1.0 + scale) + shift


def _ada_kernel(c_ref, w_ref, b_ref, o_ref):
    c = c_ref[...]
    a = (c * jax.nn.sigmoid(c)).astype(BF16)
    o_ref[...] = jnp.dot(a, w_ref[...].astype(BF16), preferred_element_type=F32) + b_ref[...]


def _ada(cc, ada_w, ada_b):
    tn = 1024
    return pl.pallas_call(
        _ada_kernel,
        grid=(DEPTH, 6 * D_MODEL // tn),
        in_specs=[
            pl.BlockSpec((8, D_MODEL), lambda l, j: (0, 0)),
            pl.BlockSpec((None, D_MODEL, tn), lambda l, j: (l, 0, j)),
            pl.BlockSpec((None, 1, tn), lambda l, j: (l, 0, j)),
        ],
        out_specs=pl.BlockSpec((None, 8, tn), lambda l, j: (l, 0, j)),
        out_shape=jax.ShapeDtypeStruct((DEPTH, 8, 6 * D_MODEL), F32),
        compiler_params=_params("arbitrary", "arbitrary"),
        name="ada_mod",
    )(cc, ada_w, ada_b.reshape(DEPTH, 1, 6 * D_MODEL))


def _inproj_mlstm_kernel(x_ref, g_ref, shift_ref, scale_ref, w_ref, wg_ref, out_ref, gate_ref, h_ref):
    j = pl.program_id(1)

    @pl.when(j == 0)
    def _():
        h = _norm_modulate(x_ref[...], g_ref[...], shift_ref[...], scale_ref[...]).astype(BF16)
        h_ref[...] = h
        gate_ref[...] = jnp.dot(h, wg_ref[...], preferred_element_type=F32)

    acc = jnp.dot(h_ref[...], w_ref[...], preferred_element_type=F32)
    qscale = jnp.where(j < MLSTM_QK // TN, MLSTM_DK ** -0.5, 1.0).astype(F32)
    out_ref[...] = (acc * qscale).astype(BF16)


def _inproj_mlstm(xs, g, mods, w_main, w_gate):
    return pl.pallas_call(
        _inproj_mlstm_kernel,
        grid=(NT // TM, MLSTM_MAIN // TN),
        in_specs=[
            pl.BlockSpec((TM, D_MODEL), lambda i, j: (i, 0)),
            pl.BlockSpec((1, D_MODEL), lambda i, j: (0, 0)),
            _mod_spec(0),
            _mod_spec(1),
            pl.BlockSpec((D_MODEL, TN), lambda i, j: (0, j)),
            pl.BlockSpec((D_MODEL, LANES), lambda i, j: (0, 0)),
        ],
        out_specs=[
            pl.BlockSpec((TM, TN), lambda i, j: (i, j)),
            pl.BlockSpec((TM, LANES), lambda i, j: (i, 0)),
        ],
        out_shape=[
            jax.ShapeDtypeStruct((NT, MLSTM_MAIN), BF16),
            jax.ShapeDtypeStruct((NT, LANES), F32),
        ],
        scratch_shapes=[pltpu.VMEM((TM, D_MODEL), BF16)],
        compiler_params=_params("arbitrary", "arbitrary"),
        name="inproj_mlstm",
    )(xs, g, mods, mods, w_main, w_gate)


def _mlstm_scan_kernel(*refs, reverse, final):
    if final:
        (q_ref, k_ref, v_ref, gate_ref, gb_ref, o_ref, hf_ref, hg_ref,
         out_ref, c_ref, n_ref, m_ref) = refs
    else:
        q_ref, k_ref, v_ref, gate_ref, gb_ref, out_ref, c_ref, n_ref, m_ref = refs
    L = CHUNK

    @pl.when(pl.program_id(1) == 0)
    def _():
        c_ref[...] = jnp.zeros_like(c_ref)
        n_ref[...] = jnp.zeros_like(n_ref)
        m_ref[...] = jnp.zeros_like(m_ref)

    gg = GATE_CAP * jnp.tanh((gate_ref[...] + gb_ref[...]) / GATE_CAP)
    lsig = jax.nn.log_sigmoid(gg)
    t_idx = lax.broadcasted_iota(jnp.int32, (L, L), 0)
    s_idx = lax.broadcasted_iota(jnp.int32, (L, L), 1)
    mask = (s_idx >= t_idx) if reverse else (s_idx <= t_idx)
    bcum = jnp.dot(mask.astype(F32), lsig, precision=lax.Precision.HIGHEST,
                   preferred_element_type=F32)
    bal = pltpu.roll(bcum, LANES - MLSTM_HEADS, 1)
    r = gg - bal
    r_t = r.T
    base = 2 * MLSTM_HEADS if reverse else 0
    last = 0 if reverse else L - 1

    for h in range(MLSTM_HEADS):
        col = base + h
        b_col = bal[:, col:col + 1]
        r_col = r[:, col:col + 1]
        r_row = r_t[col:col + 1, :]
        b_last = bal[last:last + 1, col:col + 1]
        m = m_ref[h]

        dmat = jnp.where(mask, b_col + r_row, -jnp.inf)
        inter = b_col + m
        m_t = jnp.maximum(inter, jnp.max(dmat, axis=-1, keepdims=True))
        w_intra = jnp.exp(dmat - m_t)
        w_inter = jnp.exp(inter - m_t)

        qh = q_ref[:, h * MLSTM_DK:(h + 1) * MLSTM_DK]
        kh = k_ref[:, h * MLSTM_DK:(h + 1) * MLSTM_DK]
        vh = v_ref[:, h * MLSTM_DV:(h + 1) * MLSTM_DV]
        s = lax.dot_general(qh, kh, NT_DIMS, preferred_element_type=F32) * w_intra
        c_old = c_ref[h]
        num = (w_inter * jnp.dot(qh, c_old.astype(BF16), preferred_element_type=F32)
               + jnp.dot(s.astype(BF16), vh, preferred_element_type=F32))
        n_old = n_ref[h]
        qn = jnp.sum(qh.astype(F32) * n_old.astype(BF16).astype(F32), axis=-1, keepdims=True)
        den = w_inter * qn + jnp.sum(s, axis=-1, keepdims=True)
        hout = num / jnp.maximum(jnp.abs(den), jnp.exp(-m_t))

        a_col = b_last + r_col
        m_new = jnp.maximum(b_last + m, jnp.max(a_col, axis=0, keepdims=True))
        wk = jnp.exp(a_col - m_new)
        dec = jnp.exp(b_last + m - m_new)
        kw = kh.astype(F32) * wk
        c_ref[h] = dec * c_old + lax.dot_general(kw.astype(BF16), vh, TN_DIMS,
                                                 preferred_element_type=F32)
        n_ref[h] = dec * n_old + jnp.sum(kw, axis=0, keepdims=True)
        m_ref[h] = m_new

        sl = slice(h * MLSTM_DV, (h + 1) * MLSTM_DV)
        if final:
            hs = hf_ref[:, sl] + hout
            y = hs * lax.rsqrt(jnp.mean(hs * hs, axis=-1, keepdims=True) + EPS) * hg_ref[:, sl]
            out_ref[:, sl] = (y * jax.nn.sigmoid(o_ref[:, sl].astype(F32))).astype(BF16)
        else:
            out_ref[:, sl] = hout


def _mlstm_scan(qkvo, gates, gate_b, head_g, hf, *, reverse):
    final = hf is not None
    n_lat_chunks = SEQ // CHUNK
    ctx_blk0 = N_LAT // CHUNK

    def row_blk(b, c):
        j = (n_lat_chunks - c) if reverse else (c - 1)
        return jnp.where(c == 0, ctx_blk0 + b, b * n_lat_chunks + j)

    in_specs = [
        pl.BlockSpec((CHUNK, MLSTM_QK), lambda b, c: (row_blk(b, c), 0)),
        pl.BlockSpec((CHUNK, MLSTM_QK), lambda b, c: (row_blk(b, c), 1)),
        pl.BlockSpec((CHUNK, D_MODEL), lambda b, c: (row_blk(b, c), 1)),
        pl.BlockSpec((CHUNK, LANES), lambda b, c: (row_blk(b, c), 0)),
        pl.BlockSpec((1, LANES), lambda b, c: (0, 0)),
    ]
    args = [qkvo, qkvo, qkvo, gates, gate_b]
    if final:
        in_specs += [
            pl.BlockSpec((CHUNK, D_MODEL), lambda b, c: (row_blk(b, c), 2)),
            pl.BlockSpec((CHUNK, D_MODEL), lambda b, c: (row_blk(b, c), 0)),
            pl.BlockSpec((1, D_MODEL), lambda b, c: (0, 0)),
        ]
        args += [qkvo, hf, head_g]
    return pl.pallas_call(
        functools.partial(_mlstm_scan_kernel, reverse=reverse, final=final),
        grid=(BATCH, 1 + n_lat_chunks),
        in_specs=in_specs,
        out_specs=pl.BlockSpec((CHUNK, D_MODEL), lambda b, c: (row_blk(b, c), 0)),
        out_shape=jax.ShapeDtypeStruct((NT, D_MODEL), BF16 if final else F32),
        scratch_shapes=[
            pltpu.VMEM((MLSTM_HEADS, MLSTM_DK, MLSTM_DV), F32),
            pltpu.VMEM((MLSTM_HEADS, 1, MLSTM_DK), F32),
            pltpu.VMEM((MLSTM_HEADS, 1, 1), F32),
        ],
        compiler_params=_params("arbitrary", "arbitrary"),
        name="mlstm_scan_bwd" if reverse else "mlstm_scan_fwd",
    )(*args)


def _inproj_diff_kernel(x_ref, g_ref, shift_ref, scale_ref, w_ref, qkg_ref, cos_ref, sin_ref,
                        out_ref, h_ref):
    j = pl.program_id(1)
    n_qk_blocks = 2 * D_MODEL // TN

    @pl.when(j == 0)
    def _():
        h_ref[...] = _norm_modulate(x_ref[...], g_ref[...], shift_ref[...], scale_ref[...]).astype(BF16)

    acc = jnp.dot(h_ref[...], w_ref[...], preferred_element_type=F32)

    @pl.when(j < n_qk_blocks)
    def _():
        cos = cos_ref[...]
        sin = sin_ref[...]
        for grp in range(TN // LANES):
            sl = slice(grp * LANES, (grp + 1) * LANES)
            xg = acc[:, sl]
            y = xg * lax.rsqrt(jnp.mean(xg * xg, axis=-1, keepdims=True) + EPS) * qkg_ref[:, sl]
            out_ref[:, sl] = (y * cos + pltpu.roll(y, LANES // 2, 1) * sin).astype(BF16)

    @pl.when(j >= n_qk_blocks)
    def _():
        out_ref[...] = acc.astype(BF16)


def _inproj_diff(xs, g, mods, w, qkg, cos_t, sin_t):
    n_qk_blocks = 2 * D_MODEL // TN
    lat_tiles = SEQ // TM

    def rope_blk(i, j):
        return (jnp.where(i < N_LAT // TM, i % lat_tiles, lat_tiles), 0)

    return pl.pallas_call(
        _inproj_diff_kernel,
        grid=(NT // TM, 3 * D_MODEL // TN),
        in_specs=[
            pl.BlockSpec((TM, D_MODEL), lambda i, j: (i, 0)),
            pl.BlockSpec((1, D_MODEL), lambda i, j: (0, 0)),
            _mod_spec(0),
            _mod_spec(1),
            pl.BlockSpec((D_MODEL, TN), lambda i, j: (0, j)),
            pl.BlockSpec((1, TN), lambda i, j: (0, jnp.minimum(j, n_qk_blocks - 1))),
            pl.BlockSpec((TM, LANES), rope_blk),
            pl.BlockSpec((TM, LANES), rope_blk),
        ],
        out_specs=pl.BlockSpec((TM, TN), lambda i, j: (i, j)),
        out_shape=jax.ShapeDtypeStruct((NT, 3 * D_MODEL), BF16),
        scratch_shapes=[pltpu.VMEM((TM, D_MODEL), BF16)],
        compiler_params=_params("arbitrary", "arbitrary"),
        name="inproj_diff",
    )(xs, g, mods, mods, w, qkg, cos_t, sin_t)


def _attn_kernel(*refs, n_lat_chunks, lam_init):
    if n_lat_chunks:
        (q_ref, kl_ref, vl_ref, kc_ref, vc_ref, lq1_ref, lk1_ref, lq2_ref, lk2_ref, sg_ref,
         out_ref, acc_ref, m_ref, l_ref) = refs
    else:
        (q_ref, kc_ref, vc_ref, lq1_ref, lk1_ref, lq2_ref, lk2_ref, sg_ref, _,
         out_ref, acc_ref, m_ref, l_ref) = refs
    tq = q_ref.shape[0]
    dv = 2 * DIFF_DH
    c2 = DIFF_DH ** -0.5 * math.log2(math.e)
    q = q_ref[...]

    def scores(kblk):
        return jnp.concatenate(
            [lax.dot_general(q[:, t * DIFF_DH:(t + 1) * DIFF_DH], kblk[:, t * DIFF_DH:(t + 1) * DIFF_DH],
                             NT_DIMS, preferred_element_type=F32) for t in range(2)], axis=0)

    s = scores(kc_ref[...])
    m0 = jnp.max(s, axis=-1, keepdims=True)
    p = jnp.exp2((s - m0) * c2)
    m_ref[...] = jnp.broadcast_to(m0, m_ref.shape)
    l_ref[...] = jnp.broadcast_to(jnp.sum(p, axis=-1, keepdims=True), l_ref.shape)
    acc_ref[...] = jnp.dot(p.astype(BF16), vc_ref[...], preferred_element_type=F32)

    if n_lat_chunks:
        def body(c, carry):
            off = pl.multiple_of(c * TK, TK)
            s = scores(kl_ref[pl.ds(off, TK), :])
            m_old = m_ref[...]
            m_new = jnp.maximum(m_old, jnp.max(s, axis=-1, keepdims=True))
            p = jnp.exp2((s - pltpu.repeat(m_new, TK // LANES, 1)) * c2)
            alpha = jnp.exp2((m_old - m_new) * c2)
            l_ref[...] = alpha * l_ref[...] + jnp.sum(p, axis=-1, keepdims=True)
            acc_ref[...] = (pltpu.repeat(alpha, dv // LANES, 1) * acc_ref[...]
                            + jnp.dot(p.astype(BF16), vl_ref[pl.ds(off, TK), :],
                                      preferred_element_type=F32))
            m_ref[...] = m_new
            return carry
        lax.fori_loop(0, n_lat_chunks, body, 0)

    lam = (jnp.exp(jnp.sum(lq1_ref[...] * lk1_ref[...], axis=-1, keepdims=True))
           - jnp.exp(jnp.sum(lq2_ref[...] * lk2_ref[...], axis=-1, keepdims=True)) + lam_init)
    o_all = acc_ref[...] / pltpu.repeat(l_ref[...], dv // LANES, 1)
    o = o_all[:tq] - lam * o_all[tq:]
    y = o * lax.rsqrt(jnp.mean(o * o, axis=-1, keepdims=True) + EPS) * sg_ref[...]
    out_ref[...] = (y * (1.0 - lam_init)).astype(BF16)


def _attention(qkv, lams, subln_g, lam_init, prev_out):
    dv = 2 * DIFF_DH
    kcol = D_MODEL // dv
    vcol = 2 * D_MODEL // dv
    ctx_blk0 = N_LAT // CTX_LEN
    small = [pl.BlockSpec((1, DIFF_DH), lambda b, h, i: (0, 0))] * 4
    small.append(pl.BlockSpec((1, dv), lambda b, h, i: (0, 0)))
    ctx_specs = [
        pl.BlockSpec((CTX_LEN, dv), lambda b, h, i: (ctx_blk0 + b, kcol + h)),
        pl.BlockSpec((CTX_LEN, dv), lambda b, h, i: (ctx_blk0 + b, vcol + h)),
    ]
    if prev_out is None:
        tq, nq = TQ, SEQ // TQ
        q_map = lambda b, h, i: (b * nq + i, h)
        in_specs = [
            pl.BlockSpec((tq, dv), q_map),
            pl.BlockSpec((SEQ, dv), lambda b, h, i: (b, kcol + h)),
            pl.BlockSpec((SEQ, dv), lambda b, h, i: (b, vcol + h)),
        ] + ctx_specs + small
        args = [qkv, qkv, qkv, qkv, qkv] + list(lams) + [subln_g]
        aliases = {}
        n_lat_chunks = SEQ // TK
    else:
        tq, nq = CTX_LEN, 1
        q_map = lambda b, h, i: (ctx_blk0 + b, h)
        in_specs = [pl.BlockSpec((tq, dv), q_map)] + ctx_specs + small
        in_specs.append(pl.BlockSpec(memory_space=pl.ANY))
        args = [qkv, qkv, qkv] + list(lams) + [subln_g, prev_out]
        aliases = {len(args) - 1: 0}
        n_lat_chunks = 0
    return pl.pallas_call(
        functools.partial(_attn_kernel, n_lat_chunks=n_lat_chunks, lam_init=lam_init),
        grid=(BATCH, DIFF_HEADS, nq),
        in_specs=in_specs,
        out_specs=pl.BlockSpec((tq, dv), q_map),
        out_shape=jax.ShapeDtypeStruct((NT, D_MODEL), BF16),
        scratch_shapes=[
            pltpu.VMEM((2 * tq, dv), F32),
            pltpu.VMEM((2 * tq, LANES), F32),
            pltpu.VMEM((2 * tq, LANES), F32),
        ],
        input_output_aliases=aliases,
        compiler_params=_params("arbitrary", "arbitrary", "arbitrary"),
        name="diff_attn_ctx" if n_lat_chunks == 0 else "diff_attn",
    )(*args)


def _outproj_kernel(a_ref, w_ref, x_ref, gate_ref, out_ref):
    y = jnp.dot(a_ref[...], w_ref[...], preferred_element_type=F32)
    out_ref[...] = x_ref[...] + gate_ref[...] * y


def _outproj(a, w, xs, mods, n_tiles):
    nj = D_MODEL // TN
    return pl.pallas_call(
        _outproj_kernel,
        grid=(n_tiles, nj),
        in_specs=[
            pl.BlockSpec((TM, D_MODEL), lambda i, j: (i, 0)),
            pl.BlockSpec((D_MODEL, TN), lambda i, j: (0, j)),
            pl.BlockSpec((TM, TN), lambda i, j: (i, j)),
            pl.BlockSpec((None, 1, TN), lambda i, j: (_mod_row(i, TM), 0, 2 * nj + j)),
        ],
        out_specs=pl.BlockSpec((TM, TN), lambda i, j: (i, j)),
        out_shape=jax.ShapeDtypeStruct((NT, D_MODEL), F32),
        input_output_aliases={2: 0},
        compiler_params=_params("arbitrary", "arbitrary"),
        name="outproj",
    )(a, w, xs, mods)


def _ffn_kernel(x_ref, g_ref, shift_ref, scale_ref, gate_ref, wg_ref, wu_ref, wd_ref, out_ref, h_ref):
    k = pl.program_id(1)

    @pl.when(k == 0)
    def _():
        h_ref[...] = _norm_modulate(x_ref[...], g_ref[...], shift_ref[...], scale_ref[...]).astype(BF16)
        out_ref[...] = jnp.zeros_like(out_ref)

    h = h_ref[...]
    gt = jnp.dot(h, wg_ref[...], preferred_element_type=F32)
    up = jnp.dot(h, wu_ref[...], preferred_element_type=F32)
    act = (gt * jax.nn.sigmoid(gt) * up).astype(BF16)
    out_ref[...] += jnp.dot(act, wd_ref[...], preferred_element_type=F32)

    @pl.when(k == pl.num_programs(1) - 1)
    def _():
        out_ref[...] = x_ref[...] + gate_ref[...] * out_ref[...]


def _ffn(xs, g, mods, w_gu, w_down, n_tiles):
    nh = FFN_HIDDEN // TH
    return pl.pallas_call(
        _ffn_kernel,
        grid=(n_tiles, nh),
        in_specs=[
            pl.BlockSpec((TMF, D_MODEL), lambda i, k: (i, 0)),
            pl.BlockSpec((1, D_MODEL), lambda i, k: (0, 0)),
            _mod_spec(3, TMF),
            _mod_spec(4, TMF),
            _mod_spec(5, TMF),
            pl.BlockSpec((D_MODEL, TH), lambda i, k: (0, k)),
            pl.BlockSpec((D_MODEL, TH), lambda i, k: (0, nh + k)),
            pl.BlockSpec((TH, D_MODEL), lambda i, k: (k, 0)),
        ],
        out_specs=pl.BlockSpec((TMF, D_MODEL), lambda i, k: (i, 0)),
        out_shape=jax.ShapeDtypeStruct((n_tiles * TMF, D_MODEL), F32),
        scratch_shapes=[pltpu.VMEM((TMF, D_MODEL), BF16)],
        compiler_params=_params("arbitrary", "arbitrary"),
        name="ffn",
    )(xs, g, mods, mods, mods, w_gu, w_gu, w_down)


def _pair_major(a):
    n_freq = DIFF_DH // 4
    lead = a.shape[:-1]
    a = a.reshape(*lead, a.shape[-1] // DIFF_DH, 2, 2, n_freq)
    return jnp.swapaxes(a, -3, -2).reshape(*lead, -1)


def _rope_tables():
    n_freq = DIFF_DH // 4
    pos = jnp.arange(SEQ)
    freqs = ROPE_BASE ** (-jnp.arange(n_freq, dtype=F32) / n_freq)
    ang = jnp.stack([pos // GRID_W, pos % GRID_W], axis=-1).astype(F32)[:, :, None] * freqs
    cos, sin = jnp.cos(ang), jnp.sin(ang)
    cos_t = jnp.concatenate([cos[:, 0], cos[:, 1], cos[:, 0], cos[:, 1]], axis=-1)
    sin_t = jnp.concatenate([-sin[:, 0], -sin[:, 1], sin[:, 0], sin[:, 1]], axis=-1)
    cos_t = jnp.concatenate([cos_t, jnp.ones((TM, LANES), F32)], axis=0)
    sin_t = jnp.concatenate([sin_t, jnp.zeros((TM, LANES), F32)], axis=0)
    return cos_t, sin_t


def kernel(x, c, ctx, c_ctx, ada_w, ada_b, norm_g, mlstm_w_in, mlstm_gate_b, mlstm_head_g, mlstm_w_out,
           diff_w_in, diff_w_out, diff_q_g, diff_k_g, diff_lq1, diff_lk1, diff_lq2, diff_lk2, diff_subln_g,
           ffn_w_gu, ffn_w_down):
    assert x.shape == (BATCH, SEQ, D_MODEL) and ctx.shape == (BATCH, CTX_LEN, D_MODEL)
    xs = jnp.concatenate([x.reshape(N_LAT, D_MODEL), ctx.reshape(N_CTX, D_MODEL)], axis=0)
    cc = jnp.concatenate([c, c_ctx[None], jnp.zeros((8 - BATCH - 1, D_MODEL), F32)], axis=0)
    mods_all = _ada(cc, ada_w, ada_b).reshape(DEPTH, 8, 1, 6 * D_MODEL)
    cos_t, sin_t = _rope_tables()
    row = lambda v: v.reshape(1, -1)

    for i in range(DEPTH):
        last = i == DEPTH - 1
        n_rows = N_LAT if last else NT
        mods = mods_all[i]
        j = i // N_MIXERS
        if i % N_MIXERS == 0:
            w_in = mlstm_w_in[j]
            w_main = w_in[:, :MLSTM_MAIN].astype(BF16)
            w_gate = jnp.pad(w_in[:, MLSTM_MAIN:], ((0, 0), (0, LANES - N_GATES))).astype(BF16)
            gate_b = jnp.pad(mlstm_gate_b[j], (0, LANES - N_GATES)).reshape(1, LANES)
            qkvo, gates = _inproj_mlstm(xs, row(norm_g[i, 0]), mods, w_main, w_gate)
            hf = _mlstm_scan(qkvo, gates, gate_b, None, None, reverse=False)
            a = _mlstm_scan(qkvo, gates, gate_b, row(mlstm_head_g[j]), hf, reverse=True)
            w_out = mlstm_w_out[j].astype(BF16)
        else:
            lam_init = 0.8 - 0.6 * math.exp(-0.3 * i)
            qkg = jnp.concatenate([jnp.tile(_pair_major(diff_q_g[j]), 2 * DIFF_HEADS),
                                   jnp.tile(_pair_major(diff_k_g[j]), 2 * DIFF_HEADS)]).reshape(1, 2 * D_MODEL)
            w_in = diff_w_in[j]
            w_in = jnp.concatenate([_pair_major(w_in[:, :2 * D_MODEL]), w_in[:, 2 * D_MODEL:]], axis=1)
            qkv = _inproj_diff(xs, row(norm_g[i, 0]), mods, w_in.astype(BF16), qkg, cos_t, sin_t)
            lams = [row(diff_lq1[j]), row(diff_lk1[j]), row(diff_lq2[j]), row(diff_lk2[j])]
            a = _attention(qkv, lams, row(diff_subln_g[j]), lam_init, None)
            if not last:
                a = _attention(qkv, lams, row(diff_subln_g[j]), lam_init, a)
            w_out = diff_w_out[j].astype(BF16)
        xs = _outproj(a, w_out, xs, mods, n_rows // TM)
        xs = _ffn(xs, row(norm_g[i, 1]), mods, ffn_w_gu[i].astype(BF16), ffn_w_down[i].astype(BF16),
                  n_rows // TMF)
    return xs.reshape(BATCH, SEQ, D_MODEL)
```

```python
import functools
import math

import jax
import jax.numpy as jnp
from jax import lax
from jax.experimental import pallas as pl
from jax.experimental.pallas import tpu as pltpu

F32 = jnp.float32
BF16 = jnp.bfloat16

D_MODEL = 2048
BATCH = 4
SEQ = 4096
DEPTH = 4
GRID_W = 64
CTX_LEN = 256
N_MIXERS = 2

MLSTM_HEADS = 8
MLSTM_DK = D_MODEL // (2 * MLSTM_HEADS)
MLSTM_DV = D_MODEL // MLSTM_HEADS
GATE_CAP = 15.0
MLSTM_QK = MLSTM_HEADS * MLSTM_DK
MLSTM_MAIN = 2 * MLSTM_QK + 2 * D_MODEL
N_GATES = 4 * MLSTM_HEADS

DIFF_HEADS = 8
DIFF_DH = D_MODEL // (2 * DIFF_HEADS)
ROPE_BASE = 10000.0

FFN_HIDDEN = -(-(8 * D_MODEL) // (3 * 256)) * 256
EPS = 1e-6

N_LAT = BATCH * SEQ
N_CTX = BATCH * CTX_LEN
NT = N_LAT + N_CTX

LANES = 128
TM = 1024
TN = 1024
TMF = 512
TH = 512
CHUNK = 256
TQ = 512
TK = 2048
VMEM_LIMIT = 56 * 1024 * 1024

NT_DIMS = (((1,), (1,)), ((), ()))
TN_DIMS = (((0,), (0,)), ((), ()))


def _params(*sem):
    return pltpu.CompilerParams(dimension_semantics=sem, vmem_limit_bytes=VMEM_LIMIT)


def _mod_row(i, tm):
    return jnp.minimum(i // (SEQ // tm), BATCH)


def _mod_spec(k, tm=TM):
    return pl.BlockSpec((None, 1, D_MODEL), lambda i, j: (_mod_row(i, tm), 0, k))


def _lane_repeat(x, n, axis):
    assert axis == 1
    return jnp.concatenate([x] * n, axis=1) if n > 1 else x


def _col_tiles(w, tn):
    k, n = w.shape
    return w.reshape(k, n // tn, tn).transpose(1, 0, 2).astype(BF16)


def _norm_modulate(x, g, shift, scale):
    y = x * lax.rsqrt(jnp.mean(x * x, axis=-1, keepdims=True) + EPS) * g
    return y * (1.0 + scale) + shift


def _ada_kernel(c_ref, w_ref, b_ref, o_ref):
    c = c_ref[...]
    a = (c * jax.nn.sigmoid(c)).astype(BF16)
    o_ref[...] = jnp.dot(a, w_ref[...].astype(BF16), preferred_element_type=F32) + b_ref[...]


def _ada(cc, ada_w, ada_b):
    tn = 1024
    return pl.pallas_call(
        _ada_kernel,
        grid=(DEPTH, 6 * D_MODEL // tn),
        in_specs=[
            pl.BlockSpec((8, D_MODEL), lambda l, j: (0, 0)),
            pl.BlockSpec((None, D_MODEL, tn), lambda l, j: (l, 0, j)),
            pl.BlockSpec((None, 1, tn), lambda l, j: (l, 0, j)),
        ],
        out_specs=pl.BlockSpec((None, 8, tn), lambda l, j: (l, 0, j)),
        out_shape=jax.ShapeDtypeStruct((DEPTH, 8, 6 * D_MODEL), F32),
        compiler_params=_params("arbitrary", "arbitrary"),
        name="ada_mod",
    )(cc, ada_w, ada_b.reshape(DEPTH, 1, 6 * D_MODEL))


def _inproj_mlstm_kernel(x_ref, g_ref, shift_ref, scale_ref, w_ref, wg_ref, out_ref, gate_ref, h_ref):
    j = pl.program_id(1)

    @pl.when(j == 0)
    def _():
        h = _norm_modulate(x_ref[...], g_ref[...], shift_ref[...], scale_ref[...]).astype(BF16)
        h_ref[...] = h
        gate_ref[...] = jnp.dot(h, wg_ref[...], preferred_element_type=F32)

    acc = jnp.dot(h_ref[...], w_ref[...], preferred_element_type=F32)
    qscale = jnp.where(j < MLSTM_QK // TN, MLSTM_DK ** -0.5, 1.0).astype(F32)
    out_ref[...] = (acc * qscale).astype(BF16)


def _inproj_mlstm(xs, g, mods, w_main, w_gate):
    return pl.pallas_call(
        _inproj_mlstm_kernel,
        grid=(NT // TM, MLSTM_MAIN // TN),
        in_specs=[
            pl.BlockSpec((TM, D_MODEL), lambda i, j: (i, 0)),
            pl.BlockSpec((1, D_MODEL), lambda i, j: (0, 0)),
            _mod_spec(0),
            _mod_spec(1),
            pl.BlockSpec((None, D_MODEL, TN), lambda i, j: (j, 0, 0)),
            pl.BlockSpec((D_MODEL, LANES), lambda i, j: (0, 0)),
        ],
        out_specs=[
            pl.BlockSpec((TM, TN), lambda i, j: (i, j)),
            pl.BlockSpec((TM, LANES), lambda i, j: (i, 0)),
        ],
        out_shape=[
            jax.ShapeDtypeStruct((NT, MLSTM_MAIN), BF16),
            jax.ShapeDtypeStruct((NT, LANES), F32),
        ],
        scratch_shapes=[pltpu.VMEM((TM, D_MODEL), BF16)],
        compiler_params=_params("arbitrary", "arbitrary"),
        name="inproj_mlstm",
    )(xs, g, mods, mods, w_main, w_gate)


def _mlstm_scan_kernel(*refs, reverse, final):
    if final:
        (q_ref, k_ref, v_ref, gate_ref, gb_ref, o_ref, hf_ref, hg_ref,
         out_ref, c_ref, n_ref, m_ref) = refs
    else:
        q_ref, k_ref, v_ref, gate_ref, gb_ref, out_ref, c_ref, n_ref, m_ref = refs
    L = CHUNK

    @pl.when(pl.program_id(1) == 0)
    def _():
        c_ref[...] = jnp.zeros_like(c_ref)
        n_ref[...] = jnp.zeros_like(n_ref)
        m_ref[...] = jnp.zeros_like(m_ref)

    gg = GATE_CAP * jnp.tanh((gate_ref[...] + gb_ref[...]) / GATE_CAP)
    lsig = jax.nn.log_sigmoid(gg)
    t_idx = lax.broadcasted_iota(jnp.int32, (L, L), 0)
    s_idx = lax.broadcasted_iota(jnp.int32, (L, L), 1)
    mask = (s_idx >= t_idx) if reverse else (s_idx <= t_idx)
    bcum = jnp.dot(mask.astype(F32), lsig, precision=lax.Precision.HIGHEST,
                   preferred_element_type=F32)
    bal = pltpu.roll(bcum, LANES - MLSTM_HEADS, 1)
    r = gg - bal
    r_t = r.T
    base = 2 * MLSTM_HEADS if reverse else 0
    last = 0 if reverse else L - 1

    for h in range(MLSTM_HEADS):
        col = base + h
        b_rep = jnp.broadcast_to(bal[:, col:col + 1], (L, LANES))
        r_rep = jnp.broadcast_to(r[:, col:col + 1], (L, LANES))
        r_row = r_t[col:col + 1, :]
        b_last = b_rep[last:last + 1, :]
        m = m_ref[h]

        dmat = jnp.where(mask, _lane_repeat(b_rep, L // LANES, 1) + r_row, -jnp.inf)
        inter = b_rep + m
        m_t = jnp.maximum(inter, jnp.max(dmat, axis=-1, keepdims=True))
        w_intra = jnp.exp(dmat - _lane_repeat(m_t, L // LANES, 1))
        w_inter = jnp.exp(inter - m_t)

        qh = q_ref[:, h * MLSTM_DK:(h + 1) * MLSTM_DK]
        kh = k_ref[:, h * MLSTM_DK:(h + 1) * MLSTM_DK]
        vh = v_ref[:, h * MLSTM_DV:(h + 1) * MLSTM_DV]
        s = lax.dot_general(qh, kh, NT_DIMS, preferred_element_type=F32) * w_intra
        c_old = c_ref[h]
        num = (_lane_repeat(w_inter, MLSTM_DV // LANES, 1)
               * jnp.dot(qh, c_old.astype(BF16), preferred_element_type=F32)
               + jnp.dot(s.astype(BF16), vh, preferred_element_type=F32))
        n_old = n_ref[h]
        qn = jnp.sum(qh.astype(F32) * n_old.astype(BF16).astype(F32), axis=-1, keepdims=True)
        den = w_inter * qn + jnp.sum(s, axis=-1, keepdims=True)
        inv = 1.0 / jnp.maximum(jnp.abs(den), jnp.exp(-m_t))
        hout = num * _lane_repeat(inv, MLSTM_DV // LANES, 1)

        a_rep = b_last + r_rep
        m_new = jnp.maximum(b_last + m, jnp.max(a_rep, axis=0, keepdims=True))
        wk = jnp.exp(a_rep - m_new)
        dec = jnp.exp(b_last + m - m_new)
        kw = kh.astype(F32) * wk
        c_ref[h] = (_lane_repeat(dec, MLSTM_DV // LANES, 1) * c_old
                    + lax.dot_general(kw.astype(BF16), vh, TN_DIMS, preferred_element_type=F32))
        n_ref[h] = dec * n_old + jnp.sum(kw, axis=0, keepdims=True)
        m_ref[h] = m_new

        sl = slice(h * MLSTM_DV, (h + 1) * MLSTM_DV)
        if final:
            hs = hf_ref[:, sl] + hout
            y = hs * lax.rsqrt(jnp.mean(hs * hs, axis=-1, keepdims=True) + EPS) * hg_ref[:, sl]
            out_ref[:, sl] = (y * jax.nn.sigmoid(o_ref[:, sl].astype(F32))).astype(BF16)
        else:
            out_ref[:, sl] = hout


def _mlstm_scan(qkvo, gates, gate_b, head_g, hf, *, reverse):
    final = hf is not None
    n_lat_chunks = SEQ // CHUNK
    ctx_blk0 = N_LAT // CHUNK

    def row_blk(b, c):
        j = (n_lat_chunks - c) if reverse else (c - 1)
        return jnp.where(c == 0, ctx_blk0 + b, b * n_lat_chunks + j)

    in_specs = [
        pl.BlockSpec((CHUNK, MLSTM_QK), lambda b, c: (row_blk(b, c), 0)),
        pl.BlockSpec((CHUNK, MLSTM_QK), lambda b, c: (row_blk(b, c), 1)),
        pl.BlockSpec((CHUNK, D_MODEL), lambda b, c: (row_blk(b, c), 1)),
        pl.BlockSpec((CHUNK, LANES), lambda b, c: (row_blk(b, c), 0)),
        pl.BlockSpec((1, LANES), lambda b, c: (0, 0)),
    ]
    args = [qkvo, qkvo, qkvo, gates, gate_b]
    if final:
        in_specs += [
            pl.BlockSpec((CHUNK, D_MODEL), lambda b, c: (row_blk(b, c), 2)),
            pl.BlockSpec((CHUNK, D_MODEL), lambda b, c: (row_blk(b, c), 0)),
            pl.BlockSpec((1, D_MODEL), lambda b, c: (0, 0)),
        ]
        args += [qkvo, hf, head_g]
    return pl.pallas_call(
        functools.partial(_mlstm_scan_kernel, reverse=reverse, final=final),
        grid=(BATCH, 1 + n_lat_chunks),
        in_specs=in_specs,
        out_specs=pl.BlockSpec((CHUNK, D_MODEL), lambda b, c: (row_blk(b, c), 0)),
        out_shape=jax.ShapeDtypeStruct((NT, D_MODEL), BF16 if final else F32),
        scratch_shapes=[
            pltpu.VMEM((MLSTM_HEADS, MLSTM_DK, MLSTM_DV), F32),
            pltpu.VMEM((MLSTM_HEADS, 1, MLSTM_DK), F32),
            pltpu.VMEM((MLSTM_HEADS, 1, LANES), F32),
        ],
        compiler_params=_params("arbitrary", "arbitrary"),
        name="mlstm_scan_bwd" if reverse else "mlstm_scan_fwd",
    )(*args)


def _inproj_diff_kernel(x_ref, g_ref, shift_ref, scale_ref, w_ref, qkg_ref, cos_ref, sin_ref,
                        out_ref, h_ref):
    j = pl.program_id(1)
    n_qk_blocks = 2 * D_MODEL // TN

    @pl.when(j == 0)
    def _():
        h_ref[...] = _norm_modulate(x_ref[...], g_ref[...], shift_ref[...], scale_ref[...]).astype(BF16)

    acc = jnp.dot(h_ref[...], w_ref[...], preferred_element_type=F32)

    @pl.when(j < n_qk_blocks)
    def _():
        cos = cos_ref[...]
        sin = sin_ref[...]
        for grp in range(TN // LANES):
            sl = slice(grp * LANES, (grp + 1) * LANES)
            xg = acc[:, sl]
            y = xg * lax.rsqrt(jnp.mean(xg * xg, axis=-1, keepdims=True) + EPS) * qkg_ref[:, sl]
            out_ref[:, sl] = (y * cos + pltpu.roll(y, LANES // 2, 1) * sin).astype(BF16)

    @pl.when(j >= n_qk_blocks)
    def _():
        out_ref[...] = acc.astype(BF16)


def _inproj_diff(xs, g, mods, w, qkg, cos_t, sin_t):
    n_qk_blocks = 2 * D_MODEL // TN
    lat_tiles = SEQ // TM

    def rope_blk(i, j):
        return (jnp.where(i < N_LAT // TM, i % lat_tiles, lat_tiles), 0)

    return pl.pallas_call(
        _inproj_diff_kernel,
        grid=(NT // TM, 3 * D_MODEL // TN),
        in_specs=[
            pl.BlockSpec((TM, D_MODEL), lambda i, j: (i, 0)),
            pl.BlockSpec((1, D_MODEL), lambda i, j: (0, 0)),
            _mod_spec(0),
            _mod_spec(1),
            pl.BlockSpec((None, D_MODEL, TN), lambda i, j: (j, 0, 0)),
            pl.BlockSpec((1, TN), lambda i, j: (0, jnp.minimum(j, n_qk_blocks - 1))),
            pl.BlockSpec((TM, LANES), rope_blk),
            pl.BlockSpec((TM, LANES), rope_blk),
        ],
        out_specs=pl.BlockSpec((TM, TN), lambda i, j: (i, j)),
        out_shape=jax.ShapeDtypeStruct((NT, 3 * D_MODEL), BF16),
        scratch_shapes=[pltpu.VMEM((TM, D_MODEL), BF16)],
        compiler_params=_params("arbitrary", "arbitrary"),
        name="inproj_diff",
    )(xs, g, mods, mods, w, qkg, cos_t, sin_t)


def _attn_kernel(*refs, n_lat_chunks, lam_init):
    if n_lat_chunks:
        (q_ref, kl_ref, vl_ref, kc_ref, vc_ref, lq1_ref, lk1_ref, lq2_ref, lk2_ref, sg_ref,
         out_ref, acc_ref, m_ref, l_ref) = refs
    else:
        (q_ref, kc_ref, vc_ref, lq1_ref, lk1_ref, lq2_ref, lk2_ref, sg_ref, _,
         out_ref, acc_ref, m_ref, l_ref) = refs
    tq = q_ref.shape[0]
    dv = 2 * DIFF_DH
    c2 = DIFF_DH ** -0.5 * math.log2(math.e)
    q = q_ref[...]

    def scores(kblk):
        return jnp.concatenate(
            [lax.dot_general(q[:, t * DIFF_DH:(t + 1) * DIFF_DH], kblk[:, t * DIFF_DH:(t + 1) * DIFF_DH],
                             NT_DIMS, preferred_element_type=F32) for t in range(2)], axis=0)

    s = scores(kc_ref[...])
    m0 = jnp.max(s, axis=-1, keepdims=True)
    p = jnp.exp2((s - m0) * c2)
    m_ref[...] = jnp.broadcast_to(m0, m_ref.shape)
    l_ref[...] = jnp.broadcast_to(jnp.sum(p, axis=-1, keepdims=True), l_ref.shape)
    acc_ref[...] = jnp.dot(p.astype(BF16), vc_ref[...], preferred_element_type=F32)

    if n_lat_chunks:
        def body(c, carry):
            off = pl.multiple_of(c * TK, TK)
            s = scores(kl_ref[pl.ds(off, TK), :])
            m_old = m_ref[...]
            m_new = jnp.maximum(m_old, jnp.max(s, axis=-1, keepdims=True))
            p = jnp.exp2((s - _lane_repeat(m_new, TK // LANES, 1)) * c2)
            alpha = jnp.exp2((m_old - m_new) * c2)
            l_ref[...] = alpha * l_ref[...] + jnp.sum(p, axis=-1, keepdims=True)
            acc_ref[...] = (_lane_repeat(alpha, dv // LANES, 1) * acc_ref[...]
                            + jnp.dot(p.astype(BF16), vl_ref[pl.ds(off, TK), :],
                                      preferred_element_type=F32))
            m_ref[...] = m_new
            return carry
        lax.fori_loop(0, n_lat_chunks, body, 0)

    lam = (jnp.exp(jnp.sum(lq1_ref[...] * lk1_ref[...], axis=-1, keepdims=True))
           - jnp.exp(jnp.sum(lq2_ref[...] * lk2_ref[...], axis=-1, keepdims=True)) + lam_init)
    o_all = acc_ref[...] / _lane_repeat(l_ref[...], dv // LANES, 1)
    o = o_all[:tq] - lam * o_all[tq:]
    y = o * lax.rsqrt(jnp.mean(o * o, axis=-1, keepdims=True) + EPS) * sg_ref[...]
    out_ref[...] = (y * (1.0 - lam_init)).astype(BF16)


def _attention(qkv, lams, subln_g, lam_init, prev_out):
    dv = 2 * DIFF_DH
    kcol = D_MODEL // dv
    vcol = 2 * D_MODEL // dv
    ctx_blk0 = N_LAT // CTX_LEN
    small = [pl.BlockSpec((1, DIFF_DH), lambda b, h, i: (0, 0))] * 4
    small.append(pl.BlockSpec((1, dv), lambda b, h, i: (0, 0)))
    ctx_specs = [
        pl.BlockSpec((CTX_LEN, dv), lambda b, h, i: (ctx_blk0 + b, kcol + h)),
        pl.BlockSpec((CTX_LEN, dv), lambda b, h, i: (ctx_blk0 + b, vcol + h)),
    ]
    if prev_out is None:
        tq, nq = TQ, SEQ // TQ
        q_map = lambda b, h, i: (b * nq + i, h)
        in_specs = [
            pl.BlockSpec((tq, dv), q_map),
            pl.BlockSpec((SEQ, dv), lambda b, h, i: (b, kcol + h)),
            pl.BlockSpec((SEQ, dv), lambda b, h, i: (b, vcol + h)),
        ] + ctx_specs + small
        args = [qkv, qkv, qkv, qkv, qkv] + list(lams) + [subln_g]
        aliases = {}
        n_lat_chunks = SEQ // TK
    else:
        tq, nq = CTX_LEN, 1
        q_map = lambda b, h, i: (ctx_blk0 + b, h)
        in_specs = [pl.BlockSpec((tq, dv), q_map)] + ctx_specs + small
        in_specs.append(pl.BlockSpec(memory_space=pl.ANY))
        args = [qkv, qkv, qkv] + list(lams) + [subln_g, prev_out]
        aliases = {len(args) - 1: 0}
        n_lat_chunks = 0
    return pl.pallas_call(
        functools.partial(_attn_kernel, n_lat_chunks=n_lat_chunks, lam_init=lam_init),
        grid=(BATCH, DIFF_HEADS, nq),
        in_specs=in_specs,
        out_specs=pl.BlockSpec((tq, dv), q_map),
        out_shape=jax.ShapeDtypeStruct((NT, D_MODEL), BF16),
        scratch_shapes=[
            pltpu.VMEM((2 * tq, dv), F32),
            pltpu.VMEM((2 * tq, LANES), F32),
            pltpu.VMEM((2 * tq, LANES), F32),
        ],
        input_output_aliases=aliases,
        compiler_params=_params("arbitrary", "arbitrary", "arbitrary"),
        name="diff_attn_ctx" if n_lat_chunks == 0 else "diff_attn",
    )(*args)


def _outproj_kernel(a_ref, w_ref, x_ref, gate_ref, out_ref):
    y = jnp.dot(a_ref[...], w_ref[...], preferred_element_type=F32)
    out_ref[...] = x_ref[...] + gate_ref[...] * y


def _outproj(a, w, xs, mods, n_tiles):
    nj = D_MODEL // TN
    return pl.pallas_call(
        _outproj_kernel,
        grid=(n_tiles, nj),
        in_specs=[
            pl.BlockSpec((TM, D_MODEL), lambda i, j: (i, 0)),
            pl.BlockSpec((None, D_MODEL, TN), lambda i, j: (j, 0, 0)),
            pl.BlockSpec((TM, TN), lambda i, j: (i, j)),
            pl.BlockSpec((None, 1, TN), lambda i, j: (_mod_row(i, TM), 0, 2 * nj + j)),
        ],
        out_specs=pl.BlockSpec((TM, TN), lambda i, j: (i, j)),
        out_shape=jax.ShapeDtypeStruct((NT, D_MODEL), F32),
        input_output_aliases={2: 0},
        compiler_params=_params("arbitrary", "arbitrary"),
        name="outproj",
    )(a, w, xs, mods)


def _ffn_kernel(x_ref, g_ref, shift_ref, scale_ref, gate_ref, wg_ref, wu_ref, wd_ref, out_ref, h_ref):
    k = pl.program_id(1)

    @pl.when(k == 0)
    def _():
        h_ref[...] = _norm_modulate(x_ref[...], g_ref[...], shift_ref[...], scale_ref[...]).astype(BF16)
        out_ref[...] = jnp.zeros_like(out_ref)

    h = h_ref[...]
    gt = jnp.dot(h, wg_ref[...], preferred_element_type=F32)
    up = jnp.dot(h, wu_ref[...], preferred_element_type=F32)
    act = (gt * jax.nn.sigmoid(gt) * up).astype(BF16)
    out_ref[...] += jnp.dot(act, wd_ref[...], preferred_element_type=F32)

    @pl.when(k == pl.num_programs(1) - 1)
    def _():
        out_ref[...] = x_ref[...] + gate_ref[...] * out_ref[...]


def _ffn(xs, g, mods, w_gu, w_down, n_tiles):
    nh = FFN_HIDDEN // TH
    return pl.pallas_call(
        _ffn_kernel,
        grid=(n_tiles, nh),
        in_specs=[
            pl.BlockSpec((TMF, D_MODEL), lambda i, k: (i, 0)),
            pl.BlockSpec((1, D_MODEL), lambda i, k: (0, 0)),
            _mod_spec(3, TMF),
            _mod_spec(4, TMF),
            _mod_spec(5, TMF),
            pl.BlockSpec((None, D_MODEL, TH), lambda i, k: (k, 0, 0)),
            pl.BlockSpec((None, D_MODEL, TH), lambda i, k: (nh + k, 0, 0)),
            pl.BlockSpec((TH, D_MODEL), lambda i, k: (k, 0)),
        ],
        out_specs=pl.BlockSpec((TMF, D_MODEL), lambda i, k: (i, 0)),
        out_shape=jax.ShapeDtypeStruct((n_tiles * TMF, D_MODEL), F32),
        scratch_shapes=[pltpu.VMEM((TMF, D_MODEL), BF16)],
        compiler_params=_params("arbitrary", "arbitrary"),
        name="ffn",
    )(xs, g, mods, mods, mods, w_gu, w_gu, w_down)


def _pair_major(a):
    n_freq = DIFF_DH // 4
    lead = a.shape[:-1]
    a = a.reshape(*lead, a.shape[-1] // DIFF_DH, 2, 2, n_freq)
    return jnp.swapaxes(a, -3, -2).reshape(*lead, -1)


def _rope_tables():
    n_freq = DIFF_DH // 4
    pos = jnp.arange(SEQ)
    freqs = ROPE_BASE ** (-jnp.arange(n_freq, dtype=F32) / n_freq)
    ang = jnp.stack([pos // GRID_W, pos % GRID_W], axis=-1).astype(F32)[:, :, None] * freqs
    cos, sin = jnp.cos(ang), jnp.sin(ang)
    cos_t = jnp.concatenate([cos[:, 0], cos[:, 1], cos[:, 0], cos[:, 1]], axis=-1)
    sin_t = jnp.concatenate([-sin[:, 0], -sin[:, 1], sin[:, 0], sin[:, 1]], axis=-1)
    cos_t = jnp.concatenate([cos_t, jnp.ones((TM, LANES), F32)], axis=0)
    sin_t = jnp.concatenate([sin_t, jnp.zeros((TM, LANES), F32)], axis=0)
    return cos_t, sin_t


def kernel(x, c, ctx, c_ctx, ada_w, ada_b, norm_g, mlstm_w_in, mlstm_gate_b, mlstm_head_g, mlstm_w_out,
           diff_w_in, diff_w_out, diff_q_g, diff_k_g, diff_lq1, diff_lk1, diff_lq2, diff_lk2, diff_subln_g,
           ffn_w_gu, ffn_w_down):
    assert x.shape == (BATCH, SEQ, D_MODEL) and ctx.shape == (BATCH, CTX_LEN, D_MODEL)
    xs = jnp.concatenate([x.reshape(N_LAT, D_MODEL), ctx.reshape(N_CTX, D_MODEL)], axis=0)
    cc = jnp.concatenate([c, c_ctx[None], jnp.zeros((8 - BATCH - 1, D_MODEL), F32)], axis=0)
    mods_all = _ada(cc, ada_w, ada_b).reshape(DEPTH, 8, 1, 6 * D_MODEL)
    cos_t, sin_t = _rope_tables()
    row = lambda v: v.reshape(1, -1)

    for i in range(DEPTH):
        last = i == DEPTH - 1
        n_rows = N_LAT if last else NT
        mods = mods_all[i]
        j = i // N_MIXERS
        if i % N_MIXERS == 0:
            w_in = mlstm_w_in[j]
            w_main = _col_tiles(w_in[:, :MLSTM_MAIN], TN)
            w_gate = jnp.pad(w_in[:, MLSTM_MAIN:], ((0, 0), (0, LANES - N_GATES))).astype(BF16)
            gate_b = jnp.pad(mlstm_gate_b[j], (0, LANES - N_GATES)).reshape(1, LANES)
            qkvo, gates = _inproj_mlstm(xs, row(norm_g[i, 0]), mods, w_main, w_gate)
            hf = _mlstm_scan(qkvo, gates, gate_b, None, None, reverse=False)
            a = _mlstm_scan(qkvo, gates, gate_b, row(mlstm_head_g[j]), hf, reverse=True)
            w_out = _col_tiles(mlstm_w_out[j], TN)
        else:
            lam_init = 0.8 - 0.6 * math.exp(-0.3 * i)
            qkg = jnp.concatenate([jnp.tile(_pair_major(diff_q_g[j]), 2 * DIFF_HEADS),
                                   jnp.tile(_pair_major(diff_k_g[j]), 2 * DIFF_HEADS)]).reshape(1, 2 * D_MODEL)
            w_in = diff_w_in[j]
            w_in = jnp.concatenate([_pair_major(w_in[:, :2 * D_MODEL]), w_in[:, 2 * D_MODEL:]], axis=1)
            qkv = _inproj_diff(xs, row(norm_g[i, 0]), mods, _col_tiles(w_in, TN), qkg, cos_t, sin_t)
            lams = [row(diff_lq1[j]), row(diff_lk1[j]), row(diff_lq2[j]), row(diff_lk2[j])]
            a = _attention(qkv, lams, row(diff_subln_g[j]), lam_init, None)
            if not last:
                a = _attention(qkv, lams, row(diff_subln_g[j]), lam_init, a)
            w_out = _col_tiles(diff_w_out[j], TN)
        xs = _outproj(a, w_out, xs, mods, n_rows // TM)
        xs = _ffn(xs, row(norm_g[i, 1]), mods, _col_tiles(ffn_w_gu[i], TH), ffn_w_down[i].astype(BF16),
                  n_rows // TMF)
    return xs.reshape(BATCH, SEQ, D_MODEL)
```

```python
import functools
import math

import jax
import jax.numpy as jnp
from jax import lax
from jax.experimental import pallas as pl
from jax.experimental.pallas import tpu as pltpu

F32 = jnp.float32
BF16 = jnp.bfloat16

D_MODEL = 2048
BATCH = 4
SEQ = 4096
DEPTH = 4
GRID_W = 64
CTX_LEN = 256
N_MIXERS = 2

MLSTM_HEADS = 8
MLSTM_DK = D_MODEL // (2 * MLSTM_HEADS)
MLSTM_DV = D_MODEL // MLSTM_HEADS
GATE_CAP = 15.0
MLSTM_QK = MLSTM_HEADS * MLSTM_DK
MLSTM_MAIN = 2 * MLSTM_QK + 2 * D_MODEL
N_GATES = 4 * MLSTM_HEADS

DIFF_HEADS = 8
DIFF_DH = D_MODEL // (2 * DIFF_HEADS)
ROPE_BASE = 10000.0

FFN_HIDDEN = -(-(8 * D_MODEL) // (3 * 256)) * 256
EPS = 1e-6

N_LAT = BATCH * SEQ
N_CTX = BATCH * CTX_LEN
NT = N_LAT + N_CTX

LANES = 128
TM = 1024
TN = 1024
TMF = 512
TH = 512
CHUNK = 256
TQ = 512
TK = 2048
VMEM_LIMIT = 56 * 1024 * 1024

NT_DIMS = (((1,), (1,)), ((), ()))
TN_DIMS = (((0,), (0,)), ((), ()))


def _params(*sem):
    return pltpu.CompilerParams(dimension_semantics=sem, vmem_limit_bytes=VMEM_LIMIT)


def _mod_row(i, tm):
    return jnp.minimum(i // (SEQ // tm), BATCH)


def _mod_spec(layer, k, tm=TM):
    return pl.BlockSpec((None, None, 1, D_MODEL), lambda i, j: (layer, _mod_row(i, tm), 0, k))


def _weight_spec(layer, tn):
    return pl.BlockSpec((None, D_MODEL, tn), lambda i, j: (layer, 0, j))


def _lane_repeat(x, n, axis):
    assert axis == 1
    return jnp.concatenate([x] * n, axis=1) if n > 1 else x


def _norm_modulate(x, g, shift, scale):
    y = x * lax.rsqrt(jnp.mean(x * x, axis=-1, keepdims=True) + EPS) * g
    return y * (1.0 + scale) + shift


def _ada_kernel(c_ref, w_ref, b_ref, o_ref):
    c = c_ref[...]
    a = (c * jax.nn.sigmoid(c)).astype(BF16)
    o_ref[...] = jnp.dot(a, w_ref[...].astype(BF16), preferred_element_type=F32) + b_ref[...]


def _ada(cc, ada_w, ada_b):
    tn = 1024
    return pl.pallas_call(
        _ada_kernel,
        grid=(DEPTH, 6 * D_MODEL // tn),
        in_specs=[
            pl.BlockSpec((8, D_MODEL), lambda l, j: (0, 0)),
            pl.BlockSpec((None, D_MODEL, tn), lambda l, j: (l, 0, j)),
            pl.BlockSpec((None, 1, tn), lambda l, j: (l, 0, j)),
        ],
        out_specs=pl.BlockSpec((None, 8, tn), lambda l, j: (l, 0, j)),
        out_shape=jax.ShapeDtypeStruct((DEPTH, 8, 6 * D_MODEL), F32),
        compiler_params=_params("arbitrary", "arbitrary"),
        name="ada_mod",
    )(cc, ada_w, ada_b.reshape(DEPTH, 1, 6 * D_MODEL))


def _inproj_mlstm_kernel(x_ref, g_ref, shift_ref, scale_ref, w_ref, wg_ref, out_ref, gate_ref, h_ref):
    j = pl.program_id(1)

    @pl.when(j == 0)
    def _():
        h = _norm_modulate(x_ref[...], g_ref[...], shift_ref[...], scale_ref[...]).astype(BF16)
        h_ref[...] = h
        gate_ref[...] = jnp.dot(h, wg_ref[...], preferred_element_type=F32)

    acc = jnp.dot(h_ref[...], w_ref[...], preferred_element_type=F32)
    qscale = jnp.where(j < MLSTM_QK // TN, MLSTM_DK ** -0.5, 1.0).astype(F32)
    out_ref[...] = (acc * qscale).astype(BF16)


def _inproj_mlstm(xs, g, mods, layer, w_all, w_layer, w_gate):
    return pl.pallas_call(
        _inproj_mlstm_kernel,
        grid=(NT // TM, MLSTM_MAIN // TN),
        in_specs=[
            pl.BlockSpec((TM, D_MODEL), lambda i, j: (i, 0)),
            pl.BlockSpec((1, D_MODEL), lambda i, j: (0, 0)),
            _mod_spec(layer, 0),
            _mod_spec(layer, 1),
            _weight_spec(w_layer, TN),
            pl.BlockSpec((D_MODEL, LANES), lambda i, j: (0, 0)),
        ],
        out_specs=[
            pl.BlockSpec((TM, TN), lambda i, j: (i, j)),
            pl.BlockSpec((TM, LANES), lambda i, j: (i, 0)),
        ],
        out_shape=[
            jax.ShapeDtypeStruct((NT, MLSTM_MAIN), BF16),
            jax.ShapeDtypeStruct((NT, LANES), F32),
        ],
        scratch_shapes=[pltpu.VMEM((TM, D_MODEL), BF16)],
        compiler_params=_params("arbitrary", "arbitrary"),
        name="inproj_mlstm",
    )(xs, g, mods, mods, w_all, w_gate)


def _mlstm_scan_kernel(*refs, reverse, final):
    if final:
        (q_ref, k_ref, v_ref, gate_ref, gb_ref, o_ref, hf_ref, hg_ref,
         out_ref, c_ref, n_ref, m_ref) = refs
    else:
        q_ref, k_ref, v_ref, gate_ref, gb_ref, out_ref, c_ref, n_ref, m_ref = refs
    L = CHUNK

    @pl.when(pl.program_id(1) == 0)
    def _():
        c_ref[...] = jnp.zeros_like(c_ref)
        n_ref[...] = jnp.zeros_like(n_ref)
        m_ref[...] = jnp.zeros_like(m_ref)

    gg = GATE_CAP * jnp.tanh((gate_ref[...] + gb_ref[...]) / GATE_CAP)
    lsig = jax.nn.log_sigmoid(gg)
    t_idx = lax.broadcasted_iota(jnp.int32, (L, L), 0)
    s_idx = lax.broadcasted_iota(jnp.int32, (L, L), 1)
    mask = (s_idx >= t_idx) if reverse else (s_idx <= t_idx)
    bcum = jnp.dot(mask.astype(F32), lsig, precision=lax.Precision.HIGHEST,
                   preferred_element_type=F32)
    bal = pltpu.roll(bcum, LANES - MLSTM_HEADS, 1)
    r = gg - bal
    r_t = r.T
    base = 2 * MLSTM_HEADS if reverse else 0
    last = 0 if reverse else L - 1

    for h in range(MLSTM_HEADS):
        col = base + h
        b_rep = jnp.broadcast_to(bal[:, col:col + 1], (L, LANES))
        r_rep = jnp.broadcast_to(r[:, col:col + 1], (L, LANES))
        r_row = r_t[col:col + 1, :]
        b_last = b_rep[last:last + 1, :]
        m = m_ref[h]

        dmat = jnp.where(mask, _lane_repeat(b_rep, L // LANES, 1) + r_row, -jnp.inf)
        inter = b_rep + m
        m_t = jnp.maximum(inter, jnp.max(dmat, axis=-1, keepdims=True))
        w_intra = jnp.exp(dmat - _lane_repeat(m_t, L // LANES, 1))
        w_inter = jnp.exp(inter - m_t)

        qh = q_ref[:, h * MLSTM_DK:(h + 1) * MLSTM_DK]
        kh = k_ref[:, h * MLSTM_DK:(h + 1) * MLSTM_DK]
        vh = v_ref[:, h * MLSTM_DV:(h + 1) * MLSTM_DV]
        s = lax.dot_general(qh, kh, NT_DIMS, preferred_element_type=F32) * w_intra
        c_old = c_ref[h]
        num = (_lane_repeat(w_inter, MLSTM_DV // LANES, 1)
               * jnp.dot(qh, c_old.astype(BF16), preferred_element_type=F32)
               + jnp.dot(s.astype(BF16), vh, preferred_element_type=F32))
        n_old = n_ref[h]
        qn = jnp.sum(qh.astype(F32) * n_old.astype(BF16).astype(F32), axis=-1, keepdims=True)
        den = w_inter * qn + jnp.sum(s, axis=-1, keepdims=True)
        inv = 1.0 / jnp.maximum(jnp.abs(den), jnp.exp(-m_t))
        hout = num * _lane_repeat(inv, MLSTM_DV // LANES, 1)

        a_rep = b_last + r_rep
        m_new = jnp.maximum(b_last + m, jnp.max(a_rep, axis=0, keepdims=True))
        wk = jnp.exp(a_rep - m_new)
        dec = jnp.exp(b_last + m - m_new)
        kw = kh.astype(F32) * wk
        c_ref[h] = (_lane_repeat(dec, MLSTM_DV // LANES, 1) * c_old
                    + lax.dot_general(kw.astype(BF16), vh, TN_DIMS, preferred_element_type=F32))
        n_ref[h] = dec * n_old + jnp.sum(kw, axis=0, keepdims=True)
        m_ref[h] = m_new

        sl = slice(h * MLSTM_DV, (h + 1) * MLSTM_DV)
        if final:
            hs = hf_ref[:, sl] + hout
            y = hs * lax.rsqrt(jnp.mean(hs * hs, axis=-1, keepdims=True) + EPS) * hg_ref[:, sl]
            out_ref[:, sl] = (y * jax.nn.sigmoid(o_ref[:, sl].astype(F32))).astype(BF16)
        else:
            out_ref[:, sl] = hout


def _mlstm_scan(qkvo, gates, gate_b, head_g, hf, *, reverse):
    final = hf is not None
    n_lat_chunks = SEQ // CHUNK
    ctx_blk0 = N_LAT // CHUNK

    def row_blk(b, c):
        j = (n_lat_chunks - c) if reverse else (c - 1)
        return jnp.where(c == 0, ctx_blk0 + b, b * n_lat_chunks + j)

    in_specs = [
        pl.BlockSpec((CHUNK, MLSTM_QK), lambda b, c: (row_blk(b, c), 0)),
        pl.BlockSpec((CHUNK, MLSTM_QK), lambda b, c: (row_blk(b, c), 1)),
        pl.BlockSpec((CHUNK, D_MODEL), lambda b, c: (row_blk(b, c), 1)),
        pl.BlockSpec((CHUNK, LANES), lambda b, c: (row_blk(b, c), 0)),
        pl.BlockSpec((1, LANES), lambda b, c: (0, 0)),
    ]
    args = [qkvo, qkvo, qkvo, gates, gate_b]
    if final:
        in_specs += [
            pl.BlockSpec((CHUNK, D_MODEL), lambda b, c: (row_blk(b, c), 2)),
            pl.BlockSpec((CHUNK, D_MODEL), lambda b, c: (row_blk(b, c), 0)),
            pl.BlockSpec((1, D_MODEL), lambda b, c: (0, 0)),
        ]
        args += [qkvo, hf, head_g]
    return pl.pallas_call(
        functools.partial(_mlstm_scan_kernel, reverse=reverse, final=final),
        grid=(BATCH, 1 + n_lat_chunks),
        in_specs=in_specs,
        out_specs=pl.BlockSpec((CHUNK, D_MODEL), lambda b, c: (row_blk(b, c), 0)),
        out_shape=jax.ShapeDtypeStruct((NT, D_MODEL), BF16 if final else F32),
        scratch_shapes=[
            pltpu.VMEM((MLSTM_HEADS, MLSTM_DK, MLSTM_DV), F32),
            pltpu.VMEM((MLSTM_HEADS, 1, MLSTM_DK), F32),
            pltpu.VMEM((MLSTM_HEADS, 1, LANES), F32),
        ],
        compiler_params=_params("arbitrary", "arbitrary"),
        name="mlstm_scan_bwd" if reverse else "mlstm_scan_fwd",
    )(*args)


def _inproj_diff_kernel(x_ref, g_ref, shift_ref, scale_ref, w_ref, qkg_ref, cos_ref, sin_ref,
                        out_ref, h_ref):
    j = pl.program_id(1)
    n_qk_blocks = 2 * D_MODEL // TN

    @pl.when(j == 0)
    def _():
        h_ref[...] = _norm_modulate(x_ref[...], g_ref[...], shift_ref[...], scale_ref[...]).astype(BF16)

    acc = jnp.dot(h_ref[...], w_ref[...], preferred_element_type=F32)

    @pl.when(j < n_qk_blocks)
    def _():
        cos = cos_ref[...]
        sin = sin_ref[...]
        for grp in range(TN // LANES):
            sl = slice(grp * LANES, (grp + 1) * LANES)
            xg = acc[:, sl]
            y = xg * lax.rsqrt(jnp.mean(xg * xg, axis=-1, keepdims=True) + EPS) * qkg_ref[:, sl]
            out_ref[:, sl] = (y * cos + pltpu.roll(y, LANES // 2, 1) * sin).astype(BF16)

    @pl.when(j >= n_qk_blocks)
    def _():
        out_ref[...] = acc.astype(BF16)


def _inproj_diff(xs, g, mods, layer, w_all, w_layer, qkg, cos_t, sin_t):
    n_qk_blocks = 2 * D_MODEL // TN
    lat_tiles = SEQ // TM

    def rope_blk(i, j):
        return (jnp.where(i < N_LAT // TM, i % lat_tiles, lat_tiles), 0)

    return pl.pallas_call(
        _inproj_diff_kernel,
        grid=(NT // TM, 3 * D_MODEL // TN),
        in_specs=[
            pl.BlockSpec((TM, D_MODEL), lambda i, j: (i, 0)),
            pl.BlockSpec((1, D_MODEL), lambda i, j: (0, 0)),
            _mod_spec(layer, 0),
            _mod_spec(layer, 1),
            _weight_spec(w_layer, TN),
            pl.BlockSpec((1, TN), lambda i, j: (0, jnp.minimum(j, n_qk_blocks - 1))),
            pl.BlockSpec((TM, LANES), rope_blk),
            pl.BlockSpec((TM, LANES), rope_blk),
        ],
        out_specs=pl.BlockSpec((TM, TN), lambda i, j: (i, j)),
        out_shape=jax.ShapeDtypeStruct((NT, 3 * D_MODEL), BF16),
        scratch_shapes=[pltpu.VMEM((TM, D_MODEL), BF16)],
        compiler_params=_params("arbitrary", "arbitrary"),
        name="inproj_diff",
    )(xs, g, mods, mods, w_all, qkg, cos_t, sin_t)


def _attn_kernel(*refs, n_lat_chunks, lam_init):
    if n_lat_chunks:
        (q_ref, kl_ref, vl_ref, kc_ref, vc_ref, lq1_ref, lk1_ref, lq2_ref, lk2_ref, sg_ref,
         out_ref, acc_ref, m_ref, l_ref) = refs
    else:
        (q_ref, kc_ref, vc_ref, lq1_ref, lk1_ref, lq2_ref, lk2_ref, sg_ref, _,
         out_ref, acc_ref, m_ref, l_ref) = refs
    tq = q_ref.shape[0]
    dv = 2 * DIFF_DH
    c2 = DIFF_DH ** -0.5 * math.log2(math.e)
    q = q_ref[...]

    def scores(kblk):
        return jnp.concatenate(
            [lax.dot_general(q[:, t * DIFF_DH:(t + 1) * DIFF_DH], kblk[:, t * DIFF_DH:(t + 1) * DIFF_DH],
                             NT_DIMS, preferred_element_type=F32) for t in range(2)], axis=0)

    s = scores(kc_ref[...])
    m0 = jnp.max(s, axis=-1, keepdims=True)
    p = jnp.exp2((s - m0) * c2)
    m_ref[...] = jnp.broadcast_to(m0, m_ref.shape)
    l_ref[...] = jnp.broadcast_to(jnp.sum(p, axis=-1, keepdims=True), l_ref.shape)
    acc_ref[...] = jnp.dot(p.astype(BF16), vc_ref[...], preferred_element_type=F32)

    if n_lat_chunks:
        def body(c, carry):
            off = pl.multiple_of(c * TK, TK)
            s = scores(kl_ref[pl.ds(off, TK), :])
            m_old = m_ref[...]
            m_new = jnp.maximum(m_old, jnp.max(s, axis=-1, keepdims=True))
            p = jnp.exp2((s - _lane_repeat(m_new, TK // LANES, 1)) * c2)
            alpha = jnp.exp2((m_old - m_new) * c2)
            l_ref[...] = alpha * l_ref[...] + jnp.sum(p, axis=-1, keepdims=True)
            acc_ref[...] = (_lane_repeat(alpha, dv // LANES, 1) * acc_ref[...]
                            + jnp.dot(p.astype(BF16), vl_ref[pl.ds(off, TK), :],
                                      preferred_element_type=F32))
            m_ref[...] = m_new
            return carry
        lax.fori_loop(0, n_lat_chunks, body, 0)

    lam = (jnp.exp(jnp.sum(lq1_ref[...] * lk1_ref[...], axis=-1, keepdims=True))
           - jnp.exp(jnp.sum(lq2_ref[...] * lk2_ref[...], axis=-1, keepdims=True)) + lam_init)
    o_all = acc_ref[...] / _lane_repeat(l_ref[...], dv // LANES, 1)
    o = o_all[:tq] - lam * o_all[tq:]
    y = o * lax.rsqrt(jnp.mean(o * o, axis=-1, keepdims=True) + EPS) * sg_ref[...]
    out_ref[...] = (y * (1.0 - lam_init)).astype(BF16)


def _attention(qkv, lams, subln_g, lam_init, prev_out):
    dv = 2 * DIFF_DH
    kcol = D_MODEL // dv
    vcol = 2 * D_MODEL // dv
    ctx_blk0 = N_LAT // CTX_LEN
    small = [pl.BlockSpec((1, DIFF_DH), lambda b, h, i: (0, 0))] * 4
    small.append(pl.BlockSpec((1, dv), lambda b, h, i: (0, 0)))
    ctx_specs = [
        pl.BlockSpec((CTX_LEN, dv), lambda b, h, i: (ctx_blk0 + b, kcol + h)),
        pl.BlockSpec((CTX_LEN, dv), lambda b, h, i: (ctx_blk0 + b, vcol + h)),
    ]
    if prev_out is None:
        tq, nq = TQ, SEQ // TQ
        q_map = lambda b, h, i: (b * nq + i, h)
        in_specs = [
            pl.BlockSpec((tq, dv), q_map),
            pl.BlockSpec((SEQ, dv), lambda b, h, i: (b, kcol + h)),
            pl.BlockSpec((SEQ, dv), lambda b, h, i: (b, vcol + h)),
        ] + ctx_specs + small
        args = [qkv, qkv, qkv, qkv, qkv] + list(lams) + [subln_g]
        aliases = {}
        n_lat_chunks = SEQ // TK
    else:
        tq, nq = CTX_LEN, 1
        q_map = lambda b, h, i: (ctx_blk0 + b, h)
        in_specs = [pl.BlockSpec((tq, dv), q_map)] + ctx_specs + small
        in_specs.append(pl.BlockSpec(memory_space=pl.ANY))
        args = [qkv, qkv, qkv] + list(lams) + [subln_g, prev_out]
        aliases = {len(args) - 1: 0}
        n_lat_chunks = 0
    return pl.pallas_call(
        functools.partial(_attn_kernel, n_lat_chunks=n_lat_chunks, lam_init=lam_init),
        grid=(BATCH, DIFF_HEADS, nq),
        in_specs=in_specs,
        out_specs=pl.BlockSpec((tq, dv), q_map),
        out_shape=jax.ShapeDtypeStruct((NT, D_MODEL), BF16),
        scratch_shapes=[
            pltpu.VMEM((2 * tq, dv), F32),
            pltpu.VMEM((2 * tq, LANES), F32),
            pltpu.VMEM((2 * tq, LANES), F32),
        ],
        input_output_aliases=aliases,
        compiler_params=_params("arbitrary", "arbitrary", "arbitrary"),
        name="diff_attn_ctx" if n_lat_chunks == 0 else "diff_attn",
    )(*args)


def _outproj_kernel(a_ref, w_ref, x_ref, gate_ref, out_ref):
    y = jnp.dot(a_ref[...], w_ref[...], preferred_element_type=F32)
    out_ref[...] = x_ref[...] + gate_ref[...] * y


def _outproj(a, w_all, w_layer, xs, mods, layer, n_tiles):
    nj = D_MODEL // TN
    return pl.pallas_call(
        _outproj_kernel,
        grid=(n_tiles, nj),
        in_specs=[
            pl.BlockSpec((TM, D_MODEL), lambda i, j: (i, 0)),
            _weight_spec(w_layer, TN),
            pl.BlockSpec((TM, TN), lambda i, j: (i, j)),
            pl.BlockSpec((None, None, 1, TN), lambda i, j: (layer, _mod_row(i, TM), 0, 2 * nj + j)),
        ],
        out_specs=pl.BlockSpec((TM, TN), lambda i, j: (i, j)),
        out_shape=jax.ShapeDtypeStruct((NT, D_MODEL), F32),
        input_output_aliases={2: 0},
        compiler_params=_params("arbitrary", "arbitrary"),
        name="outproj",
    )(a, w_all, xs, mods)


def _ffn_kernel(x_ref, g_ref, shift_ref, scale_ref, gate_ref, wg_ref, wu_ref, wd_ref, out_ref, h_ref):
    k = pl.program_id(1)

    @pl.when(k == 0)
    def _():
        h_ref[...] = _norm_modulate(x_ref[...], g_ref[...], shift_ref[...], scale_ref[...]).astype(BF16)
        out_ref[...] = jnp.zeros_like(out_ref)

    h = h_ref[...]
    gt = jnp.dot(h, wg_ref[...], preferred_element_type=F32)
    up = jnp.dot(h, wu_ref[...], preferred_element_type=F32)
    act = (gt * jax.nn.sigmoid(gt) * up).astype(BF16)
    out_ref[...] += jnp.dot(act, wd_ref[...], preferred_element_type=F32)

    @pl.when(k == pl.num_programs(1) - 1)
    def _():
        out_ref[...] = x_ref[...] + gate_ref[...] * out_ref[...]


def _ffn(xs, g, mods, layer, w_gu, w_down, n_tiles):
    nh = FFN_HIDDEN // TH
    return pl.pallas_call(
        _ffn_kernel,
        grid=(n_tiles, nh),
        in_specs=[
            pl.BlockSpec((TMF, D_MODEL), lambda i, k: (i, 0)),
            pl.BlockSpec((1, D_MODEL), lambda i, k: (0, 0)),
            _mod_spec(layer, 3, TMF),
            _mod_spec(layer, 4, TMF),
            _mod_spec(layer, 5, TMF),
            pl.BlockSpec((None, D_MODEL, TH), lambda i, k: (layer, 0, k)),
            pl.BlockSpec((None, D_MODEL, TH), lambda i, k: (layer, 0, nh + k)),
            pl.BlockSpec((None, TH, D_MODEL), lambda i, k: (layer, k, 0)),
        ],
        out_specs=pl.BlockSpec((TMF, D_MODEL), lambda i, k: (i, 0)),
        out_shape=jax.ShapeDtypeStruct((n_tiles * TMF, D_MODEL), F32),
        scratch_shapes=[pltpu.VMEM((TMF, D_MODEL), BF16)],
        compiler_params=_params("arbitrary", "arbitrary"),
        name="ffn",
    )(xs, g, mods, mods, mods, w_gu, w_gu, w_down)


def _pair_major(a):
    n_freq = DIFF_DH // 4
    lead = a.shape[:-1]
    a = a.reshape(*lead, a.shape[-1] // DIFF_DH, 2, 2, n_freq)
    return jnp.swapaxes(a, -3, -2).reshape(*lead, -1)


def _rope_tables():
    n_freq = DIFF_DH // 4
    pos = jnp.arange(SEQ)
    freqs = ROPE_BASE ** (-jnp.arange(n_freq, dtype=F32) / n_freq)
    ang = jnp.stack([pos // GRID_W, pos % GRID_W], axis=-1).astype(F32)[:, :, None] * freqs
    cos, sin = jnp.cos(ang), jnp.sin(ang)
    cos_t = jnp.concatenate([cos[:, 0], cos[:, 1], cos[:, 0], cos[:, 1]], axis=-1)
    sin_t = jnp.concatenate([-sin[:, 0], -sin[:, 1], sin[:, 0], sin[:, 1]], axis=-1)
    cos_t = jnp.concatenate([cos_t, jnp.ones((TM, LANES), F32)], axis=0)
    sin_t = jnp.concatenate([sin_t, jnp.zeros((TM, LANES), F32)], axis=0)
    return cos_t, sin_t


def kernel(x, c, ctx, c_ctx, ada_w, ada_b, norm_g, mlstm_w_in, mlstm_gate_b, mlstm_head_g, mlstm_w_out,
           diff_w_in, diff_w_out, diff_q_g, diff_k_g, diff_lq1, diff_lk1, diff_lq2, diff_lk2, diff_subln_g,
           ffn_w_gu, ffn_w_down):
    assert x.shape == (BATCH, SEQ, D_MODEL) and ctx.shape == (BATCH, CTX_LEN, D_MODEL)
    xs = jnp.concatenate([x.reshape(N_LAT, D_MODEL), ctx.reshape(N_CTX, D_MODEL)], axis=0)
    cc = jnp.concatenate([c, c_ctx[None], jnp.zeros((8 - BATCH - 1, D_MODEL), F32)], axis=0)
    mods = _ada(cc, ada_w, ada_b).reshape(DEPTH, 8, 1, 6 * D_MODEL)
    cos_t, sin_t = _rope_tables()
    row = lambda v: v.reshape(1, -1)
    ffn_gu = ffn_w_gu.astype(BF16)
    ffn_down = ffn_w_down.astype(BF16)
    mlstm_in = mlstm_w_in.astype(BF16)
    mlstm_out = mlstm_w_out.astype(BF16)
    diff_in = jnp.concatenate([_pair_major(diff_w_in[..., :2 * D_MODEL]), diff_w_in[..., 2 * D_MODEL:]],
                              axis=-1).astype(BF16)
    diff_out = diff_w_out.astype(BF16)

    for i in range(DEPTH):
        last = i == DEPTH - 1
        n_rows = N_LAT if last else NT
        j = i // N_MIXERS
        if i % N_MIXERS == 0:
            w_gate = jnp.pad(mlstm_w_in[j, :, MLSTM_MAIN:], ((0, 0), (0, LANES - N_GATES))).astype(BF16)
            gate_b = jnp.pad(mlstm_gate_b[j], (0, LANES - N_GATES)).reshape(1, LANES)
            qkvo, gates = _inproj_mlstm(xs, row(norm_g[i, 0]), mods, i, mlstm_in, j, w_gate)
            hf = _mlstm_scan(qkvo, gates, gate_b, None, None, reverse=False)
            a = _mlstm_scan(qkvo, gates, gate_b, row(mlstm_head_g[j]), hf, reverse=True)
            w_out = mlstm_out
        else:
            lam_init = 0.8 - 0.6 * math.exp(-0.3 * i)
            qkg = jnp.concatenate([jnp.tile(_pair_major(diff_q_g[j]), 2 * DIFF_HEADS),
                                   jnp.tile(_pair_major(diff_k_g[j]), 2 * DIFF_HEADS)]).reshape(1, 2 * D_MODEL)
            qkv = _inproj_diff(xs, row(norm_g[i, 0]), mods, i, diff_in, j, qkg, cos_t, sin_t)
            lams = [row(diff_lq1[j]), row(diff_lk1[j]), row(diff_lq2[j]), row(diff_lk2[j])]
            a = _attention(qkv, lams, row(diff_subln_g[j]), lam_init, None)
            if not last:
                a = _attention(qkv, lams, row(diff_subln_g[j]), lam_init, a)
            w_out = diff_out
        xs = _outproj(a, w_out, j, xs, mods, i, n_rows // TM)
        xs = _ffn(xs, row(norm_g[i, 1]), mods, i, ffn_gu, ffn_down, n_rows // TMF)
    return xs.reshape(BATCH, SEQ, D_MODEL)
```

```python
import functools
import math

import jax
import jax.numpy as jnp
from jax import lax
from jax.experimental import pallas as pl
from jax.experimental.pallas import tpu as pltpu

F32 = jnp.float32
BF16 = jnp.bfloat16

D_MODEL = 2048
BATCH = 4
SEQ = 4096
DEPTH = 4
GRID_W = 64
CTX_LEN = 256
N_MIXERS = 2

MLSTM_HEADS = 8
MLSTM_DK = D_MODEL // (2 * MLSTM_HEADS)
MLSTM_DV = D_MODEL // MLSTM_HEADS
GATE_CAP = 15.0
MLSTM_QK = MLSTM_HEADS * MLSTM_DK
MLSTM_MAIN = 2 * MLSTM_QK + 2 * D_MODEL
N_GATES = 4 * MLSTM_HEADS

DIFF_HEADS = 8
DIFF_DH = D_MODEL // (2 * DIFF_HEADS)
ROPE_BASE = 10000.0

FFN_HIDDEN = -(-(8 * D_MODEL) // (3 * 256)) * 256
EPS = 1e-6

N_LAT = BATCH * SEQ
N_CTX = BATCH * CTX_LEN
NT = N_LAT + N_CTX

LANES = 128
TM = 1024
TN = 1024
TMF = 512
TH = 512
CHUNK = 256
TQ = 1024
TK = 1024
VMEM_LIMIT = 56 * 1024 * 1024

NT_DIMS = (((1,), (1,)), ((), ()))
TN_DIMS = (((0,), (0,)), ((), ()))


def _params(*sem):
    return pltpu.CompilerParams(dimension_semantics=sem, vmem_limit_bytes=VMEM_LIMIT)


def _mod_row(i, tm):
    return jnp.minimum(i // (SEQ // tm), BATCH)


def _mod_spec(layer, k, tm=TM):
    return pl.BlockSpec((None, None, 1, D_MODEL), lambda i, j: (layer, _mod_row(i, tm), 0, k))


def _weight_spec(layer, tn):
    return pl.BlockSpec((None, D_MODEL, tn), lambda i, j: (layer, 0, j))


def _lane_repeat(x, n, axis):
    assert axis == 1
    return jnp.concatenate([x] * n, axis=1) if n > 1 else x


def _norm_modulate(x, g, shift, scale):
    y = x * lax.rsqrt(jnp.mean(x * x, axis=-1, keepdims=True) + EPS) * g
    return y * (1.0 + scale) + shift


def _ada_kernel(c_ref, w_ref, b_ref, o_ref):
    c = c_ref[...]
    a = (c * jax.nn.sigmoid(c)).astype(BF16)
    o_ref[...] = jnp.dot(a, w_ref[...].astype(BF16), preferred_element_type=F32) + b_ref[...]


def _ada(cc, ada_w, ada_b):
    tn = 1024
    return pl.pallas_call(
        _ada_kernel,
        grid=(DEPTH, 6 * D_MODEL // tn),
        in_specs=[
            pl.BlockSpec((8, D_MODEL), lambda l, j: (0, 0)),
            pl.BlockSpec((None, D_MODEL, tn), lambda l, j: (l, 0, j)),
            pl.BlockSpec((None, 1, tn), lambda l, j: (l, 0, j)),
        ],
        out_specs=pl.BlockSpec((None, 8, tn), lambda l, j: (l, 0, j)),
        out_shape=jax.ShapeDtypeStruct((DEPTH, 8, 6 * D_MODEL), F32),
        compiler_params=_params("arbitrary", "arbitrary"),
        name="ada_mod",
    )(cc, ada_w, ada_b.reshape(DEPTH, 1, 6 * D_MODEL))


def _inproj_mlstm_kernel(x_ref, g_ref, shift_ref, scale_ref, w_ref, wg_ref, out_ref, gate_ref, h_ref):
    j = pl.program_id(1)

    @pl.when(j == 0)
    def _():
        h = _norm_modulate(x_ref[...], g_ref[...], shift_ref[...], scale_ref[...]).astype(BF16)
        h_ref[...] = h
        gate_ref[...] = jnp.dot(h, wg_ref[...], preferred_element_type=F32)

    acc = jnp.dot(h_ref[...], w_ref[...], preferred_element_type=F32)
    qscale = jnp.where(j < MLSTM_QK // TN, MLSTM_DK ** -0.5, 1.0).astype(F32)
    out_ref[...] = (acc * qscale).astype(BF16)


def _inproj_mlstm(xs, g, mods, layer, w_all, w_layer, w_gate):
    return pl.pallas_call(
        _inproj_mlstm_kernel,
        grid=(NT // TM, MLSTM_MAIN // TN),
        in_specs=[
            pl.BlockSpec((TM, D_MODEL), lambda i, j: (i, 0)),
            pl.BlockSpec((1, D_MODEL), lambda i, j: (0, 0)),
            _mod_spec(layer, 0),
            _mod_spec(layer, 1),
            _weight_spec(w_layer, TN),
            pl.BlockSpec((D_MODEL, LANES), lambda i, j: (0, 0)),
        ],
        out_specs=[
            pl.BlockSpec((TM, TN), lambda i, j: (i, j)),
            pl.BlockSpec((TM, LANES), lambda i, j: (i, 0)),
        ],
        out_shape=[
            jax.ShapeDtypeStruct((NT, MLSTM_MAIN), BF16),
            jax.ShapeDtypeStruct((NT, LANES), F32),
        ],
        scratch_shapes=[pltpu.VMEM((TM, D_MODEL), BF16)],
        compiler_params=_params("arbitrary", "arbitrary"),
        name="inproj_mlstm",
    )(xs, g, mods, mods, w_all, w_gate)


def _mlstm_scan_kernel(*refs, reverse, final):
    if final:
        (q_ref, k_ref, v_ref, gate_ref, gb_ref, o_ref, hf_ref, hg_ref,
         out_ref, c_ref, n_ref, m_ref) = refs
    else:
        q_ref, k_ref, v_ref, gate_ref, gb_ref, out_ref, c_ref, n_ref, m_ref = refs
    L = CHUNK

    @pl.when(pl.program_id(1) == 0)
    def _():
        c_ref[...] = jnp.zeros_like(c_ref)
        n_ref[...] = jnp.zeros_like(n_ref)
        m_ref[...] = jnp.zeros_like(m_ref)

    gg = GATE_CAP * jnp.tanh((gate_ref[...] + gb_ref[...]) / GATE_CAP)
    lsig = jax.nn.log_sigmoid(gg)
    t_idx = lax.broadcasted_iota(jnp.int32, (L, L), 0)
    s_idx = lax.broadcasted_iota(jnp.int32, (L, L), 1)
    mask = (s_idx >= t_idx) if reverse else (s_idx <= t_idx)
    bcum = jnp.dot(mask.astype(F32), lsig, precision=lax.Precision.HIGHEST,
                   preferred_element_type=F32)
    bal = pltpu.roll(bcum, LANES - MLSTM_HEADS, 1)
    r = gg - bal
    r_t = r.T
    base = 2 * MLSTM_HEADS if reverse else 0
    last = 0 if reverse else L - 1

    for h in range(MLSTM_HEADS):
        col = base + h
        b_rep = jnp.broadcast_to(bal[:, col:col + 1], (L, LANES))
        r_rep = jnp.broadcast_to(r[:, col:col + 1], (L, LANES))
        r_row = r_t[col:col + 1, :]
        b_last = b_rep[last:last + 1, :]
        m = m_ref[h]

        dmat = jnp.where(mask, _lane_repeat(b_rep, L // LANES, 1) + r_row, -jnp.inf)
        inter = b_rep + m
        m_t = jnp.maximum(inter, jnp.max(dmat, axis=-1, keepdims=True))
        w_intra = jnp.exp(dmat - _lane_repeat(m_t, L // LANES, 1))
        w_inter = jnp.exp(inter - m_t)

        qh = q_ref[:, h * MLSTM_DK:(h + 1) * MLSTM_DK]
        kh = k_ref[:, h * MLSTM_DK:(h + 1) * MLSTM_DK]
        vh = v_ref[:, h * MLSTM_DV:(h + 1) * MLSTM_DV]
        s = lax.dot_general(qh, kh, NT_DIMS, preferred_element_type=F32) * w_intra
        c_old = c_ref[h]
        num = (_lane_repeat(w_inter, MLSTM_DV // LANES, 1)
               * jnp.dot(qh, c_old.astype(BF16), preferred_element_type=F32)
               + jnp.dot(s.astype(BF16), vh, preferred_element_type=F32))
        n_old = n_ref[h]
        qn = jnp.sum(qh.astype(F32) * n_old.astype(BF16).astype(F32), axis=-1, keepdims=True)
        den = w_inter * qn + jnp.sum(s, axis=-1, keepdims=True)
        inv = 1.0 / jnp.maximum(jnp.abs(den), jnp.exp(-m_t))
        hout = num * _lane_repeat(inv, MLSTM_DV // LANES, 1)

        a_rep = b_last + r_rep
        m_new = jnp.maximum(b_last + m, jnp.max(a_rep, axis=0, keepdims=True))
        wk = jnp.exp(a_rep - m_new)
        dec = jnp.exp(b_last + m - m_new)
        kw = kh.astype(F32) * wk
        c_ref[h] = (_lane_repeat(dec, MLSTM_DV // LANES, 1) * c_old
                    + lax.dot_general(kw.astype(BF16), vh, TN_DIMS, preferred_element_type=F32))
        n_ref[h] = dec * n_old + jnp.sum(kw, axis=0, keepdims=True)
        m_ref[h] = m_new

        sl = slice(h * MLSTM_DV, (h + 1) * MLSTM_DV)
        if final:
            hs = hf_ref[:, sl] + hout
            y = hs * lax.rsqrt(jnp.mean(hs * hs, axis=-1, keepdims=True) + EPS) * hg_ref[:, sl]
            out_ref[:, sl] = (y * jax.nn.sigmoid(o_ref[:, sl].astype(F32))).astype(BF16)
        else:
            out_ref[:, sl] = hout


def _mlstm_scan(qkvo, gates, gate_b, head_g, hf, *, reverse):
    final = hf is not None
    n_lat_chunks = SEQ // CHUNK
    ctx_blk0 = N_LAT // CHUNK

    def row_blk(b, c):
        j = (n_lat_chunks - c) if reverse else (c - 1)
        return jnp.where(c == 0, ctx_blk0 + b, b * n_lat_chunks + j)

    in_specs = [
        pl.BlockSpec((CHUNK, MLSTM_QK), lambda b, c: (row_blk(b, c), 0)),
        pl.BlockSpec((CHUNK, MLSTM_QK), lambda b, c: (row_blk(b, c), 1)),
        pl.BlockSpec((CHUNK, D_MODEL), lambda b, c: (row_blk(b, c), 1)),
        pl.BlockSpec((CHUNK, LANES), lambda b, c: (row_blk(b, c), 0)),
        pl.BlockSpec((1, LANES), lambda b, c: (0, 0)),
    ]
    args = [qkvo, qkvo, qkvo, gates, gate_b]
    if final:
        in_specs += [
            pl.BlockSpec((CHUNK, D_MODEL), lambda b, c: (row_blk(b, c), 2)),
            pl.BlockSpec((CHUNK, D_MODEL), lambda b, c: (row_blk(b, c), 0)),
            pl.BlockSpec((1, D_MODEL), lambda b, c: (0, 0)),
        ]
        args += [qkvo, hf, head_g]
    return pl.pallas_call(
        functools.partial(_mlstm_scan_kernel, reverse=reverse, final=final),
        grid=(BATCH, 1 + n_lat_chunks),
        in_specs=in_specs,
        out_specs=pl.BlockSpec((CHUNK, D_MODEL), lambda b, c: (row_blk(b, c), 0)),
        out_shape=jax.ShapeDtypeStruct((NT, D_MODEL), BF16 if final else F32),
        scratch_shapes=[
            pltpu.VMEM((MLSTM_HEADS, MLSTM_DK, MLSTM_DV), F32),
            pltpu.VMEM((MLSTM_HEADS, 1, MLSTM_DK), F32),
            pltpu.VMEM((MLSTM_HEADS, 1, LANES), F32),
        ],
        compiler_params=_params("arbitrary", "arbitrary"),
        name="mlstm_scan_bwd" if reverse else "mlstm_scan_fwd",
    )(*args)


def _inproj_diff_kernel(x_ref, g_ref, shift_ref, scale_ref, w_ref, qkg_ref, cos_ref, sin_ref,
                        out_ref, h_ref):
    j = pl.program_id(1)
    n_qk_blocks = 2 * D_MODEL // TN

    @pl.when(j == 0)
    def _():
        h_ref[...] = _norm_modulate(x_ref[...], g_ref[...], shift_ref[...], scale_ref[...]).astype(BF16)

    acc = jnp.dot(h_ref[...], w_ref[...], preferred_element_type=F32)

    @pl.when(j < n_qk_blocks)
    def _():
        cos = cos_ref[...]
        sin = sin_ref[...]
        for grp in range(TN // LANES):
            sl = slice(grp * LANES, (grp + 1) * LANES)
            xg = acc[:, sl]
            y = xg * lax.rsqrt(jnp.mean(xg * xg, axis=-1, keepdims=True) + EPS) * qkg_ref[:, sl]
            out_ref[:, sl] = (y * cos + pltpu.roll(y, LANES // 2, 1) * sin).astype(BF16)

    @pl.when(j >= n_qk_blocks)
    def _():
        out_ref[...] = acc.astype(BF16)


def _inproj_diff(xs, g, mods, layer, w_all, w_layer, qkg, cos_t, sin_t):
    n_qk_blocks = 2 * D_MODEL // TN
    lat_tiles = SEQ // TM

    def rope_blk(i, j):
        return (jnp.where(i < N_LAT // TM, i % lat_tiles, lat_tiles), 0)

    return pl.pallas_call(
        _inproj_diff_kernel,
        grid=(NT // TM, 3 * D_MODEL // TN),
        in_specs=[
            pl.BlockSpec((TM, D_MODEL), lambda i, j: (i, 0)),
            pl.BlockSpec((1, D_MODEL), lambda i, j: (0, 0)),
            _mod_spec(layer, 0),
            _mod_spec(layer, 1),
            _weight_spec(w_layer, TN),
            pl.BlockSpec((1, TN), lambda i, j: (0, jnp.minimum(j, n_qk_blocks - 1))),
            pl.BlockSpec((TM, LANES), rope_blk),
            pl.BlockSpec((TM, LANES), rope_blk),
        ],
        out_specs=pl.BlockSpec((TM, TN), lambda i, j: (i, j)),
        out_shape=jax.ShapeDtypeStruct((NT, 3 * D_MODEL), BF16),
        scratch_shapes=[pltpu.VMEM((TM, D_MODEL), BF16)],
        compiler_params=_params("arbitrary", "arbitrary"),
        name="inproj_diff",
    )(xs, g, mods, mods, w_all, qkg, cos_t, sin_t)


def _attn_kernel(*refs, n_lat_chunks, lam_init):
    if n_lat_chunks:
        (q_ref, kl_ref, vl_ref, kc_ref, vc_ref, lq1_ref, lk1_ref, lq2_ref, lk2_ref, sg_ref,
         out_ref, acc_ref, m_ref, l_ref) = refs
    else:
        (q_ref, kc_ref, vc_ref, lq1_ref, lk1_ref, lq2_ref, lk2_ref, sg_ref, _,
         out_ref, acc_ref, m_ref, l_ref) = refs
    tq = q_ref.shape[0]
    dv = 2 * DIFF_DH
    q = q_ref[...]

    def scores(kblk):
        return jnp.concatenate(
            [lax.dot_general(q[:, t * DIFF_DH:(t + 1) * DIFF_DH], kblk[:, t * DIFF_DH:(t + 1) * DIFF_DH],
                             NT_DIMS, preferred_element_type=F32) for t in range(2)], axis=0)

    s = scores(kc_ref[...])
    m0 = jnp.max(s, axis=-1, keepdims=True)
    p = jnp.exp2(s - m0)
    m_ref[...] = jnp.broadcast_to(m0, m_ref.shape)
    l_ref[...] = jnp.broadcast_to(jnp.sum(p, axis=-1, keepdims=True), l_ref.shape)
    acc_ref[...] = jnp.dot(p.astype(BF16), vc_ref[...], preferred_element_type=F32)

    if n_lat_chunks:
        def body(c, carry):
            off = pl.multiple_of(c * TK, TK)
            s = scores(kl_ref[pl.ds(off, TK), :])
            m_old = m_ref[...]
            m_new = jnp.maximum(m_old, jnp.max(s, axis=-1, keepdims=True))
            p = jnp.exp2(s - _lane_repeat(m_new, TK // LANES, 1))
            alpha = jnp.exp2(m_old - m_new)
            l_ref[...] = alpha * l_ref[...] + jnp.sum(p, axis=-1, keepdims=True)
            acc_ref[...] = (_lane_repeat(alpha, dv // LANES, 1) * acc_ref[...]
                            + jnp.dot(p.astype(BF16), vl_ref[pl.ds(off, TK), :],
                                      preferred_element_type=F32))
            m_ref[...] = m_new
            return carry
        lax.fori_loop(0, n_lat_chunks, body, 0)

    lam = (jnp.exp(jnp.sum(lq1_ref[...] * lk1_ref[...], axis=-1, keepdims=True))
           - jnp.exp(jnp.sum(lq2_ref[...] * lk2_ref[...], axis=-1, keepdims=True)) + lam_init)
    o_all = acc_ref[...] / _lane_repeat(l_ref[...], dv // LANES, 1)
    o = o_all[:tq] - lam * o_all[tq:]
    y = o * lax.rsqrt(jnp.mean(o * o, axis=-1, keepdims=True) + EPS) * sg_ref[...]
    out_ref[...] = (y * (1.0 - lam_init)).astype(BF16)


def _attention(qkv, lams, subln_g, lam_init, prev_out):
    dv = 2 * DIFF_DH
    kcol = D_MODEL // dv
    vcol = 2 * D_MODEL // dv
    ctx_blk0 = N_LAT // CTX_LEN
    small = [pl.BlockSpec((1, DIFF_DH), lambda b, h, i: (0, 0))] * 4
    small.append(pl.BlockSpec((1, dv), lambda b, h, i: (0, 0)))
    ctx_specs = [
        pl.BlockSpec((CTX_LEN, dv), lambda b, h, i: (ctx_blk0 + b, kcol + h)),
        pl.BlockSpec((CTX_LEN, dv), lambda b, h, i: (ctx_blk0 + b, vcol + h)),
    ]
    if prev_out is None:
        tq, nq = TQ, SEQ // TQ
        q_map = lambda b, h, i: (b * nq + i, h)
        in_specs = [
            pl.BlockSpec((tq, dv), q_map),
            pl.BlockSpec((SEQ, dv), lambda b, h, i: (b, kcol + h)),
            pl.BlockSpec((SEQ, dv), lambda b, h, i: (b, vcol + h)),
        ] + ctx_specs + small
        args = [qkv, qkv, qkv, qkv, qkv] + list(lams) + [subln_g]
        aliases = {}
        n_lat_chunks = SEQ // TK
    else:
        tq, nq = CTX_LEN, 1
        q_map = lambda b, h, i: (ctx_blk0 + b, h)
        in_specs = [pl.BlockSpec((tq, dv), q_map)] + ctx_specs + small
        in_specs.append(pl.BlockSpec(memory_space=pl.ANY))
        args = [qkv, qkv, qkv] + list(lams) + [subln_g, prev_out]
        aliases = {len(args) - 1: 0}
        n_lat_chunks = 0
    return pl.pallas_call(
        functools.partial(_attn_kernel, n_lat_chunks=n_lat_chunks, lam_init=lam_init),
        grid=(BATCH, DIFF_HEADS, nq),
        in_specs=in_specs,
        out_specs=pl.BlockSpec((tq, dv), q_map),
        out_shape=jax.ShapeDtypeStruct((NT, D_MODEL), BF16),
        scratch_shapes=[
            pltpu.VMEM((2 * tq, dv), F32),
            pltpu.VMEM((2 * tq, LANES), F32),
            pltpu.VMEM((2 * tq, LANES), F32),
        ],
        input_output_aliases=aliases,
        compiler_params=_params("arbitrary", "arbitrary", "arbitrary"),
        name="diff_attn_ctx" if n_lat_chunks == 0 else "diff_attn",
    )(*args)


def _outproj_kernel(a_ref, w_ref, x_ref, gate_ref, out_ref):
    y = jnp.dot(a_ref[...], w_ref[...], preferred_element_type=F32)
    out_ref[...] = x_ref[...] + gate_ref[...] * y


def _outproj(a, w_all, w_layer, xs, mods, layer, n_tiles):
    nj = D_MODEL // TN
    return pl.pallas_call(
        _outproj_kernel,
        grid=(n_tiles, nj),
        in_specs=[
            pl.BlockSpec((TM, D_MODEL), lambda i, j: (i, 0)),
            _weight_spec(w_layer, TN),
            pl.BlockSpec((TM, TN), lambda i, j: (i, j)),
            pl.BlockSpec((None, None, 1, TN), lambda i, j: (layer, _mod_row(i, TM), 0, 2 * nj + j)),
        ],
        out_specs=pl.BlockSpec((TM, TN), lambda i, j: (i, j)),
        out_shape=jax.ShapeDtypeStruct((NT, D_MODEL), F32),
        input_output_aliases={2: 0},
        compiler_params=_params("arbitrary", "arbitrary"),
        name="outproj",
    )(a, w_all, xs, mods)


def _ffn_kernel(x_ref, g_ref, shift_ref, scale_ref, gate_ref, wg_ref, wu_ref, wd_ref, out_ref, h_ref):
    k = pl.program_id(1)

    @pl.when(k == 0)
    def _():
        h_ref[...] = _norm_modulate(x_ref[...], g_ref[...], shift_ref[...], scale_ref[...]).astype(BF16)
        out_ref[...] = jnp.zeros_like(out_ref)

    h = h_ref[...]
    gt = jnp.dot(h, wg_ref[...], preferred_element_type=F32)
    up = jnp.dot(h, wu_ref[...], preferred_element_type=F32)
    act = (gt * jax.nn.sigmoid(gt) * up).astype(BF16)
    out_ref[...] += jnp.dot(act, wd_ref[...], preferred_element_type=F32)

    @pl.when(k == pl.num_programs(1) - 1)
    def _():
        out_ref[...] = x_ref[...] + gate_ref[...] * out_ref[...]


def _ffn(xs, g, mods, layer, w_gu, w_down, n_tiles):
    nh = FFN_HIDDEN // TH
    return pl.pallas_call(
        _ffn_kernel,
        grid=(n_tiles, nh),
        in_specs=[
            pl.BlockSpec((TMF, D_MODEL), lambda i, k: (i, 0)),
            pl.BlockSpec((1, D_MODEL), lambda i, k: (0, 0)),
            _mod_spec(layer, 3, TMF),
            _mod_spec(layer, 4, TMF),
            _mod_spec(layer, 5, TMF),
            pl.BlockSpec((None, D_MODEL, TH), lambda i, k: (layer, 0, k)),
            pl.BlockSpec((None, D_MODEL, TH), lambda i, k: (layer, 0, nh + k)),
            pl.BlockSpec((None, TH, D_MODEL), lambda i, k: (layer, k, 0)),
        ],
        out_specs=pl.BlockSpec((TMF, D_MODEL), lambda i, k: (i, 0)),
        out_shape=jax.ShapeDtypeStruct((n_tiles * TMF, D_MODEL), F32),
        scratch_shapes=[pltpu.VMEM((TMF, D_MODEL), BF16)],
        compiler_params=_params("arbitrary", "arbitrary"),
        name="ffn",
    )(xs, g, mods, mods, mods, w_gu, w_gu, w_down)


def _pair_major(a):
    n_freq = DIFF_DH // 4
    lead = a.shape[:-1]
    a = a.reshape(*lead, a.shape[-1] // DIFF_DH, 2, 2, n_freq)
    return jnp.swapaxes(a, -3, -2).reshape(*lead, -1)


def _rope_tables():
    n_freq = DIFF_DH // 4
    pos = jnp.arange(SEQ)
    freqs = ROPE_BASE ** (-jnp.arange(n_freq, dtype=F32) / n_freq)
    ang = jnp.stack([pos // GRID_W, pos % GRID_W], axis=-1).astype(F32)[:, :, None] * freqs
    cos, sin = jnp.cos(ang), jnp.sin(ang)
    cos_t = jnp.concatenate([cos[:, 0], cos[:, 1], cos[:, 0], cos[:, 1]], axis=-1)
    sin_t = jnp.concatenate([-sin[:, 0], -sin[:, 1], sin[:, 0], sin[:, 1]], axis=-1)
    cos_t = jnp.concatenate([cos_t, jnp.ones((TM, LANES), F32)], axis=0)
    sin_t = jnp.concatenate([sin_t, jnp.zeros((TM, LANES), F32)], axis=0)
    return cos_t, sin_t


def kernel(x, c, ctx, c_ctx, ada_w, ada_b, norm_g, mlstm_w_in, mlstm_gate_b, mlstm_head_g, mlstm_w_out,
           diff_w_in, diff_w_out, diff_q_g, diff_k_g, diff_lq1, diff_lk1, diff_lq2, diff_lk2, diff_subln_g,
           ffn_w_gu, ffn_w_down):
    assert x.shape == (BATCH, SEQ, D_MODEL) and ctx.shape == (BATCH, CTX_LEN, D_MODEL)
    xs = jnp.concatenate([x.reshape(N_LAT, D_MODEL), ctx.reshape(N_CTX, D_MODEL)], axis=0)
    cc = jnp.concatenate([c, c_ctx[None], jnp.zeros((8 - BATCH - 1, D_MODEL), F32)], axis=0)
    mods = _ada(cc, ada_w, ada_b).reshape(DEPTH, 8, 1, 6 * D_MODEL)
    cos_t, sin_t = _rope_tables()
    row = lambda v: v.reshape(1, -1)
    ffn_gu = ffn_w_gu.astype(BF16)
    ffn_down = ffn_w_down.astype(BF16)
    mlstm_in = mlstm_w_in.astype(BF16)
    mlstm_out = mlstm_w_out.astype(BF16)
    diff_in = jnp.concatenate([_pair_major(diff_w_in[..., :2 * D_MODEL]), diff_w_in[..., 2 * D_MODEL:]],
                              axis=-1).astype(BF16)
    diff_out = diff_w_out.astype(BF16)

    for i in range(DEPTH):
        last = i == DEPTH - 1
        n_rows = N_LAT if last else NT
        j = i // N_MIXERS
        if i % N_MIXERS == 0:
            w_gate = jnp.pad(mlstm_w_in[j, :, MLSTM_MAIN:], ((0, 0), (0, LANES - N_GATES))).astype(BF16)
            gate_b = jnp.pad(mlstm_gate_b[j], (0, LANES - N_GATES)).reshape(1, LANES)
            qkvo, gates = _inproj_mlstm(xs, row(norm_g[i, 0]), mods, i, mlstm_in, j, w_gate)
            hf = _mlstm_scan(qkvo, gates, gate_b, None, None, reverse=False)
            a = _mlstm_scan(qkvo, gates, gate_b, row(mlstm_head_g[j]), hf, reverse=True)
            w_out = mlstm_out
        else:
            lam_init = 0.8 - 0.6 * math.exp(-0.3 * i)
            q_gain = _pair_major(diff_q_g[j]) * (DIFF_DH ** -0.5 * math.log2(math.e))
            qkg = jnp.concatenate([jnp.tile(q_gain, 2 * DIFF_HEADS),
                                   jnp.tile(_pair_major(diff_k_g[j]), 2 * DIFF_HEADS)]).reshape(1, 2 * D_MODEL)
            qkv = _inproj_diff(xs, row(norm_g[i, 0]), mods, i, diff_in, j, qkg, cos_t, sin_t)
            lams = [row(diff_lq1[j]), row(diff_lk1[j]), row(diff_lq2[j]), row(diff_lk2[j])]
            a = _attention(qkv, lams, row(diff_subln_g[j]), lam_init, None)
            if not last:
                a = _attention(qkv, lams, row(diff_subln_g[j]), lam_init, a)
            w_out = diff_out
        xs = _outproj(a, w_out, j, xs, mods, i, n_rows // TM)
        xs = _ffn(xs, row(norm_g[i, 1]), mods, i, ffn_gu, ffn_down, n_rows // TMF)
    return xs.reshape(BATCH, SEQ, D_MODEL)
```

```python
import functools
import math

import jax
import jax.numpy as jnp
from jax import lax
from jax.experimental import pallas as pl
from jax.experimental.pallas import tpu as pltpu

F32 = jnp.float32
BF16 = jnp.bfloat16

D_MODEL = 2048
BATCH = 4
SEQ = 4096
DEPTH = 4
GRID_W = 64
CTX_LEN = 256
N_MIXERS = 2

MLSTM_HEADS = 8
MLSTM_DK = D_MODEL // (2 * MLSTM_HEADS)
MLSTM_DV = D_MODEL // MLSTM_HEADS
GATE_CAP = 15.0
MLSTM_QK = MLSTM_HEADS * MLSTM_DK
MLSTM_MAIN = 2 * MLSTM_QK + 2 * D_MODEL
N_GATES = 4 * MLSTM_HEADS

DIFF_HEADS = 8
DIFF_DH = D_MODEL // (2 * DIFF_HEADS)
ROPE_BASE = 10000.0

FFN_HIDDEN = -(-(8 * D_MODEL) // (3 * 256)) * 256
EPS = 1e-6

N_LAT = BATCH * SEQ
N_CTX = BATCH * CTX_LEN
NT = N_LAT + N_CTX

LANES = 128
TM = 1024
TN = 1024
TMF = 512
TH = 512
CHUNK = 256
TQ = 1024
TK = 1024
VMEM_LIMIT = 56 * 1024 * 1024

NT_DIMS = (((1,), (1,)), ((), ()))
TN_DIMS = (((0,), (0,)), ((), ()))


def _params(*sem):
    return pltpu.CompilerParams(dimension_semantics=sem, vmem_limit_bytes=VMEM_LIMIT)


def _mod_row(i, tm):
    return jnp.minimum(i // (SEQ // tm), BATCH)


def _mod_spec(layer, k, tm=TM):
    return pl.BlockSpec((None, None, 1, D_MODEL), lambda i, j: (layer, _mod_row(i, tm), 0, k))


def _weight_spec(layer, tn):
    return pl.BlockSpec((None, D_MODEL, tn), lambda i, j: (layer, 0, j))


def _lane_repeat(x, n, axis):
    assert axis == 1
    return jnp.concatenate([x] * n, axis=1) if n > 1 else x


def _norm_modulate(x, g, shift, scale):
    y = x * lax.rsqrt(jnp.mean(x * x, axis=-1, keepdims=True) + EPS) * g
    return y * (1.0 + scale) + shift


def _ada_kernel(c_ref, w_ref, b_ref, o_ref):
    c = c_ref[...]
    a = (c * jax.nn.sigmoid(c)).astype(BF16)
    o_ref[...] = jnp.dot(a, w_ref[...].astype(BF16), preferred_element_type=F32) + b_ref[...]


def _ada(cc, ada_w, ada_b):
    tn = 1024
    return pl.pallas_call(
        _ada_kernel,
        grid=(DEPTH, 6 * D_MODEL // tn),
        in_specs=[
            pl.BlockSpec((8, D_MODEL), lambda l, j: (0, 0)),
            pl.BlockSpec((None, D_MODEL, tn), lambda l, j: (l, 0, j)),
            pl.BlockSpec((None, 1, tn), lambda l, j: (l, 0, j)),
        ],
        out_specs=pl.BlockSpec((None, 8, tn), lambda l, j: (l, 0, j)),
        out_shape=jax.ShapeDtypeStruct((DEPTH, 8, 6 * D_MODEL), F32),
        compiler_params=_params("arbitrary", "arbitrary"),
        name="ada_mod",
    )(cc, ada_w, ada_b.reshape(DEPTH, 1, 6 * D_MODEL))


MLSTM_ACT = MLSTM_MAIN - MLSTM_QK
Q_TILE = MLSTM_ACT // TN - 1


def _inproj_mlstm_kernel(x_ref, g_ref, shift_ref, scale_ref, w_ref, wkt_ref, wg_ref,
                         out_ref, kt_ref, gate_ref, h_ref):
    j = pl.program_id(1)

    @pl.when(j == 0)
    def _():
        h = _norm_modulate(x_ref[...], g_ref[...], shift_ref[...], scale_ref[...]).astype(BF16)
        h_ref[...] = h
        gate_ref[...] = jnp.dot(h, wg_ref[...], preferred_element_type=F32)

    @pl.when(j == 1)
    def _():
        kt_ref[...] = lax.dot_general(wkt_ref[...], h_ref[...], NT_DIMS,
                                      preferred_element_type=F32).astype(BF16)

    @pl.when(j != 1)
    def _():
        acc = jnp.dot(h_ref[...], w_ref[...], preferred_element_type=F32)
        qscale = jnp.where(j == 0, MLSTM_DK ** -0.5, 1.0).astype(F32)
        out_ref[...] = (acc * qscale).astype(BF16)


def _inproj_mlstm(xs, g, mods, layer, w_all, wkt_all, w_layer, w_gate):
    assert TN == MLSTM_QK
    return pl.pallas_call(
        _inproj_mlstm_kernel,
        grid=(NT // TM, MLSTM_MAIN // TN),
        in_specs=[
            pl.BlockSpec((TM, D_MODEL), lambda i, j: (i, 0)),
            pl.BlockSpec((1, D_MODEL), lambda i, j: (0, 0)),
            _mod_spec(layer, 0),
            _mod_spec(layer, 1),
            pl.BlockSpec((None, D_MODEL, TN), lambda i, j: (w_layer, 0, jnp.where(j == 1, 0, j))),
            pl.BlockSpec((None, MLSTM_QK, D_MODEL), lambda i, j: (w_layer, 0, 0),
                         pipeline_mode=pl.Buffered(1)),
            pl.BlockSpec((D_MODEL, LANES), lambda i, j: (0, 0)),
        ],
        out_specs=[
            pl.BlockSpec((TM, TN), lambda i, j: (i, jnp.where(j <= 1, Q_TILE, j - 2))),
            pl.BlockSpec((MLSTM_QK, TM), lambda i, j: (0, i)),
            pl.BlockSpec((TM, LANES), lambda i, j: (i, 0)),
        ],
        out_shape=[
            jax.ShapeDtypeStruct((NT, MLSTM_ACT), BF16),
            jax.ShapeDtypeStruct((MLSTM_QK, NT), BF16),
            jax.ShapeDtypeStruct((NT, LANES), F32),
        ],
        scratch_shapes=[pltpu.VMEM((TM, D_MODEL), BF16)],
        compiler_params=_params("arbitrary", "arbitrary"),
        name="inproj_mlstm",
    )(xs, g, mods, mods, w_all, wkt_all, w_gate)


def _mlstm_scan_kernel(*refs, reverse, final):
    if final:
        (q_ref, kt_ref, v_ref, gate_ref, gb_ref, o_ref, hf_ref, hg_ref,
         out_ref, c_ref, n_ref, m_ref) = refs
    else:
        q_ref, kt_ref, v_ref, gate_ref, gb_ref, out_ref, c_ref, n_ref, m_ref = refs
    L = CHUNK

    @pl.when(pl.program_id(1) == 0)
    def _():
        c_ref[...] = jnp.zeros_like(c_ref)
        n_ref[...] = jnp.zeros_like(n_ref)
        m_ref[...] = jnp.zeros_like(m_ref)

    gg = GATE_CAP * jnp.tanh((gate_ref[...] + gb_ref[...]) / GATE_CAP)
    lsig = jax.nn.log_sigmoid(gg)
    t_idx = lax.broadcasted_iota(jnp.int32, (L, L), 0)
    s_idx = lax.broadcasted_iota(jnp.int32, (L, L), 1)
    mask = (s_idx >= t_idx) if reverse else (s_idx <= t_idx)
    bcum = jnp.dot(mask.astype(F32), lsig, precision=lax.Precision.HIGHEST,
                   preferred_element_type=F32)
    bal = pltpu.roll(bcum, LANES - MLSTM_HEADS, 1)
    r_t = (gg - bal).T
    base = 2 * MLSTM_HEADS if reverse else 0
    last = 0 if reverse else L - 1

    for h in range(MLSTM_HEADS):
        col = base + h
        b_rep = jnp.broadcast_to(bal[:, col:col + 1], (L, LANES))
        r_row = r_t[col:col + 1, :]
        m = m_ref[h]
        m11 = m[:, 0:1]
        b_last = b_rep[last:last + 1, 0:1]

        dmat = jnp.where(mask, _lane_repeat(b_rep, L // LANES, 1) + r_row, -jnp.inf)
        inter = b_rep + m
        m_t = jnp.maximum(inter, jnp.max(dmat, axis=-1, keepdims=True))
        w_intra = jnp.exp(dmat - _lane_repeat(m_t, L // LANES, 1))
        w_inter = jnp.exp(inter - m_t)

        qh = q_ref[:, h * MLSTM_DK:(h + 1) * MLSTM_DK]
        kth = kt_ref[h * MLSTM_DK:(h + 1) * MLSTM_DK, :]
        vh = v_ref[:, h * MLSTM_DV:(h + 1) * MLSTM_DV]
        s = jnp.dot(qh, kth, preferred_element_type=F32) * w_intra
        c_old = c_ref[h]
        num = (_lane_repeat(w_inter, MLSTM_DV // LANES, 1)
               * jnp.dot(qh, c_old.astype(BF16), preferred_element_type=F32)
               + jnp.dot(s.astype(BF16), vh, preferred_element_type=F32))
        n_old = n_ref[h]
        qn = jnp.dot(qh, n_old.astype(BF16), preferred_element_type=F32)
        den = w_inter * qn + jnp.sum(s, axis=-1, keepdims=True)
        inv = 1.0 / jnp.maximum(jnp.abs(den), jnp.exp(-m_t))
        hout = num * _lane_repeat(inv, MLSTM_DV // LANES, 1)

        a_row = b_last + r_row
        m_new = jnp.maximum(b_last + m11, jnp.max(a_row, axis=-1, keepdims=True))
        wk = jnp.exp(a_row - m_new)
        dec = jnp.exp(b_last + m11 - m_new)
        kw_t = kth.astype(F32) * wk
        c_ref[h] = dec * c_old + jnp.dot(kw_t.astype(BF16), vh, preferred_element_type=F32)
        n_ref[h] = dec * n_old + jnp.sum(kw_t, axis=-1, keepdims=True)
        m_ref[h] = jnp.broadcast_to(m_new, (1, LANES))

        sl = slice(h * MLSTM_DV, (h + 1) * MLSTM_DV)
        if final:
            hs = hf_ref[:, sl] + hout
            y = hs * lax.rsqrt(jnp.mean(hs * hs, axis=-1, keepdims=True) + EPS) * hg_ref[:, sl]
            out_ref[:, sl] = (y * jax.nn.sigmoid(o_ref[:, sl].astype(F32))).astype(BF16)
        else:
            out_ref[:, sl] = hout


def _mlstm_scan(act, k_t, gates, gate_b, head_g, hf, *, reverse):
    final = hf is not None
    n_lat_chunks = SEQ // CHUNK
    ctx_blk0 = N_LAT // CHUNK

    def row_blk(b, c):
        j = (n_lat_chunks - c) if reverse else (c - 1)
        return jnp.where(c == 0, ctx_blk0 + b, b * n_lat_chunks + j)

    in_specs = [
        pl.BlockSpec((CHUNK, MLSTM_QK), lambda b, c: (row_blk(b, c), Q_TILE)),
        pl.BlockSpec((MLSTM_QK, CHUNK), lambda b, c: (0, row_blk(b, c))),
        pl.BlockSpec((CHUNK, D_MODEL), lambda b, c: (row_blk(b, c), 0)),
        pl.BlockSpec((CHUNK, LANES), lambda b, c: (row_blk(b, c), 0)),
        pl.BlockSpec((1, LANES), lambda b, c: (0, 0)),
    ]
    args = [act, k_t, act, gates, gate_b]
    if final:
        in_specs += [
            pl.BlockSpec((CHUNK, D_MODEL), lambda b, c: (row_blk(b, c), 1)),
            pl.BlockSpec((CHUNK, D_MODEL), lambda b, c: (row_blk(b, c), 0)),
            pl.BlockSpec((1, D_MODEL), lambda b, c: (0, 0)),
        ]
        args += [act, hf, head_g]
    return pl.pallas_call(
        functools.partial(_mlstm_scan_kernel, reverse=reverse, final=final),
        grid=(BATCH, 1 + n_lat_chunks),
        in_specs=in_specs,
        out_specs=pl.BlockSpec((CHUNK, D_MODEL), lambda b, c: (row_blk(b, c), 0)),
        out_shape=jax.ShapeDtypeStruct((NT, D_MODEL), BF16 if final else F32),
        scratch_shapes=[
            pltpu.VMEM((MLSTM_HEADS, MLSTM_DK, MLSTM_DV), F32),
            pltpu.VMEM((MLSTM_HEADS, MLSTM_DK, LANES), F32),
            pltpu.VMEM((MLSTM_HEADS, 1, LANES), F32),
        ],
        compiler_params=_params("arbitrary", "arbitrary"),
        name="mlstm_scan_bwd" if reverse else "mlstm_scan_fwd",
    )(*args)


def _inproj_diff_kernel(x_ref, g_ref, shift_ref, scale_ref, w_ref, qkg_ref, cos_ref, sin_ref,
                        out_ref, h_ref):
    j = pl.program_id(1)
    n_qk_blocks = 2 * D_MODEL // TN

    @pl.when(j == 0)
    def _():
        h_ref[...] = _norm_modulate(x_ref[...], g_ref[...], shift_ref[...], scale_ref[...]).astype(BF16)

    acc = jnp.dot(h_ref[...], w_ref[...], preferred_element_type=F32)

    @pl.when(j < n_qk_blocks)
    def _():
        cos = cos_ref[...]
        sin = sin_ref[...]
        for grp in range(TN // LANES):
            sl = slice(grp * LANES, (grp + 1) * LANES)
            xg = acc[:, sl]
            y = xg * lax.rsqrt(jnp.mean(xg * xg, axis=-1, keepdims=True) + EPS) * qkg_ref[:, sl]
            out_ref[:, sl] = (y * cos + pltpu.roll(y, LANES // 2, 1) * sin).astype(BF16)

    @pl.when(j >= n_qk_blocks)
    def _():
        out_ref[...] = acc.astype(BF16)


def _inproj_diff(xs, g, mods, layer, w_all, w_layer, qkg, cos_t, sin_t):
    n_qk_blocks = 2 * D_MODEL // TN
    lat_tiles = SEQ // TM

    def rope_blk(i, j):
        return (jnp.where(i < N_LAT // TM, i % lat_tiles, lat_tiles), 0)

    return pl.pallas_call(
        _inproj_diff_kernel,
        grid=(NT // TM, 3 * D_MODEL // TN),
        in_specs=[
            pl.BlockSpec((TM, D_MODEL), lambda i, j: (i, 0)),
            pl.BlockSpec((1, D_MODEL), lambda i, j: (0, 0)),
            _mod_spec(layer, 0),
            _mod_spec(layer, 1),
            _weight_spec(w_layer, TN),
            pl.BlockSpec((1, TN), lambda i, j: (0, jnp.minimum(j, n_qk_blocks - 1))),
            pl.BlockSpec((TM, LANES), rope_blk),
            pl.BlockSpec((TM, LANES), rope_blk),
        ],
        out_specs=pl.BlockSpec((TM, TN), lambda i, j: (i, j)),
        out_shape=jax.ShapeDtypeStruct((NT, 3 * D_MODEL), BF16),
        scratch_shapes=[pltpu.VMEM((TM, D_MODEL), BF16)],
        compiler_params=_params("arbitrary", "arbitrary"),
        name="inproj_diff",
    )(xs, g, mods, mods, w_all, qkg, cos_t, sin_t)


def _attn_kernel(*refs, n_lat_chunks, lam_init):
    if n_lat_chunks:
        (q_ref, kl_ref, vl_ref, kc_ref, vc_ref, lq1_ref, lk1_ref, lq2_ref, lk2_ref, sg_ref,
         out_ref, acc_ref, m_ref, l_ref) = refs
    else:
        (q_ref, kc_ref, vc_ref, lq1_ref, lk1_ref, lq2_ref, lk2_ref, sg_ref, _,
         out_ref, acc_ref, m_ref, l_ref) = refs
    tq = q_ref.shape[0]
    dv = 2 * DIFF_DH
    q = q_ref[...]

    def scores(kblk):
        return jnp.concatenate(
            [lax.dot_general(q[:, t * DIFF_DH:(t + 1) * DIFF_DH], kblk[:, t * DIFF_DH:(t + 1) * DIFF_DH],
                             NT_DIMS, preferred_element_type=F32) for t in range(2)], axis=0)

    s = scores(kc_ref[...])
    m0 = jnp.max(s, axis=-1, keepdims=True)
    p = jnp.exp2(s - m0)
    m_ref[...] = jnp.broadcast_to(m0, m_ref.shape)
    l_ref[...] = jnp.broadcast_to(jnp.sum(p, axis=-1, keepdims=True), l_ref.shape)
    acc_ref[...] = jnp.dot(p.astype(BF16), vc_ref[...], preferred_element_type=F32)

    if n_lat_chunks:
        def body(c, carry):
            off = pl.multiple_of(c * TK, TK)
            s = scores(kl_ref[pl.ds(off, TK), :])
            m_old = m_ref[...]
            m_new = jnp.maximum(m_old, jnp.max(s, axis=-1, keepdims=True))
            p = jnp.exp2(s - _lane_repeat(m_new, TK // LANES, 1))
            alpha = jnp.exp2(m_old - m_new)
            l_ref[...] = alpha * l_ref[...] + jnp.sum(p, axis=-1, keepdims=True)
            acc_ref[...] = (_lane_repeat(alpha, dv // LANES, 1) * acc_ref[...]
                            + jnp.dot(p.astype(BF16), vl_ref[pl.ds(off, TK), :],
                                      preferred_element_type=F32))
            m_ref[...] = m_new
            return carry
        lax.fori_loop(0, n_lat_chunks, body, 0)

    lam = (jnp.exp(jnp.sum(lq1_ref[...] * lk1_ref[...], axis=-1, keepdims=True))
           - jnp.exp(jnp.sum(lq2_ref[...] * lk2_ref[...], axis=-1, keepdims=True)) + lam_init)
    o_all = acc_ref[...] / _lane_repeat(l_ref[...], dv // LANES, 1)
    o = o_all[:tq] - lam * o_all[tq:]
    y = o * lax.rsqrt(jnp.mean(o * o, axis=-1, keepdims=True) + EPS) * sg_ref[...]
    out_ref[...] = (y * (1.0 - lam_init)).astype(BF16)


def _attention(qkv, lams, subln_g, lam_init, prev_out):
    dv = 2 * DIFF_DH
    kcol = D_MODEL // dv
    vcol = 2 * D_MODEL // dv
    ctx_blk0 = N_LAT // CTX_LEN
    small = [pl.BlockSpec((1, DIFF_DH), lambda b, h, i: (0, 0))] * 4
    small.append(pl.BlockSpec((1, dv), lambda b, h, i: (0, 0)))
    ctx_specs = [
        pl.BlockSpec((CTX_LEN, dv), lambda b, h, i: (ctx_blk0 + b, kcol + h)),
        pl.BlockSpec((CTX_LEN, dv), lambda b, h, i: (ctx_blk0 + b, vcol + h)),
    ]
    if prev_out is None:
        tq, nq = TQ, SEQ // TQ
        q_map = lambda b, h, i: (b * nq + i, h)
        in_specs = [
            pl.BlockSpec((tq, dv), q_map),
            pl.BlockSpec((SEQ, dv), lambda b, h, i: (b, kcol + h)),
            pl.BlockSpec((SEQ, dv), lambda b, h, i: (b, vcol + h)),
        ] + ctx_specs + small
        args = [qkv, qkv, qkv, qkv, qkv] + list(lams) + [subln_g]
        aliases = {}
        n_lat_chunks = SEQ // TK
    else:
        tq, nq = CTX_LEN, 1
        q_map = lambda b, h, i: (ctx_blk0 + b, h)
        in_specs = [pl.BlockSpec((tq, dv), q_map)] + ctx_specs + small
        in_specs.append(pl.BlockSpec(memory_space=pl.ANY))
        args = [qkv, qkv, qkv] + list(lams) + [subln_g, prev_out]
        aliases = {len(args) - 1: 0}
        n_lat_chunks = 0
    return pl.pallas_call(
        functools.partial(_attn_kernel, n_lat_chunks=n_lat_chunks, lam_init=lam_init),
        grid=(BATCH, DIFF_HEADS, nq),
        in_specs=in_specs,
        out_specs=pl.BlockSpec((tq, dv), q_map),
        out_shape=jax.ShapeDtypeStruct((NT, D_MODEL), BF16),
        scratch_shapes=[
            pltpu.VMEM((2 * tq, dv), F32),
            pltpu.VMEM((2 * tq, LANES), F32),
            pltpu.VMEM((2 * tq, LANES), F32),
        ],
        input_output_aliases=aliases,
        compiler_params=_params("arbitrary", "arbitrary", "arbitrary"),
        name="diff_attn_ctx" if n_lat_chunks == 0 else "diff_attn",
    )(*args)


def _outproj_kernel(a_ref, w_ref, x_ref, gate_ref, out_ref):
    y = jnp.dot(a_ref[...], w_ref[...], preferred_element_type=F32)
    out_ref[...] = x_ref[...] + gate_ref[...] * y


def _outproj(a, w_all, w_layer, xs, mods, layer, n_tiles):
    nj = D_MODEL // TN
    return pl.pallas_call(
        _outproj_kernel,
        grid=(n_tiles, nj),
        in_specs=[
            pl.BlockSpec((TM, D_MODEL), lambda i, j: (i, 0)),
            _weight_spec(w_layer, TN),
            pl.BlockSpec((TM, TN), lambda i, j: (i, j)),
            pl.BlockSpec((None, None, 1, TN), lambda i, j: (layer, _mod_row(i, TM), 0, 2 * nj + j)),
        ],
        out_specs=pl.BlockSpec((TM, TN), lambda i, j: (i, j)),
        out_shape=jax.ShapeDtypeStruct((NT, D_MODEL), F32),
        input_output_aliases={2: 0},
        compiler_params=_params("arbitrary", "arbitrary"),
        name="outproj",
    )(a, w_all, xs, mods)


def _ffn_kernel(x_ref, g_ref, shift_ref, scale_ref, gate_ref, wg_ref, wu_ref, wd_ref, out_ref, h_ref):
    k = pl.program_id(1)

    @pl.when(k == 0)
    def _():
        h_ref[...] = _norm_modulate(x_ref[...], g_ref[...], shift_ref[...], scale_ref[...]).astype(BF16)
        out_ref[...] = jnp.zeros_like(out_ref)

    h = h_ref[...]
    gt = jnp.dot(h, wg_ref[...], preferred_element_type=F32)
    up = jnp.dot(h, wu_ref[...], preferred_element_type=F32)
    act = (gt * jax.nn.sigmoid(gt) * up).astype(BF16)
    out_ref[...] += jnp.dot(act, wd_ref[...], preferred_element_type=F32)

    @pl.when(k == pl.num_programs(1) - 1)
    def _():
        out_ref[...] = x_ref[...] + gate_ref[...] * out_ref[...]


def _ffn(xs, g, mods, layer, w_gu, w_down, n_tiles):
    nh = FFN_HIDDEN // TH
    return pl.pallas_call(
        _ffn_kernel,
        grid=(n_tiles, nh),
        in_specs=[
            pl.BlockSpec((TMF, D_MODEL), lambda i, k: (i, 0)),
            pl.BlockSpec((1, D_MODEL), lambda i, k: (0, 0)),
            _mod_spec(layer, 3, TMF),
            _mod_spec(layer, 4, TMF),
            _mod_spec(layer, 5, TMF),
            pl.BlockSpec((None, D_MODEL, TH), lambda i, k: (layer, 0, k)),
            pl.BlockSpec((None, D_MODEL, TH), lambda i, k: (layer, 0, nh + k)),
            pl.BlockSpec((None, TH, D_MODEL), lambda i, k: (layer, k, 0)),
        ],
        out_specs=pl.BlockSpec((TMF, D_MODEL), lambda i, k: (i, 0)),
        out_shape=jax.ShapeDtypeStruct((n_tiles * TMF, D_MODEL), F32),
        scratch_shapes=[pltpu.VMEM((TMF, D_MODEL), BF16)],
        compiler_params=_params("arbitrary", "arbitrary"),
        name="ffn",
    )(xs, g, mods, mods, mods, w_gu, w_gu, w_down)


def _pair_major(a):
    n_freq = DIFF_DH // 4
    lead = a.shape[:-1]
    a = a.reshape(*lead, a.shape[-1] // DIFF_DH, 2, 2, n_freq)
    return jnp.swapaxes(a, -3, -2).reshape(*lead, -1)


def _rope_tables():
    n_freq = DIFF_DH // 4
    pos = jnp.arange(SEQ)
    freqs = ROPE_BASE ** (-jnp.arange(n_freq, dtype=F32) / n_freq)
    ang = jnp.stack([pos // GRID_W, pos % GRID_W], axis=-1).astype(F32)[:, :, None] * freqs
    cos, sin = jnp.cos(ang), jnp.sin(ang)
    cos_t = jnp.concatenate([cos[:, 0], cos[:, 1], cos[:, 0], cos[:, 1]], axis=-1)
    sin_t = jnp.concatenate([-sin[:, 0], -sin[:, 1], sin[:, 0], sin[:, 1]], axis=-1)
    cos_t = jnp.concatenate([cos_t, jnp.ones((TM, LANES), F32)], axis=0)
    sin_t = jnp.concatenate([sin_t, jnp.zeros((TM, LANES), F32)], axis=0)
    return cos_t, sin_t


def kernel(x, c, ctx, c_ctx, ada_w, ada_b, norm_g, mlstm_w_in, mlstm_gate_b, mlstm_head_g, mlstm_w_out,
           diff_w_in, diff_w_out, diff_q_g, diff_k_g, diff_lq1, diff_lk1, diff_lq2, diff_lk2, diff_subln_g,
           ffn_w_gu, ffn_w_down):
    assert x.shape == (BATCH, SEQ, D_MODEL) and ctx.shape == (BATCH, CTX_LEN, D_MODEL)
    xs = jnp.concatenate([x.reshape(N_LAT, D_MODEL), ctx.reshape(N_CTX, D_MODEL)], axis=0)
    cc = jnp.concatenate([c, c_ctx[None], jnp.zeros((8 - BATCH - 1, D_MODEL), F32)], axis=0)
    mods = _ada(cc, ada_w, ada_b).reshape(DEPTH, 8, 1, 6 * D_MODEL)
    cos_t, sin_t = _rope_tables()
    row = lambda v: v.reshape(1, -1)
    ffn_gu = ffn_w_gu.astype(BF16)
    ffn_down = ffn_w_down.astype(BF16)
    mlstm_in = mlstm_w_in.astype(BF16)
    mlstm_kt = jnp.swapaxes(mlstm_w_in[:, :, MLSTM_QK:2 * MLSTM_QK], 1, 2).astype(BF16)
    mlstm_out = mlstm_w_out.astype(BF16)
    diff_in = jnp.concatenate([_pair_major(diff_w_in[..., :2 * D_MODEL]), diff_w_in[..., 2 * D_MODEL:]],
                              axis=-1).astype(BF16)
    diff_out = diff_w_out.astype(BF16)

    for i in range(DEPTH):
        last = i == DEPTH - 1
        n_rows = N_LAT if last else NT
        j = i // N_MIXERS
        if i % N_MIXERS == 0:
            w_gate = jnp.pad(mlstm_w_in[j, :, MLSTM_MAIN:], ((0, 0), (0, LANES - N_GATES))).astype(BF16)
            gate_b = jnp.pad(mlstm_gate_b[j], (0, LANES - N_GATES)).reshape(1, LANES)
            act, k_t, gates = _inproj_mlstm(xs, row(norm_g[i, 0]), mods, i, mlstm_in, mlstm_kt, j, w_gate)
            hf = _mlstm_scan(act, k_t, gates, gate_b, None, None, reverse=False)
            a = _mlstm_scan(act, k_t, gates, gate_b, row(mlstm_head_g[j]), hf, reverse=True)
            w_out = mlstm_out
        else:
            lam_init = 0.8 - 0.6 * math.exp(-0.3 * i)
            q_gain = _pair_major(diff_q_g[j]) * (DIFF_DH ** -0.5 * math.log2(math.e))
            qkg = jnp.concatenate([jnp.tile(q_gain, 2 * DIFF_HEADS),
                                   jnp.tile(_pair_major(diff_k_g[j]), 2 * DIFF_HEADS)]).reshape(1, 2 * D_MODEL)
            qkv = _inproj_diff(xs, row(norm_g[i, 0]), mods, i, diff_in, j, qkg, cos_t, sin_t)
            lams = [row(diff_lq1[j]), row(diff_lk1[j]), row(diff_lq2[j]), row(diff_lk2[j])]
            a = _attention(qkv, lams, row(diff_subln_g[j]), lam_init, None)
            if not last:
                a = _attention(qkv, lams, row(diff_subln_g[j]), lam_init, a)
            w_out = diff_out
        xs = _outproj(a, w_out, j, xs, mods, i, n_rows // TM)
        xs = _ffn(xs, row(norm_g[i, 1]), mods, i, ffn_gu, ffn_down, n_rows // TMF)
    return xs.reshape(BATCH, SEQ, D_MODEL)
```

```python
import functools
import math

import jax
import jax.numpy as jnp
from jax import lax
from jax.experimental import pallas as pl
from jax.experimental.pallas import tpu as pltpu

F32 = jnp.float32
BF16 = jnp.bfloat16

D_MODEL = 2048
BATCH = 4
SEQ = 4096
DEPTH = 4
GRID_W = 64
CTX_LEN = 256
N_MIXERS = 2

MLSTM_HEADS = 8
MLSTM_DK = D_MODEL // (2 * MLSTM_HEADS)
MLSTM_DV = D_MODEL // MLSTM_HEADS
GATE_CAP = 15.0
MLSTM_QK = MLSTM_HEADS * MLSTM_DK
MLSTM_MAIN = 2 * MLSTM_QK + 2 * D_MODEL
N_GATES = 4 * MLSTM_HEADS

DIFF_HEADS = 8
DIFF_DH = D_MODEL // (2 * DIFF_HEADS)
ROPE_BASE = 10000.0

FFN_HIDDEN = -(-(8 * D_MODEL) // (3 * 256)) * 256
EPS = 1e-6

N_LAT = BATCH * SEQ
N_CTX = BATCH * CTX_LEN
NT = N_LAT + N_CTX

LANES = 128
TM = 1024
TN = 1024
TMF = 512
TH = 512
CHUNK = 256
TQ = 1024
TK = 1024
VMEM_LIMIT = 56 * 1024 * 1024

NT_DIMS = (((1,), (1,)), ((), ()))
TN_DIMS = (((0,), (0,)), ((), ()))


def _params(*sem):
    return pltpu.CompilerParams(dimension_semantics=sem, vmem_limit_bytes=VMEM_LIMIT)


def _mod_row(i, tm):
    return jnp.minimum(i // (SEQ // tm), BATCH)


def _mod_spec(layer, k, tm=TM):
    return pl.BlockSpec((None, None, 1, D_MODEL), lambda i, j: (layer, _mod_row(i, tm), 0, k))


def _weight_spec(layer, tn):
    return pl.BlockSpec((None, D_MODEL, tn), lambda i, j: (layer, 0, j))


def _lane_repeat(x, n, axis):
    assert axis == 1
    return jnp.concatenate([x] * n, axis=1) if n > 1 else x


def _norm_modulate(x, g, shift, scale):
    y = x * lax.rsqrt(jnp.mean(x * x, axis=-1, keepdims=True) + EPS) * g
    return y * (1.0 + scale) + shift


def _ada_kernel(c_ref, w_ref, b_ref, o_ref):
    c = c_ref[...]
    a = (c * jax.nn.sigmoid(c)).astype(BF16)
    o_ref[...] = jnp.dot(a, w_ref[...].astype(BF16), preferred_element_type=F32) + b_ref[...]


def _ada(cc, ada_w, ada_b):
    tn = 1024
    return pl.pallas_call(
        _ada_kernel,
        grid=(DEPTH, 6 * D_MODEL // tn),
        in_specs=[
            pl.BlockSpec((8, D_MODEL), lambda l, j: (0, 0)),
            pl.BlockSpec((None, D_MODEL, tn), lambda l, j: (l, 0, j)),
            pl.BlockSpec((None, 1, tn), lambda l, j: (l, 0, j)),
        ],
        out_specs=pl.BlockSpec((None, 8, tn), lambda l, j: (l, 0, j)),
        out_shape=jax.ShapeDtypeStruct((DEPTH, 8, 6 * D_MODEL), F32),
        compiler_params=_params("arbitrary", "arbitrary"),
        name="ada_mod",
    )(cc, ada_w, ada_b.reshape(DEPTH, 1, 6 * D_MODEL))


MLSTM_ACT = MLSTM_MAIN - MLSTM_QK
Q_TILE = MLSTM_ACT // TN - 1


def _inproj_mlstm_kernel(x_ref, g_ref, shift_ref, scale_ref, w_ref, wkt_ref, wg_ref,
                         out_ref, kt_ref, gate_ref, h_ref):
    j = pl.program_id(1)

    @pl.when(j == 0)
    def _():
        h = _norm_modulate(x_ref[...], g_ref[...], shift_ref[...], scale_ref[...]).astype(BF16)
        h_ref[...] = h
        gate_ref[...] = jnp.dot(h, wg_ref[...], preferred_element_type=F32)

    @pl.when(j == 1)
    def _():
        kt_ref[...] = lax.dot_general(wkt_ref[...], h_ref[...], NT_DIMS,
                                      preferred_element_type=F32).astype(BF16)

    @pl.when(j != 1)
    def _():
        acc = jnp.dot(h_ref[...], w_ref[...], preferred_element_type=F32)
        qscale = jnp.where(j == 0, MLSTM_DK ** -0.5, 1.0).astype(F32)
        out_ref[...] = (acc * qscale).astype(BF16)


def _inproj_mlstm(xs, g, mods, layer, w_all, wkt_all, w_layer, w_gate):
    assert TN == MLSTM_QK
    return pl.pallas_call(
        _inproj_mlstm_kernel,
        grid=(NT // TM, MLSTM_MAIN // TN),
        in_specs=[
            pl.BlockSpec((TM, D_MODEL), lambda i, j: (i, 0)),
            pl.BlockSpec((1, D_MODEL), lambda i, j: (0, 0)),
            _mod_spec(layer, 0),
            _mod_spec(layer, 1),
            pl.BlockSpec((None, D_MODEL, TN), lambda i, j: (w_layer, 0, jnp.where(j == 1, 0, j))),
            pl.BlockSpec((None, MLSTM_QK, D_MODEL), lambda i, j: (w_layer, 0, 0),
                         pipeline_mode=pl.Buffered(1)),
            pl.BlockSpec((D_MODEL, LANES), lambda i, j: (0, 0)),
        ],
        out_specs=[
            pl.BlockSpec((TM, TN), lambda i, j: (i, jnp.where(j <= 1, Q_TILE, j - 2))),
            pl.BlockSpec((MLSTM_QK, TM), lambda i, j: (0, i)),
            pl.BlockSpec((TM, LANES), lambda i, j: (i, 0)),
        ],
        out_shape=[
            jax.ShapeDtypeStruct((NT, MLSTM_ACT), BF16),
            jax.ShapeDtypeStruct((MLSTM_QK, NT), BF16),
            jax.ShapeDtypeStruct((NT, LANES), F32),
        ],
        scratch_shapes=[pltpu.VMEM((TM, D_MODEL), BF16)],
        compiler_params=_params("arbitrary", "arbitrary"),
        name="inproj_mlstm",
    )(xs, g, mods, mods, w_all, wkt_all, w_gate)


def _mlstm_scan_kernel(*refs, reverse, final):
    if final:
        (q_ref, kt_ref, v_ref, gate_ref, gb_ref, o_ref, hf_ref, hg_ref,
         out_ref, c_ref, n_ref, m_ref) = refs
    else:
        q_ref, kt_ref, v_ref, gate_ref, gb_ref, out_ref, c_ref, n_ref, m_ref = refs
    L = CHUNK

    @pl.when(pl.program_id(1) == 0)
    def _():
        c_ref[...] = jnp.zeros_like(c_ref)
        n_ref[...] = jnp.zeros_like(n_ref)
        m_ref[...] = jnp.zeros_like(m_ref)

    gg = GATE_CAP * jnp.tanh((gate_ref[...] + gb_ref[...]) / GATE_CAP)
    lsig = jax.nn.log_sigmoid(gg)
    t_idx = lax.broadcasted_iota(jnp.int32, (L, L), 0)
    s_idx = lax.broadcasted_iota(jnp.int32, (L, L), 1)
    mask = (s_idx >= t_idx) if reverse else (s_idx <= t_idx)
    bcum = jnp.dot(mask.astype(F32), lsig, precision=lax.Precision.HIGHEST,
                   preferred_element_type=F32)
    bal = pltpu.roll(bcum, LANES - MLSTM_HEADS, 1)
    r_t = (gg - bal).T
    base = 2 * MLSTM_HEADS if reverse else 0
    last = 0 if reverse else L - 1

    for h in range(MLSTM_HEADS):
        col = base + h
        b_rep = jnp.broadcast_to(bal[:, col:col + 1], (L, LANES))
        r_row = r_t[col:col + 1, :]
        m = m_ref[h]
        m11 = m[:, 0:1]
        b_last = b_rep[last:last + 1, 0:1]

        dmat = jnp.where(mask, _lane_repeat(b_rep, L // LANES, 1) + r_row, -jnp.inf)
        inter = b_rep + m
        m_t = jnp.maximum(inter, jnp.max(dmat, axis=-1, keepdims=True))
        w_intra = jnp.exp(dmat - _lane_repeat(m_t, L // LANES, 1))
        w_inter = jnp.exp(inter - m_t)

        qh = q_ref[:, h * MLSTM_DK:(h + 1) * MLSTM_DK]
        kth = kt_ref[h * MLSTM_DK:(h + 1) * MLSTM_DK, :]
        vh = v_ref[:, h * MLSTM_DV:(h + 1) * MLSTM_DV]
        s = jnp.dot(qh, kth, preferred_element_type=F32) * w_intra
        c_old = c_ref[h]
        num = (_lane_repeat(w_inter, MLSTM_DV // LANES, 1)
               * jnp.dot(qh, c_old.astype(BF16), preferred_element_type=F32)
               + jnp.dot(s.astype(BF16), vh, preferred_element_type=F32))
        n_old = n_ref[h]
        qn = jnp.dot(qh, n_old.astype(BF16), preferred_element_type=F32)
        den = w_inter * qn + jnp.sum(s, axis=-1, keepdims=True)
        inv = 1.0 / jnp.maximum(jnp.abs(den), jnp.exp(-m_t))
        hout = num * _lane_repeat(inv, MLSTM_DV // LANES, 1)

        a_row = b_last + r_row
        m_new = jnp.maximum(b_last + m11, jnp.max(a_row, axis=-1, keepdims=True))
        wk = jnp.exp(a_row - m_new)
        dec = jnp.exp(b_last + m11 - m_new)
        kw_t = kth.astype(F32) * wk
        c_ref[h] = dec * c_old + jnp.dot(kw_t.astype(BF16), vh, preferred_element_type=F32)
        n_ref[h] = dec * n_old + jnp.sum(kw_t, axis=-1, keepdims=True)
        m_ref[h] = jnp.broadcast_to(m_new, (1, LANES))

        sl = slice(h * MLSTM_DV, (h + 1) * MLSTM_DV)
        if final:
            hs = hf_ref[:, sl] + hout
            y = hs * lax.rsqrt(jnp.mean(hs * hs, axis=-1, keepdims=True) + EPS) * hg_ref[:, sl]
            out_ref[:, sl] = (y * jax.nn.sigmoid(o_ref[:, sl].astype(F32))).astype(BF16)
        else:
            out_ref[:, sl] = hout


def _mlstm_scan(act, k_t, gates, gate_b, head_g, hf, *, reverse):
    final = hf is not None
    n_lat_chunks = SEQ // CHUNK
    ctx_blk0 = N_LAT // CHUNK

    def row_blk(b, c):
        j = (n_lat_chunks - c) if reverse else (c - 1)
        return jnp.where(c == 0, ctx_blk0 + b, b * n_lat_chunks + j)

    in_specs = [
        pl.BlockSpec((CHUNK, MLSTM_QK), lambda b, c: (row_blk(b, c), Q_TILE)),
        pl.BlockSpec((MLSTM_QK, CHUNK), lambda b, c: (0, row_blk(b, c))),
        pl.BlockSpec((CHUNK, D_MODEL), lambda b, c: (row_blk(b, c), 0)),
        pl.BlockSpec((CHUNK, LANES), lambda b, c: (row_blk(b, c), 0)),
        pl.BlockSpec((1, LANES), lambda b, c: (0, 0)),
    ]
    args = [act, k_t, act, gates, gate_b]
    if final:
        in_specs += [
            pl.BlockSpec((CHUNK, D_MODEL), lambda b, c: (row_blk(b, c), 1)),
            pl.BlockSpec((CHUNK, D_MODEL), lambda b, c: (row_blk(b, c), 0)),
            pl.BlockSpec((1, D_MODEL), lambda b, c: (0, 0)),
        ]
        args += [act, hf, head_g]
    return pl.pallas_call(
        functools.partial(_mlstm_scan_kernel, reverse=reverse, final=final),
        grid=(BATCH, 1 + n_lat_chunks),
        in_specs=in_specs,
        out_specs=pl.BlockSpec((CHUNK, D_MODEL), lambda b, c: (row_blk(b, c), 0)),
        out_shape=jax.ShapeDtypeStruct((NT, D_MODEL), BF16 if final else F32),
        scratch_shapes=[
            pltpu.VMEM((MLSTM_HEADS, MLSTM_DK, MLSTM_DV), F32),
            pltpu.VMEM((MLSTM_HEADS, MLSTM_DK, LANES), F32),
            pltpu.VMEM((MLSTM_HEADS, 1, LANES), F32),
        ],
        compiler_params=_params("arbitrary", "arbitrary"),
        name="mlstm_scan_bwd" if reverse else "mlstm_scan_fwd",
    )(*args)


def _inproj_diff_kernel(x_ref, g_ref, shift_ref, scale_ref, w_ref, qkg_ref, cos_ref, sin_ref,
                        out_ref, h_ref):
    j = pl.program_id(1)
    n_qk_blocks = 2 * D_MODEL // TN

    @pl.when(j == 0)
    def _():
        h_ref[...] = _norm_modulate(x_ref[...], g_ref[...], shift_ref[...], scale_ref[...]).astype(BF16)

    acc = jnp.dot(h_ref[...], w_ref[...], preferred_element_type=F32)

    @pl.when(j < n_qk_blocks)
    def _():
        cos = cos_ref[...]
        sin = sin_ref[...]
        for grp in range(TN // LANES):
            sl = slice(grp * LANES, (grp + 1) * LANES)
            xg = acc[:, sl]
            y = xg * lax.rsqrt(jnp.mean(xg * xg, axis=-1, keepdims=True) + EPS) * qkg_ref[:, sl]
            out_ref[:, sl] = (y * cos + pltpu.roll(y, LANES // 2, 1) * sin).astype(BF16)

    @pl.when(j >= n_qk_blocks)
    def _():
        out_ref[...] = acc.astype(BF16)


def _inproj_diff(xs, g, mods, layer, w_all, w_layer, qkg, cos_t, sin_t):
    n_qk_blocks = 2 * D_MODEL // TN
    lat_tiles = SEQ // TM

    def rope_blk(i, j):
        return (jnp.where(i < N_LAT // TM, i % lat_tiles, lat_tiles), 0)

    return pl.pallas_call(
        _inproj_diff_kernel,
        grid=(NT // TM, 3 * D_MODEL // TN),
        in_specs=[
            pl.BlockSpec((TM, D_MODEL), lambda i, j: (i, 0)),
            pl.BlockSpec((1, D_MODEL), lambda i, j: (0, 0)),
            _mod_spec(layer, 0),
            _mod_spec(layer, 1),
            _weight_spec(w_layer, TN),
            pl.BlockSpec((1, TN), lambda i, j: (0, jnp.minimum(j, n_qk_blocks - 1))),
            pl.BlockSpec((TM, LANES), rope_blk),
            pl.BlockSpec((TM, LANES), rope_blk),
        ],
        out_specs=pl.BlockSpec((TM, TN), lambda i, j: (i, j)),
        out_shape=jax.ShapeDtypeStruct((NT, 3 * D_MODEL), BF16),
        scratch_shapes=[pltpu.VMEM((TM, D_MODEL), BF16)],
        compiler_params=_params("arbitrary", "arbitrary"),
        name="inproj_diff",
    )(xs, g, mods, mods, w_all, qkg, cos_t, sin_t)


def _attn_kernel(*refs, n_lat_chunks, lam_init):
    if n_lat_chunks:
        (q_ref, kl_ref, vl_ref, kc_ref, vc_ref, lq1_ref, lk1_ref, lq2_ref, lk2_ref, sg_ref,
         out_ref, acc_ref, m_ref, l_ref) = refs
    else:
        (q_ref, kc_ref, vc_ref, lq1_ref, lk1_ref, lq2_ref, lk2_ref, sg_ref, _,
         out_ref, acc_ref, m_ref, l_ref) = refs
    tq = q_ref.shape[0]
    dv = 2 * DIFF_DH
    q = q_ref[...]

    def scores(kblk):
        return jnp.concatenate(
            [lax.dot_general(q[:, t * DIFF_DH:(t + 1) * DIFF_DH], kblk[:, t * DIFF_DH:(t + 1) * DIFF_DH],
                             NT_DIMS, preferred_element_type=F32) for t in range(2)], axis=0)

    s = scores(kc_ref[...])
    m0 = jnp.max(s, axis=-1, keepdims=True)
    p = jnp.exp2(s - m0)
    m_ref[...] = jnp.broadcast_to(m0, m_ref.shape)
    l_ref[...] = jnp.broadcast_to(jnp.sum(p, axis=-1, keepdims=True), l_ref.shape)
    acc_ref[...] = jnp.dot(p.astype(BF16), vc_ref[...], preferred_element_type=F32)

    if n_lat_chunks:
        def body(c, carry):
            off = pl.multiple_of(c * TK, TK)
            s = scores(kl_ref[pl.ds(off, TK), :])
            m_old = m_ref[...]
            m_new = jnp.maximum(m_old, jnp.max(s, axis=-1, keepdims=True))
            p = jnp.exp2(s - _lane_repeat(m_new, TK // LANES, 1))
            alpha = jnp.exp2(m_old - m_new)
            l_ref[...] = alpha * l_ref[...] + jnp.sum(p, axis=-1, keepdims=True)
            acc_ref[...] = (_lane_repeat(alpha, dv // LANES, 1) * acc_ref[...]
                            + jnp.dot(p.astype(BF16), vl_ref[pl.ds(off, TK), :],
                                      preferred_element_type=F32))
            m_ref[...] = m_new
            return carry
        lax.fori_loop(0, n_lat_chunks, body, 0)

    lam = (jnp.exp(jnp.sum(lq1_ref[...] * lk1_ref[...], axis=-1, keepdims=True))
           - jnp.exp(jnp.sum(lq2_ref[...] * lk2_ref[...], axis=-1, keepdims=True)) + lam_init)
    o_all = acc_ref[...] / _lane_repeat(l_ref[...], dv // LANES, 1)
    o = o_all[:tq] - lam * o_all[tq:]
    y = o * lax.rsqrt(jnp.mean(o * o, axis=-1, keepdims=True) + EPS) * sg_ref[...]
    out_ref[...] = (y * (1.0 - lam_init)).astype(BF16)


def _attention(qkv, lams, subln_g, lam_init, prev_out):
    dv = 2 * DIFF_DH
    kcol = D_MODEL // dv
    vcol = 2 * D_MODEL // dv
    ctx_blk0 = N_LAT // CTX_LEN
    small = [pl.BlockSpec((1, DIFF_DH), lambda b, h, i: (0, 0))] * 4
    small.append(pl.BlockSpec((1, dv), lambda b, h, i: (0, 0)))
    ctx_specs = [
        pl.BlockSpec((CTX_LEN, dv), lambda b, h, i: (ctx_blk0 + b, kcol + h)),
        pl.BlockSpec((CTX_LEN, dv), lambda b, h, i: (ctx_blk0 + b, vcol + h)),
    ]
    if prev_out is None:
        tq, nq = TQ, SEQ // TQ
        q_map = lambda b, h, i: (b * nq + i, h)
        in_specs = [
            pl.BlockSpec((tq, dv), q_map),
            pl.BlockSpec((SEQ, dv), lambda b, h, i: (b, kcol + h)),
            pl.BlockSpec((SEQ, dv), lambda b, h, i: (b, vcol + h)),
        ] + ctx_specs + small
        args = [qkv, qkv, qkv, qkv, qkv] + list(lams) + [subln_g]
        aliases = {}
        n_lat_chunks = SEQ // TK
    else:
        tq, nq = CTX_LEN, 1
        q_map = lambda b, h, i: (ctx_blk0 + b, h)
        in_specs = [pl.BlockSpec((tq, dv), q_map)] + ctx_specs + small
        in_specs.append(pl.BlockSpec(memory_space=pl.ANY))
        args = [qkv, qkv, qkv] + list(lams) + [subln_g, prev_out]
        aliases = {len(args) - 1: 0}
        n_lat_chunks = 0
    return pl.pallas_call(
        functools.partial(_attn_kernel, n_lat_chunks=n_lat_chunks, lam_init=lam_init),
        grid=(BATCH, DIFF_HEADS, nq),
        in_specs=in_specs,
        out_specs=pl.BlockSpec((tq, dv), q_map),
        out_shape=jax.ShapeDtypeStruct((NT, D_MODEL), BF16),
        scratch_shapes=[
            pltpu.VMEM((2 * tq, dv), F32),
            pltpu.VMEM((2 * tq, LANES), F32),
            pltpu.VMEM((2 * tq, LANES), F32),
        ],
        input_output_aliases=aliases,
        compiler_params=_params("arbitrary", "arbitrary", "arbitrary"),
        name="diff_attn_ctx" if n_lat_chunks == 0 else "diff_attn",
    )(*args)


def _ffn_kernel(a_ref, wo_ref, x_ref, g_ref, gate1_ref, shift_ref, scale_ref, gate2_ref,
                wg_ref, wu_ref, wd_ref, out_ref, h_ref):
    k = pl.program_id(1)

    @pl.when(k == 0)
    def _():
        x1 = x_ref[...] + gate1_ref[...] * jnp.dot(a_ref[...], wo_ref[...], preferred_element_type=F32)
        out_ref[...] = x1
        h_ref[...] = _norm_modulate(x1, g_ref[...], shift_ref[...], scale_ref[...]).astype(BF16)

    h = h_ref[...]
    gt = jnp.dot(h, wg_ref[...], preferred_element_type=F32)
    up = jnp.dot(h, wu_ref[...], preferred_element_type=F32)
    act = (gt * jax.nn.sigmoid(gt) * up).astype(BF16)
    out_ref[...] += gate2_ref[...] * jnp.dot(act, wd_ref[...], preferred_element_type=F32)


def _ffn(a, w_out, w_out_layer, xs, g, mods, layer, w_gu, w_down, n_tiles):
    nh = FFN_HIDDEN // TH
    return pl.pallas_call(
        _ffn_kernel,
        grid=(n_tiles, nh),
        in_specs=[
            pl.BlockSpec((TMF, D_MODEL), lambda i, k: (i, 0)),
            pl.BlockSpec((None, D_MODEL, D_MODEL), lambda i, k: (w_out_layer, 0, 0),
                         pipeline_mode=pl.Buffered(1)),
            pl.BlockSpec((TMF, D_MODEL), lambda i, k: (i, 0)),
            pl.BlockSpec((1, D_MODEL), lambda i, k: (0, 0)),
            _mod_spec(layer, 2, TMF),
            _mod_spec(layer, 3, TMF),
            _mod_spec(layer, 4, TMF),
            _mod_spec(layer, 5, TMF),
            pl.BlockSpec((None, D_MODEL, TH), lambda i, k: (layer, 0, k)),
            pl.BlockSpec((None, D_MODEL, TH), lambda i, k: (layer, 0, nh + k)),
            pl.BlockSpec((None, TH, D_MODEL), lambda i, k: (layer, k, 0)),
        ],
        out_specs=pl.BlockSpec((TMF, D_MODEL), lambda i, k: (i, 0)),
        out_shape=jax.ShapeDtypeStruct((n_tiles * TMF, D_MODEL), F32),
        scratch_shapes=[pltpu.VMEM((TMF, D_MODEL), BF16)],
        compiler_params=_params("arbitrary", "arbitrary"),
        name="ffn",
    )(a, w_out, xs, g, mods, mods, mods, mods, w_gu, w_gu, w_down)


def _pair_major(a):
    n_freq = DIFF_DH // 4
    lead = a.shape[:-1]
    a = a.reshape(*lead, a.shape[-1] // DIFF_DH, 2, 2, n_freq)
    return jnp.swapaxes(a, -3, -2).reshape(*lead, -1)


def _rope_tables():
    n_freq = DIFF_DH // 4
    pos = jnp.arange(SEQ)
    freqs = ROPE_BASE ** (-jnp.arange(n_freq, dtype=F32) / n_freq)
    ang = jnp.stack([pos // GRID_W, pos % GRID_W], axis=-1).astype(F32)[:, :, None] * freqs
    cos, sin = jnp.cos(ang), jnp.sin(ang)
    cos_t = jnp.concatenate([cos[:, 0], cos[:, 1], cos[:, 0], cos[:, 1]], axis=-1)
    sin_t = jnp.concatenate([-sin[:, 0], -sin[:, 1], sin[:, 0], sin[:, 1]], axis=-1)
    cos_t = jnp.concatenate([cos_t, jnp.ones((TM, LANES), F32)], axis=0)
    sin_t = jnp.concatenate([sin_t, jnp.zeros((TM, LANES), F32)], axis=0)
    return cos_t, sin_t


def kernel(x, c, ctx, c_ctx, ada_w, ada_b, norm_g, mlstm_w_in, mlstm_gate_b, mlstm_head_g, mlstm_w_out,
           diff_w_in, diff_w_out, diff_q_g, diff_k_g, diff_lq1, diff_lk1, diff_lq2, diff_lk2, diff_subln_g,
           ffn_w_gu, ffn_w_down):
    assert x.shape == (BATCH, SEQ, D_MODEL) and ctx.shape == (BATCH, CTX_LEN, D_MODEL)
    xs = jnp.concatenate([x.reshape(N_LAT, D_MODEL), ctx.reshape(N_CTX, D_MODEL)], axis=0)
    cc = jnp.concatenate([c, c_ctx[None], jnp.zeros((8 - BATCH - 1, D_MODEL), F32)], axis=0)
    mods = _ada(cc, ada_w, ada_b).reshape(DEPTH, 8, 1, 6 * D_MODEL)
    cos_t, sin_t = _rope_tables()
    row = lambda v: v.reshape(1, -1)
    ffn_gu = ffn_w_gu.astype(BF16)
    ffn_down = ffn_w_down.astype(BF16)
    mlstm_in = mlstm_w_in.astype(BF16)
    mlstm_kt = jnp.swapaxes(mlstm_w_in[:, :, MLSTM_QK:2 * MLSTM_QK], 1, 2).astype(BF16)
    mlstm_out = mlstm_w_out.astype(BF16)
    diff_in = jnp.concatenate([_pair_major(diff_w_in[..., :2 * D_MODEL]), diff_w_in[..., 2 * D_MODEL:]],
                              axis=-1).astype(BF16)
    diff_out = diff_w_out.astype(BF16)

    for i in range(DEPTH):
        last = i == DEPTH - 1
        n_rows = N_LAT if last else NT
        j = i // N_MIXERS
        if i % N_MIXERS == 0:
            w_gate = jnp.pad(mlstm_w_in[j, :, MLSTM_MAIN:], ((0, 0), (0, LANES - N_GATES))).astype(BF16)
            gate_b = jnp.pad(mlstm_gate_b[j], (0, LANES - N_GATES)).reshape(1, LANES)
            act, k_t, gates = _inproj_mlstm(xs, row(norm_g[i, 0]), mods, i, mlstm_in, mlstm_kt, j, w_gate)
            hf = _mlstm_scan(act, k_t, gates, gate_b, None, None, reverse=False)
            a = _mlstm_scan(act, k_t, gates, gate_b, row(mlstm_head_g[j]), hf, reverse=True)
            w_out = mlstm_out
        else:
            lam_init = 0.8 - 0.6 * math.exp(-0.3 * i)
            q_gain = _pair_major(diff_q_g[j]) * (DIFF_DH ** -0.5 * math.log2(math.e))
            qkg = jnp.concatenate([jnp.tile(q_gain, 2 * DIFF_HEADS),
                                   jnp.tile(_pair_major(diff_k_g[j]), 2 * DIFF_HEADS)]).reshape(1, 2 * D_MODEL)
            qkv = _inproj_diff(xs, row(norm_g[i, 0]), mods, i, diff_in, j, qkg, cos_t, sin_t)
            lams = [row(diff_lq1[j]), row(diff_lk1[j]), row(diff_lq2[j]), row(diff_lk2[j])]
            a = _attention(qkv, lams, row(diff_subln_g[j]), lam_init, None)
            if not last:
                a = _attention(qkv, lams, row(diff_subln_g[j]), lam_init, a)
            w_out = diff_out
        xs = _ffn(a, w_out, j, xs, row(norm_g[i, 1]), mods, i, ffn_gu, ffn_down, n_rows // TMF)
    return xs.reshape(BATCH, SEQ, D_MODEL)
```

```python
import functools
import math

import jax
import jax.numpy as jnp
from jax import lax
from jax.experimental import pallas as pl
from jax.experimental.pallas import tpu as pltpu

F32 = jnp.float32
BF16 = jnp.bfloat16

D_MODEL = 2048
BATCH = 4
SEQ = 4096
DEPTH = 4
GRID_W = 64
CTX_LEN = 256
N_MIXERS = 2

MLSTM_HEADS = 8
MLSTM_DK = D_MODEL // (2 * MLSTM_HEADS)
MLSTM_DV = D_MODEL // MLSTM_HEADS
GATE_CAP = 15.0
MLSTM_QK = MLSTM_HEADS * MLSTM_DK
MLSTM_MAIN = 2 * MLSTM_QK + 2 * D_MODEL
N_GATES = 4 * MLSTM_HEADS

DIFF_HEADS = 8
DIFF_DH = D_MODEL // (2 * DIFF_HEADS)
ROPE_BASE = 10000.0

FFN_HIDDEN = -(-(8 * D_MODEL) // (3 * 256)) * 256
EPS = 1e-6

N_LAT = BATCH * SEQ
N_CTX = BATCH * CTX_LEN
NT = N_LAT + N_CTX

LANES = 128
TM = 1024
TN = 1024
TMF = 512
TH = 512
CHUNK = 256
TQ = 1024
TK = 1024
VMEM_LIMIT = 56 * 1024 * 1024

NT_DIMS = (((1,), (1,)), ((), ()))
TN_DIMS = (((0,), (0,)), ((), ()))


def _params(*sem):
    return pltpu.CompilerParams(dimension_semantics=sem, vmem_limit_bytes=VMEM_LIMIT)


def _mod_row(i, tm):
    return jnp.minimum(i // (SEQ // tm), BATCH)


def _mod_spec(layer, k, tm=TM):
    return pl.BlockSpec((None, None, 1, D_MODEL), lambda i, j: (layer, _mod_row(i, tm), 0, k))


def _weight_spec(layer, tn):
    return pl.BlockSpec((None, D_MODEL, tn), lambda i, j: (layer, 0, j))


def _lane_repeat(x, n, axis):
    assert axis == 1
    return jnp.concatenate([x] * n, axis=1) if n > 1 else x


def _norm_modulate(x, g, shift, scale):
    y = x * lax.rsqrt(jnp.mean(x * x, axis=-1, keepdims=True) + EPS) * g
    return y * (1.0 + scale) + shift


def _ada_kernel(c_ref, w_ref, b_ref, o_ref):
    c = c_ref[...]
    a = (c * jax.nn.sigmoid(c)).astype(BF16)
    o_ref[...] = jnp.dot(a, w_ref[...].astype(BF16), preferred_element_type=F32) + b_ref[...]


def _ada(cc, ada_w, ada_b):
    tn = 1024
    return pl.pallas_call(
        _ada_kernel,
        grid=(DEPTH, 6 * D_MODEL // tn),
        in_specs=[
            pl.BlockSpec((8, D_MODEL), lambda l, j: (0, 0)),
            pl.BlockSpec((None, D_MODEL, tn), lambda l, j: (l, 0, j)),
            pl.BlockSpec((None, 1, tn), lambda l, j: (l, 0, j)),
        ],
        out_specs=pl.BlockSpec((None, 8, tn), lambda l, j: (l, 0, j)),
        out_shape=jax.ShapeDtypeStruct((DEPTH, 8, 6 * D_MODEL), F32),
        compiler_params=_params("arbitrary", "arbitrary"),
        name="ada_mod",
    )(cc, ada_w, ada_b.reshape(DEPTH, 1, 6 * D_MODEL))


MLSTM_ACT = MLSTM_MAIN - MLSTM_QK
Q_TILE = MLSTM_ACT // TN - 1


def _inproj_mlstm_kernel(x_ref, g_ref, shift_ref, scale_ref, w_ref, wkt_ref, wg_ref,
                         out_ref, kt_ref, gate_ref, h_ref):
    j = pl.program_id(1)

    @pl.when(j == 0)
    def _():
        h = _norm_modulate(x_ref[...], g_ref[...], shift_ref[...], scale_ref[...]).astype(BF16)
        h_ref[...] = h
        gate_ref[...] = jnp.dot(h, wg_ref[...], preferred_element_type=F32)

    @pl.when(j == 1)
    def _():
        kt_ref[...] = lax.dot_general(wkt_ref[...], h_ref[...], NT_DIMS,
                                      preferred_element_type=F32).astype(BF16)

    @pl.when(j != 1)
    def _():
        acc = jnp.dot(h_ref[...], w_ref[...], preferred_element_type=F32)
        qscale = jnp.where(j == 0, MLSTM_DK ** -0.5, 1.0).astype(F32)
        out_ref[...] = (acc * qscale).astype(BF16)


def _inproj_mlstm(xs, g, mods, layer, w_all, wkt_all, w_layer, w_gate):
    assert TN == MLSTM_QK
    return pl.pallas_call(
        _inproj_mlstm_kernel,
        grid=(NT // TM, MLSTM_MAIN // TN),
        in_specs=[
            pl.BlockSpec((TM, D_MODEL), lambda i, j: (i, 0)),
            pl.BlockSpec((1, D_MODEL), lambda i, j: (0, 0)),
            _mod_spec(layer, 0),
            _mod_spec(layer, 1),
            pl.BlockSpec((None, D_MODEL, TN), lambda i, j: (w_layer, 0, jnp.where(j == 1, 0, j))),
            pl.BlockSpec((None, MLSTM_QK, D_MODEL), lambda i, j: (w_layer, 0, 0),
                         pipeline_mode=pl.Buffered(1)),
            pl.BlockSpec((D_MODEL, LANES), lambda i, j: (0, 0)),
        ],
        out_specs=[
            pl.BlockSpec((TM, TN), lambda i, j: (i, jnp.where(j <= 1, Q_TILE, j - 2))),
            pl.BlockSpec((MLSTM_QK, TM), lambda i, j: (0, i)),
            pl.BlockSpec((TM, LANES), lambda i, j: (i, 0)),
        ],
        out_shape=[
            jax.ShapeDtypeStruct((NT, MLSTM_ACT), BF16),
            jax.ShapeDtypeStruct((MLSTM_QK, NT), BF16),
            jax.ShapeDtypeStruct((NT, LANES), F32),
        ],
        scratch_shapes=[pltpu.VMEM((TM, D_MODEL), BF16)],
        compiler_params=_params("arbitrary", "arbitrary"),
        name="inproj_mlstm",
    )(xs, g, mods, mods, w_all, wkt_all, w_gate)


def _mlstm_scan_kernel(*refs, reverse, final):
    if final:
        (q_ref, kt_ref, v_ref, gate_ref, gb_ref, o_ref, hf_ref, hg_ref,
         out_ref, c_ref, n_ref, m_ref) = refs
    else:
        q_ref, kt_ref, v_ref, gate_ref, gb_ref, out_ref, c_ref, n_ref, m_ref = refs
    L = CHUNK

    @pl.when(pl.program_id(1) == 0)
    def _():
        c_ref[...] = jnp.zeros_like(c_ref)
        n_ref[...] = jnp.zeros_like(n_ref)
        m_ref[...] = jnp.zeros_like(m_ref)

    gg = GATE_CAP * jnp.tanh((gate_ref[...] + gb_ref[...]) / GATE_CAP)
    lsig = jax.nn.log_sigmoid(gg)
    t_idx = lax.broadcasted_iota(jnp.int32, (L, L), 0)
    s_idx = lax.broadcasted_iota(jnp.int32, (L, L), 1)
    mask = (s_idx >= t_idx) if reverse else (s_idx <= t_idx)
    bcum = jnp.dot(mask.astype(F32), lsig, precision=lax.Precision.HIGHEST,
                   preferred_element_type=F32)
    bal = pltpu.roll(bcum, LANES - MLSTM_HEADS, 1)
    r_t = (gg - bal).T
    base = 2 * MLSTM_HEADS if reverse else 0
    last = 0 if reverse else L - 1

    for h in range(MLSTM_HEADS):
        col = base + h
        b_rep = jnp.broadcast_to(bal[:, col:col + 1], (L, LANES))
        r_row = r_t[col:col + 1, :]
        m = m_ref[h]
        m11 = m[:, 0:1]
        b_last = b_rep[last:last + 1, 0:1]

        dmat = jnp.where(mask, _lane_repeat(b_rep, L // LANES, 1) + r_row, -jnp.inf)
        inter = b_rep + m
        m_t = jnp.maximum(inter, jnp.max(dmat, axis=-1, keepdims=True))
        w_intra = jnp.exp(dmat - _lane_repeat(m_t, L // LANES, 1))
        w_inter = jnp.exp(inter - m_t)

        qh = q_ref[:, h * MLSTM_DK:(h + 1) * MLSTM_DK]
        kth = kt_ref[h * MLSTM_DK:(h + 1) * MLSTM_DK, :]
        vh = v_ref[:, h * MLSTM_DV:(h + 1) * MLSTM_DV]
        s = jnp.dot(qh, kth, preferred_element_type=F32) * w_intra
        c_old = c_ref[h]
        num = (_lane_repeat(w_inter, MLSTM_DV // LANES, 1)
               * jnp.dot(qh, c_old.astype(BF16), preferred_element_type=F32)
               + jnp.dot(s.astype(BF16), vh, preferred_element_type=F32))
        n_old = n_ref[h]
        qn = jnp.dot(qh, n_old.astype(BF16), preferred_element_type=F32)
        den = w_inter * qn + jnp.sum(s, axis=-1, keepdims=True)
        inv = 1.0 / jnp.maximum(jnp.abs(den), jnp.exp(-m_t))
        hout = num * _lane_repeat(inv, MLSTM_DV // LANES, 1)

        a_row = b_last + r_row
        m_new = jnp.maximum(b_last + m11, jnp.max(a_row, axis=-1, keepdims=True))
        wk = jnp.exp(a_row - m_new)
        dec = jnp.exp(b_last + m11 - m_new)
        kw_t = kth.astype(F32) * wk
        c_ref[h] = dec * c_old + jnp.dot(kw_t.astype(BF16), vh, preferred_element_type=F32)
        n_ref[h] = dec * n_old + jnp.sum(kw_t, axis=-1, keepdims=True)
        m_ref[h] = jnp.broadcast_to(m_new, (1, LANES))

        sl = slice(h * MLSTM_DV, (h + 1) * MLSTM_DV)
        if final:
            hs = hf_ref[:, sl] + hout
            y = hs * lax.rsqrt(jnp.mean(hs * hs, axis=-1, keepdims=True) + EPS) * hg_ref[:, sl]
            out_ref[:, sl] = (y * jax.nn.sigmoid(o_ref[:, sl].astype(F32))).astype(BF16)
        else:
            out_ref[:, sl] = hout


def _mlstm_scan(act, k_t, gates, gate_b, head_g, hf, *, reverse):
    final = hf is not None
    n_lat_chunks = SEQ // CHUNK
    ctx_blk0 = N_LAT // CHUNK

    def row_blk(b, c):
        j = (n_lat_chunks - c) if reverse else (c - 1)
        return jnp.where(c == 0, ctx_blk0 + b, b * n_lat_chunks + j)

    in_specs = [
        pl.BlockSpec((CHUNK, MLSTM_QK), lambda b, c: (row_blk(b, c), Q_TILE)),
        pl.BlockSpec((MLSTM_QK, CHUNK), lambda b, c: (0, row_blk(b, c))),
        pl.BlockSpec((CHUNK, D_MODEL), lambda b, c: (row_blk(b, c), 0)),
        pl.BlockSpec((CHUNK, LANES), lambda b, c: (row_blk(b, c), 0)),
        pl.BlockSpec((1, LANES), lambda b, c: (0, 0)),
    ]
    args = [act, k_t, act, gates, gate_b]
    if final:
        in_specs += [
            pl.BlockSpec((CHUNK, D_MODEL), lambda b, c: (row_blk(b, c), 1)),
            pl.BlockSpec((CHUNK, D_MODEL), lambda b, c: (row_blk(b, c), 0)),
            pl.BlockSpec((1, D_MODEL), lambda b, c: (0, 0)),
        ]
        args += [act, hf, head_g]
    return pl.pallas_call(
        functools.partial(_mlstm_scan_kernel, reverse=reverse, final=final),
        grid=(BATCH, 1 + n_lat_chunks),
        in_specs=in_specs,
        out_specs=pl.BlockSpec((CHUNK, D_MODEL), lambda b, c: (row_blk(b, c), 0)),
        out_shape=jax.ShapeDtypeStruct((NT, D_MODEL), BF16 if final else F32),
        scratch_shapes=[
            pltpu.VMEM((MLSTM_HEADS, MLSTM_DK, MLSTM_DV), F32),
            pltpu.VMEM((MLSTM_HEADS, MLSTM_DK, LANES), F32),
            pltpu.VMEM((MLSTM_HEADS, 1, LANES), F32),
        ],
        compiler_params=_params("arbitrary", "arbitrary"),
        name="mlstm_scan_bwd" if reverse else "mlstm_scan_fwd",
    )(*args)


def _inproj_diff_kernel(x_ref, g_ref, shift_ref, scale_ref, w_ref, qkg_ref, cos_ref, sin_ref,
                        out_ref, h_ref, acc0_ref, acc1_ref):
    j = pl.program_id(1)
    n_qk_tiles = 2 * D_MODEL // TN
    n_tiles = 3 * D_MODEL // TN

    def matmul_into(acc_ref):
        acc_ref[...] = jnp.dot(h_ref[...], w_ref[...], preferred_element_type=F32)

    def norm_rope_from(acc_ref):
        cos = cos_ref[...]
        sin = sin_ref[...]
        for grp in range(TN // LANES):
            sl = slice(grp * LANES, (grp + 1) * LANES)
            xg = acc_ref[:, sl]
            y = xg * lax.rsqrt(jnp.mean(xg * xg, axis=-1, keepdims=True) + EPS) * qkg_ref[:, sl]
            out_ref[:, sl] = (y * cos + pltpu.roll(y, LANES // 2, 1) * sin).astype(BF16)

    def cast_from(acc_ref):
        out_ref[...] = acc_ref[...].astype(BF16)

    @pl.when(j == 0)
    def _():
        h_ref[...] = _norm_modulate(x_ref[...], g_ref[...], shift_ref[...], scale_ref[...]).astype(BF16)
        matmul_into(acc0_ref)

    is_qk_finish = jnp.logical_and(j >= 1, j <= n_qk_tiles)

    @pl.when(jnp.logical_and(is_qk_finish, j % 2 == 1))
    def _():
        matmul_into(acc1_ref)
        norm_rope_from(acc0_ref)

    @pl.when(jnp.logical_and(is_qk_finish, j % 2 == 0))
    def _():
        matmul_into(acc0_ref)
        norm_rope_from(acc1_ref)

    assert n_qk_tiles % 2 == 0 and n_tiles == n_qk_tiles + 2

    @pl.when(j == n_tiles - 1)
    def _():
        matmul_into(acc1_ref)
        cast_from(acc0_ref)

    @pl.when(j == n_tiles)
    def _():
        cast_from(acc1_ref)


def _inproj_diff(xs, g, mods, layer, w_all, w_layer, qkg, cos_t, sin_t):
    n_qk_tiles = 2 * D_MODEL // TN
    n_tiles = 3 * D_MODEL // TN
    lat_tiles = SEQ // TM

    def rope_blk(i, j):
        return (jnp.where(i < N_LAT // TM, i % lat_tiles, lat_tiles), 0)

    return pl.pallas_call(
        _inproj_diff_kernel,
        grid=(NT // TM, n_tiles + 1),
        in_specs=[
            pl.BlockSpec((TM, D_MODEL), lambda i, j: (i, 0)),
            pl.BlockSpec((1, D_MODEL), lambda i, j: (0, 0)),
            _mod_spec(layer, 0),
            _mod_spec(layer, 1),
            pl.BlockSpec((None, D_MODEL, TN), lambda i, j: (w_layer, 0, jnp.minimum(j, n_tiles - 1))),
            pl.BlockSpec((1, TN), lambda i, j: (0, jnp.clip(j - 1, 0, n_qk_tiles - 1))),
            pl.BlockSpec((TM, LANES), rope_blk),
            pl.BlockSpec((TM, LANES), rope_blk),
        ],
        out_specs=pl.BlockSpec((TM, TN), lambda i, j: (i, jnp.maximum(j - 1, 0))),
        out_shape=jax.ShapeDtypeStruct((NT, 3 * D_MODEL), BF16),
        scratch_shapes=[
            pltpu.VMEM((TM, D_MODEL), BF16),
            pltpu.VMEM((TM, TN), F32),
            pltpu.VMEM((TM, TN), F32),
        ],
        compiler_params=_params("arbitrary", "arbitrary"),
        name="inproj_diff",
    )(xs, g, mods, mods, w_all, qkg, cos_t, sin_t)


def _attn_kernel(*refs, n_lat_chunks, lam_init):
    if n_lat_chunks:
        (q_ref, kl_ref, vl_ref, kc_ref, vc_ref, lq1_ref, lk1_ref, lq2_ref, lk2_ref, sg_ref,
         out_ref, acc_ref, m_ref, l_ref) = refs
    else:
        (q_ref, kc_ref, vc_ref, lq1_ref, lk1_ref, lq2_ref, lk2_ref, sg_ref, _,
         out_ref, acc_ref, m_ref, l_ref) = refs
    tq = q_ref.shape[0]
    dv = 2 * DIFF_DH
    q = q_ref[...]

    def scores(kblk):
        return jnp.concatenate(
            [lax.dot_general(q[:, t * DIFF_DH:(t + 1) * DIFF_DH], kblk[:, t * DIFF_DH:(t + 1) * DIFF_DH],
                             NT_DIMS, preferred_element_type=F32) for t in range(2)], axis=0)

    s = scores(kc_ref[...])
    m0 = jnp.max(s, axis=-1, keepdims=True)
    p = jnp.exp2(s - m0)
    m_ref[...] = jnp.broadcast_to(m0, m_ref.shape)
    l_ref[...] = jnp.broadcast_to(jnp.sum(p, axis=-1, keepdims=True), l_ref.shape)
    acc_ref[...] = jnp.dot(p.astype(BF16), vc_ref[...], preferred_element_type=F32)

    if n_lat_chunks:
        def body(c, carry):
            off = pl.multiple_of(c * TK, TK)
            s = scores(kl_ref[pl.ds(off, TK), :])
            m_old = m_ref[...]
            m_new = jnp.maximum(m_old, jnp.max(s, axis=-1, keepdims=True))
            p = jnp.exp2(s - _lane_repeat(m_new, TK // LANES, 1))
            alpha = jnp.exp2(m_old - m_new)
            l_ref[...] = alpha * l_ref[...] + jnp.sum(p, axis=-1, keepdims=True)
            acc_ref[...] = (_lane_repeat(alpha, dv // LANES, 1) * acc_ref[...]
                            + jnp.dot(p.astype(BF16), vl_ref[pl.ds(off, TK), :],
                                      preferred_element_type=F32))
            m_ref[...] = m_new
            return carry
        lax.fori_loop(0, n_lat_chunks, body, 0)

    lam = (jnp.exp(jnp.sum(lq1_ref[...] * lk1_ref[...], axis=-1, keepdims=True))
           - jnp.exp(jnp.sum(lq2_ref[...] * lk2_ref[...], axis=-1, keepdims=True)) + lam_init)
    o_all = acc_ref[...] / _lane_repeat(l_ref[...], dv // LANES, 1)
    o = o_all[:tq] - lam * o_all[tq:]
    y = o * lax.rsqrt(jnp.mean(o * o, axis=-1, keepdims=True) + EPS) * sg_ref[...]
    out_ref[...] = (y * (1.0 - lam_init)).astype(BF16)


def _attention(qkv, lams, subln_g, lam_init, prev_out):
    dv = 2 * DIFF_DH
    kcol = D_MODEL // dv
    vcol = 2 * D_MODEL // dv
    ctx_blk0 = N_LAT // CTX_LEN
    small = [pl.BlockSpec((1, DIFF_DH), lambda b, h, i: (0, 0))] * 4
    small.append(pl.BlockSpec((1, dv), lambda b, h, i: (0, 0)))
    ctx_specs = [
        pl.BlockSpec((CTX_LEN, dv), lambda b, h, i: (ctx_blk0 + b, kcol + h)),
        pl.BlockSpec((CTX_LEN, dv), lambda b, h, i: (ctx_blk0 + b, vcol + h)),
    ]
    if prev_out is None:
        tq, nq = TQ, SEQ // TQ
        q_map = lambda b, h, i: (b * nq + i, h)
        in_specs = [
            pl.BlockSpec((tq, dv), q_map),
            pl.BlockSpec((SEQ, dv), lambda b, h, i: (b, kcol + h)),
            pl.BlockSpec((SEQ, dv), lambda b, h, i: (b, vcol + h)),
        ] + ctx_specs + small
        args = [qkv, qkv, qkv, qkv, qkv] + list(lams) + [subln_g]
        aliases = {}
        n_lat_chunks = SEQ // TK
    else:
        tq, nq = CTX_LEN, 1
        q_map = lambda b, h, i: (ctx_blk0 + b, h)
        in_specs = [pl.BlockSpec((tq, dv), q_map)] + ctx_specs + small
        in_specs.append(pl.BlockSpec(memory_space=pl.ANY))
        args = [qkv, qkv, qkv] + list(lams) + [subln_g, prev_out]
        aliases = {len(args) - 1: 0}
        n_lat_chunks = 0
    return pl.pallas_call(
        functools.partial(_attn_kernel, n_lat_chunks=n_lat_chunks, lam_init=lam_init),
        grid=(BATCH, DIFF_HEADS, nq),
        in_specs=in_specs,
        out_specs=pl.BlockSpec((tq, dv), q_map),
        out_shape=jax.ShapeDtypeStruct((NT, D_MODEL), BF16),
        scratch_shapes=[
            pltpu.VMEM((2 * tq, dv), F32),
            pltpu.VMEM((2 * tq, LANES), F32),
            pltpu.VMEM((2 * tq, LANES), F32),
        ],
        input_output_aliases=aliases,
        compiler_params=_params("arbitrary", "arbitrary", "arbitrary"),
        name="diff_attn_ctx" if n_lat_chunks == 0 else "diff_attn",
    )(*args)


def _ffn_kernel(a_ref, wo_ref, x_ref, g_ref, gate1_ref, shift_ref, scale_ref, gate2_ref,
                wg_ref, wu_ref, wd_ref, out_ref, h_ref):
    k = pl.program_id(1)

    @pl.when(k == 0)
    def _():
        x1 = x_ref[...] + gate1_ref[...] * jnp.dot(a_ref[...], wo_ref[...], preferred_element_type=F32)
        out_ref[...] = x1
        h_ref[...] = _norm_modulate(x1, g_ref[...], shift_ref[...], scale_ref[...]).astype(BF16)

    h = h_ref[...]
    gt = jnp.dot(h, wg_ref[...], preferred_element_type=F32)
    up = jnp.dot(h, wu_ref[...], preferred_element_type=F32)
    act = (gt * jax.nn.sigmoid(gt) * up).astype(BF16)
    out_ref[...] += gate2_ref[...] * jnp.dot(act, wd_ref[...], preferred_element_type=F32)


def _ffn(a, w_out, w_out_layer, xs, g, mods, layer, w_gu, w_down, n_tiles):
    nh = FFN_HIDDEN // TH
    return pl.pallas_call(
        _ffn_kernel,
        grid=(n_tiles, nh),
        in_specs=[
            pl.BlockSpec((TMF, D_MODEL), lambda i, k: (i, 0)),
            pl.BlockSpec((None, D_MODEL, D_MODEL), lambda i, k: (w_out_layer, 0, 0),
                         pipeline_mode=pl.Buffered(1)),
            pl.BlockSpec((TMF, D_MODEL), lambda i, k: (i, 0)),
            pl.BlockSpec((1, D_MODEL), lambda i, k: (0, 0)),
            _mod_spec(layer, 2, TMF),
            _mod_spec(layer, 3, TMF),
            _mod_spec(layer, 4, TMF),
            _mod_spec(layer, 5, TMF),
            pl.BlockSpec((None, D_MODEL, TH), lambda i, k: (layer, 0, k)),
            pl.BlockSpec((None, D_MODEL, TH), lambda i, k: (layer, 0, nh + k)),
            pl.BlockSpec((None, TH, D_MODEL), lambda i, k: (layer, k, 0)),
        ],
        out_specs=pl.BlockSpec((TMF, D_MODEL), lambda i, k: (i, 0)),
        out_shape=jax.ShapeDtypeStruct((n_tiles * TMF, D_MODEL), F32),
        scratch_shapes=[pltpu.VMEM((TMF, D_MODEL), BF16)],
        compiler_params=_params("arbitrary", "arbitrary"),
        name="ffn",
    )(a, w_out, xs, g, mods, mods, mods, mods, w_gu, w_gu, w_down)


def _pair_major(a):
    n_freq = DIFF_DH // 4
    lead = a.shape[:-1]
    a = a.reshape(*lead, a.shape[-1] // DIFF_DH, 2, 2, n_freq)
    return jnp.swapaxes(a, -3, -2).reshape(*lead, -1)


def _rope_tables():
    n_freq = DIFF_DH // 4
    pos = jnp.arange(SEQ)
    freqs = ROPE_BASE ** (-jnp.arange(n_freq, dtype=F32) / n_freq)
    ang = jnp.stack([pos // GRID_W, pos % GRID_W], axis=-1).astype(F32)[:, :, None] * freqs
    cos, sin = jnp.cos(ang), jnp.sin(ang)
    cos_t = jnp.concatenate([cos[:, 0], cos[:, 1], cos[:, 0], cos[:, 1]], axis=-1)
    sin_t = jnp.concatenate([-sin[:, 0], -sin[:, 1], sin[:, 0], sin[:, 1]], axis=-1)
    cos_t = jnp.concatenate([cos_t, jnp.ones((TM, LANES), F32)], axis=0)
    sin_t = jnp.concatenate([sin_t, jnp.zeros((TM, LANES), F32)], axis=0)
    return cos_t, sin_t


def kernel(x, c, ctx, c_ctx, ada_w, ada_b, norm_g, mlstm_w_in, mlstm_gate_b, mlstm_head_g, mlstm_w_out,
           diff_w_in, diff_w_out, diff_q_g, diff_k_g, diff_lq1, diff_lk1, diff_lq2, diff_lk2, diff_subln_g,
           ffn_w_gu, ffn_w_down):
    assert x.shape == (BATCH, SEQ, D_MODEL) and ctx.shape == (BATCH, CTX_LEN, D_MODEL)
    xs = jnp.concatenate([x.reshape(N_LAT, D_MODEL), ctx.reshape(N_CTX, D_MODEL)], axis=0)
    cc = jnp.concatenate([c, c_ctx[None], jnp.zeros((8 - BATCH - 1, D_MODEL), F32)], axis=0)
    mods = _ada(cc, ada_w, ada_b).reshape(DEPTH, 8, 1, 6 * D_MODEL)
    cos_t, sin_t = _rope_tables()
    row = lambda v: v.reshape(1, -1)
    ffn_gu = ffn_w_gu.astype(BF16)
    ffn_down = ffn_w_down.astype(BF16)
    mlstm_in = mlstm_w_in.astype(BF16)
    mlstm_kt = jnp.swapaxes(mlstm_w_in[:, :, MLSTM_QK:2 * MLSTM_QK], 1, 2).astype(BF16)
    mlstm_out = mlstm_w_out.astype(BF16)
    diff_in = jnp.concatenate([_pair_major(diff_w_in[..., :2 * D_MODEL]), diff_w_in[..., 2 * D_MODEL:]],
                              axis=-1).astype(BF16)
    diff_out = diff_w_out.astype(BF16)

    for i in range(DEPTH):
        last = i == DEPTH - 1
        n_rows = N_LAT if last else NT
        j = i // N_MIXERS
        if i % N_MIXERS == 0:
            w_gate = jnp.pad(mlstm_w_in[j, :, MLSTM_MAIN:], ((0, 0), (0, LANES - N_GATES))).astype(BF16)
            gate_b = jnp.pad(mlstm_gate_b[j], (0, LANES - N_GATES)).reshape(1, LANES)
            act, k_t, gates = _inproj_mlstm(xs, row(norm_g[i, 0]), mods, i, mlstm_in, mlstm_kt, j, w_gate)
            hf = _mlstm_scan(act, k_t, gates, gate_b, None, None, reverse=False)
            a = _mlstm_scan(act, k_t, gates, gate_b, row(mlstm_head_g[j]), hf, reverse=True)
            w_out = mlstm_out
        else:
            lam_init = 0.8 - 0.6 * math.exp(-0.3 * i)
            q_gain = _pair_major(diff_q_g[j]) * (DIFF_DH ** -0.5 * math.log2(math.e))
            qkg = jnp.concatenate([jnp.tile(q_gain, 2 * DIFF_HEADS),
                                   jnp.tile(_pair_major(diff_k_g[j]), 2 * DIFF_HEADS)]).reshape(1, 2 * D_MODEL)
            qkv = _inproj_diff(xs, row(norm_g[i, 0]), mods, i, diff_in, j, qkg, cos_t, sin_t)
            lams = [row(diff_lq1[j]), row(diff_lk1[j]), row(diff_lq2[j]), row(diff_lk2[j])]
            a = _attention(qkv, lams, row(diff_subln_g[j]), lam_init, None)
            if not last:
                a = _attention(qkv, lams, row(diff_subln_g[j]), lam_init, a)
            w_out = diff_out
        xs = _ffn(a, w_out, j, xs, row(norm_g[i, 1]), mods, i, ffn_gu, ffn_down, n_rows // TMF)
    return xs.reshape(BATCH, SEQ, D_MODEL)
```

```python
import functools
import math

import jax
import jax.numpy as jnp
from jax import lax
from jax.experimental import pallas as pl
from jax.experimental.pallas import tpu as pltpu

F32 = jnp.float32
BF16 = jnp.bfloat16

D_MODEL = 2048
BATCH = 4
SEQ = 4096
DEPTH = 4
GRID_W = 64
CTX_LEN = 256
N_MIXERS = 2

MLSTM_HEADS = 8
MLSTM_DK = D_MODEL // (2 * MLSTM_HEADS)
MLSTM_DV = D_MODEL // MLSTM_HEADS
GATE_CAP = 15.0
MLSTM_QK = MLSTM_HEADS * MLSTM_DK
MLSTM_MAIN = 2 * MLSTM_QK + 2 * D_MODEL
N_GATES = 4 * MLSTM_HEADS

DIFF_HEADS = 8
DIFF_DH = D_MODEL // (2 * DIFF_HEADS)
ROPE_BASE = 10000.0

FFN_HIDDEN = -(-(8 * D_MODEL) // (3 * 256)) * 256
EPS = 1e-6

N_LAT = BATCH * SEQ
N_CTX = BATCH * CTX_LEN
NT = N_LAT + N_CTX

LANES = 128
TM = 1024
TN = 1024
TMF = 512
TH = 512
CHUNK = 256
TQ = 1024
TK = 1024
VMEM_LIMIT = 56 * 1024 * 1024

NT_DIMS = (((1,), (1,)), ((), ()))
TN_DIMS = (((0,), (0,)), ((), ()))


def _params(*sem):
    return pltpu.CompilerParams(dimension_semantics=sem, vmem_limit_bytes=VMEM_LIMIT)


def _mod_row(i, tm):
    return jnp.minimum(i // (SEQ // tm), BATCH)


def _mod_spec(layer, k, tm=TM):
    return pl.BlockSpec((None, None, 1, D_MODEL), lambda i, j: (layer, _mod_row(i, tm), 0, k))


def _weight_spec(layer, tn):
    return pl.BlockSpec((None, D_MODEL, tn), lambda i, j: (layer, 0, j))


def _lane_repeat(x, n, axis):
    assert axis == 1
    return jnp.concatenate([x] * n, axis=1) if n > 1 else x


def _norm_modulate(x, g, shift, scale):
    y = x * lax.rsqrt(jnp.mean(x * x, axis=-1, keepdims=True) + EPS) * g
    return y * (1.0 + scale) + shift


def _ada_kernel(c_ref, w_ref, b_ref, o_ref):
    c = c_ref[...]
    a = (c * jax.nn.sigmoid(c)).astype(BF16)
    o_ref[...] = jnp.dot(a, w_ref[...].astype(BF16), preferred_element_type=F32) + b_ref[...]


def _ada(cc, ada_w, ada_b):
    tn = 1024
    return pl.pallas_call(
        _ada_kernel,
        grid=(DEPTH, 6 * D_MODEL // tn),
        in_specs=[
            pl.BlockSpec((8, D_MODEL), lambda l, j: (0, 0)),
            pl.BlockSpec((None, D_MODEL, tn), lambda l, j: (l, 0, j)),
            pl.BlockSpec((None, 1, tn), lambda l, j: (l, 0, j)),
        ],
        out_specs=pl.BlockSpec((None, 8, tn), lambda l, j: (l, 0, j)),
        out_shape=jax.ShapeDtypeStruct((DEPTH, 8, 6 * D_MODEL), F32),
        compiler_params=_params("arbitrary", "arbitrary"),
        name="ada_mod",
    )(cc, ada_w, ada_b.reshape(DEPTH, 1, 6 * D_MODEL))


MLSTM_ACT = MLSTM_MAIN - MLSTM_QK
Q_TILE = MLSTM_ACT // TN - 1


def _inproj_mlstm_kernel(x_ref, g_ref, shift_ref, scale_ref, w_ref, wkt_ref, wg_ref,
                         out_ref, kt_ref, gate_ref, h_ref):
    j = pl.program_id(1)

    @pl.when(j == 0)
    def _():
        h = _norm_modulate(x_ref[...], g_ref[...], shift_ref[...], scale_ref[...]).astype(BF16)
        h_ref[...] = h
        gate_ref[...] = jnp.dot(h, wg_ref[...], preferred_element_type=F32)

    @pl.when(j == 1)
    def _():
        kt_ref[...] = lax.dot_general(wkt_ref[...], h_ref[...], NT_DIMS,
                                      preferred_element_type=F32).astype(BF16)

    @pl.when(j != 1)
    def _():
        acc = jnp.dot(h_ref[...], w_ref[...], preferred_element_type=F32)
        qscale = jnp.where(j == 0, MLSTM_DK ** -0.5, 1.0).astype(F32)
        out_ref[...] = (acc * qscale).astype(BF16)


def _inproj_mlstm(xs, g, mods, layer, w_all, wkt_all, w_layer, w_gate):
    assert TN == MLSTM_QK
    return pl.pallas_call(
        _inproj_mlstm_kernel,
        grid=(NT // TM, MLSTM_MAIN // TN),
        in_specs=[
            pl.BlockSpec((TM, D_MODEL), lambda i, j: (i, 0)),
            pl.BlockSpec((1, D_MODEL), lambda i, j: (0, 0)),
            _mod_spec(layer, 0),
            _mod_spec(layer, 1),
            pl.BlockSpec((None, D_MODEL, TN), lambda i, j: (w_layer, 0, jnp.where(j == 1, 0, j))),
            pl.BlockSpec((None, MLSTM_QK, D_MODEL), lambda i, j: (w_layer, 0, 0),
                         pipeline_mode=pl.Buffered(1)),
            pl.BlockSpec((D_MODEL, LANES), lambda i, j: (0, 0)),
        ],
        out_specs=[
            pl.BlockSpec((TM, TN), lambda i, j: (i, jnp.where(j <= 1, Q_TILE, j - 2))),
            pl.BlockSpec((MLSTM_QK, TM), lambda i, j: (0, i)),
            pl.BlockSpec((TM, LANES), lambda i, j: (i, 0)),
        ],
        out_shape=[
            jax.ShapeDtypeStruct((NT, MLSTM_ACT), BF16),
            jax.ShapeDtypeStruct((MLSTM_QK, NT), BF16),
            jax.ShapeDtypeStruct((NT, LANES), F32),
        ],
        scratch_shapes=[pltpu.VMEM((TM, D_MODEL), BF16)],
        compiler_params=_params("arbitrary", "arbitrary"),
        name="inproj_mlstm",
    )(xs, g, mods, mods, w_all, wkt_all, w_gate)


def _mlstm_scan_kernel(*refs, reverse, final):
    if final:
        (q_ref, kt_ref, v_ref, gate_ref, gb_ref, o_ref, hf_ref, hg_ref,
         out_ref, c_ref, n_ref, m_ref) = refs
    else:
        q_ref, kt_ref, v_ref, gate_ref, gb_ref, out_ref, c_ref, n_ref, m_ref = refs
    L = CHUNK

    @pl.when(pl.program_id(1) == 0)
    def _():
        c_ref[...] = jnp.zeros_like(c_ref)
        n_ref[...] = jnp.zeros_like(n_ref)
        m_ref[...] = jnp.zeros_like(m_ref)

    gg = GATE_CAP * jnp.tanh((gate_ref[...] + gb_ref[...]) / GATE_CAP)
    lsig = jax.nn.log_sigmoid(gg)
    t_idx = lax.broadcasted_iota(jnp.int32, (L, L), 0)
    s_idx = lax.broadcasted_iota(jnp.int32, (L, L), 1)
    mask = (s_idx >= t_idx) if reverse else (s_idx <= t_idx)
    bcum = jnp.dot(mask.astype(F32), lsig, precision=lax.Precision.HIGHEST,
                   preferred_element_type=F32)
    bal = pltpu.roll(bcum, LANES - MLSTM_HEADS, 1)
    r_t = (gg - bal).T
    base = 2 * MLSTM_HEADS if reverse else 0
    last = 0 if reverse else L - 1

    for h in range(MLSTM_HEADS):
        col = base + h
        b_rep = jnp.broadcast_to(bal[:, col:col + 1], (L, LANES))
        r_row = r_t[col:col + 1, :]
        m = m_ref[h]
        m11 = m[:, 0:1]
        b_last = b_rep[last:last + 1, 0:1]

        dmat = jnp.where(mask, _lane_repeat(b_rep, L // LANES, 1) + r_row, -jnp.inf)
        inter = b_rep + m
        m_t = jnp.maximum(inter, jnp.max(dmat, axis=-1, keepdims=True))
        w_intra = jnp.exp(dmat - _lane_repeat(m_t, L // LANES, 1))
        w_inter = jnp.exp(inter - m_t)

        qh = q_ref[:, h * MLSTM_DK:(h + 1) * MLSTM_DK]
        kth = kt_ref[h * MLSTM_DK:(h + 1) * MLSTM_DK, :]
        vh = v_ref[:, h * MLSTM_DV:(h + 1) * MLSTM_DV]
        s = jnp.dot(qh, kth, preferred_element_type=F32) * w_intra
        c_old = c_ref[h]
        num = (_lane_repeat(w_inter, MLSTM_DV // LANES, 1)
               * jnp.dot(qh, c_old.astype(BF16), preferred_element_type=F32)
               + jnp.dot(s.astype(BF16), vh, preferred_element_type=F32))
        n_old = n_ref[h]
        qn = jnp.dot(qh, n_old.astype(BF16), preferred_element_type=F32)
        den = w_inter * qn + jnp.sum(s, axis=-1, keepdims=True)
        inv = 1.0 / jnp.maximum(jnp.abs(den), jnp.exp(-m_t))
        hout = num * _lane_repeat(inv, MLSTM_DV // LANES, 1)

        a_row = b_last + r_row
        m_new = jnp.maximum(b_last + m11, jnp.max(a_row, axis=-1, keepdims=True))
        wk = jnp.exp(a_row - m_new)
        dec = jnp.exp(b_last + m11 - m_new)
        kw_t = kth.astype(F32) * wk
        c_ref[h] = dec * c_old + jnp.dot(kw_t.astype(BF16), vh, preferred_element_type=F32)
        n_ref[h] = dec * n_old + jnp.sum(kw_t, axis=-1, keepdims=True)
        m_ref[h] = jnp.broadcast_to(m_new, (1, LANES))

        sl = slice(h * MLSTM_DV, (h + 1) * MLSTM_DV)
        if final:
            hs = hf_ref[:, sl] + hout
            y = hs * lax.rsqrt(jnp.mean(hs * hs, axis=-1, keepdims=True) + EPS) * hg_ref[:, sl]
            out_ref[:, sl] = (y * jax.nn.sigmoid(o_ref[:, sl].astype(F32))).astype(BF16)
        else:
            out_ref[:, sl] = hout


def _mlstm_scan(act, k_t, gates, gate_b, head_g, hf, *, reverse):
    final = hf is not None
    n_lat_chunks = SEQ // CHUNK
    ctx_blk0 = N_LAT // CHUNK

    def row_blk(b, c):
        j = (n_lat_chunks - c) if reverse else (c - 1)
        return jnp.where(c == 0, ctx_blk0 + b, b * n_lat_chunks + j)

    in_specs = [
        pl.BlockSpec((CHUNK, MLSTM_QK), lambda b, c: (row_blk(b, c), Q_TILE)),
        pl.BlockSpec((MLSTM_QK, CHUNK), lambda b, c: (0, row_blk(b, c))),
        pl.BlockSpec((CHUNK, D_MODEL), lambda b, c: (row_blk(b, c), 0)),
        pl.BlockSpec((CHUNK, LANES), lambda b, c: (row_blk(b, c), 0)),
        pl.BlockSpec((1, LANES), lambda b, c: (0, 0)),
    ]
    args = [act, k_t, act, gates, gate_b]
    if final:
        in_specs += [
            pl.BlockSpec((CHUNK, D_MODEL), lambda b, c: (row_blk(b, c), 1)),
            pl.BlockSpec((CHUNK, D_MODEL), lambda b, c: (row_blk(b, c), 0)),
            pl.BlockSpec((1, D_MODEL), lambda b, c: (0, 0)),
        ]
        args += [act, hf, head_g]
    return pl.pallas_call(
        functools.partial(_mlstm_scan_kernel, reverse=reverse, final=final),
        grid=(BATCH, 1 + n_lat_chunks),
        in_specs=in_specs,
        out_specs=pl.BlockSpec((CHUNK, D_MODEL), lambda b, c: (row_blk(b, c), 0)),
        out_shape=jax.ShapeDtypeStruct((NT, D_MODEL), BF16 if final else F32),
        scratch_shapes=[
            pltpu.VMEM((MLSTM_HEADS, MLSTM_DK, MLSTM_DV), F32),
            pltpu.VMEM((MLSTM_HEADS, MLSTM_DK, LANES), F32),
            pltpu.VMEM((MLSTM_HEADS, 1, LANES), F32),
        ],
        compiler_params=_params("arbitrary", "arbitrary"),
        name="mlstm_scan_bwd" if reverse else "mlstm_scan_fwd",
    )(*args)


def _inproj_diff_kernel(x_ref, g_ref, shift_ref, scale_ref, w_ref, qkg_ref, cos_ref, sin_ref,
                        out_ref, h_ref, acc0_ref, acc1_ref):
    j = pl.program_id(1)
    n_qk_tiles = 2 * D_MODEL // TN
    n_tiles = 3 * D_MODEL // TN

    def matmul_into(acc_ref):
        acc_ref[...] = jnp.dot(h_ref[...], w_ref[...], preferred_element_type=F32)

    def norm_rope_from(acc_ref):
        cos = cos_ref[...]
        sin = sin_ref[...]
        for grp in range(TN // LANES):
            sl = slice(grp * LANES, (grp + 1) * LANES)
            xg = acc_ref[:, sl]
            y = xg * lax.rsqrt(jnp.mean(xg * xg, axis=-1, keepdims=True) + EPS) * qkg_ref[:, sl]
            out_ref[:, sl] = (y * cos + pltpu.roll(y, LANES // 2, 1) * sin).astype(BF16)

    def cast_from(acc_ref):
        out_ref[...] = acc_ref[...].astype(BF16)

    @pl.when(j == 0)
    def _():
        h_ref[...] = _norm_modulate(x_ref[...], g_ref[...], shift_ref[...], scale_ref[...]).astype(BF16)
        matmul_into(acc0_ref)

    is_qk_finish = jnp.logical_and(j >= 1, j <= n_qk_tiles)

    @pl.when(jnp.logical_and(is_qk_finish, j % 2 == 1))
    def _():
        matmul_into(acc1_ref)
        norm_rope_from(acc0_ref)

    @pl.when(jnp.logical_and(is_qk_finish, j % 2 == 0))
    def _():
        matmul_into(acc0_ref)
        norm_rope_from(acc1_ref)

    assert n_qk_tiles % 2 == 0 and n_tiles == n_qk_tiles + 2

    @pl.when(j == n_tiles - 1)
    def _():
        matmul_into(acc1_ref)
        cast_from(acc0_ref)

    @pl.when(j == n_tiles)
    def _():
        cast_from(acc1_ref)


def _inproj_diff(xs, g, mods, layer, w_all, w_layer, qkg, cos_t, sin_t):
    n_qk_tiles = 2 * D_MODEL // TN
    n_tiles = 3 * D_MODEL // TN
    lat_tiles = SEQ // TM

    def rope_blk(i, j):
        return (jnp.where(i < N_LAT // TM, i % lat_tiles, lat_tiles), 0)

    return pl.pallas_call(
        _inproj_diff_kernel,
        grid=(NT // TM, n_tiles + 1),
        in_specs=[
            pl.BlockSpec((TM, D_MODEL), lambda i, j: (i, 0)),
            pl.BlockSpec((1, D_MODEL), lambda i, j: (0, 0)),
            _mod_spec(layer, 0),
            _mod_spec(layer, 1),
            pl.BlockSpec((None, D_MODEL, TN), lambda i, j: (w_layer, 0, jnp.minimum(j, n_tiles - 1))),
            pl.BlockSpec((1, TN), lambda i, j: (0, jnp.clip(j - 1, 0, n_qk_tiles - 1))),
            pl.BlockSpec((TM, LANES), rope_blk),
            pl.BlockSpec((TM, LANES), rope_blk),
        ],
        out_specs=pl.BlockSpec((TM, TN), lambda i, j: (i, jnp.maximum(j - 1, 0))),
        out_shape=jax.ShapeDtypeStruct((NT, 3 * D_MODEL), BF16),
        scratch_shapes=[
            pltpu.VMEM((TM, D_MODEL), BF16),
            pltpu.VMEM((TM, TN), F32),
            pltpu.VMEM((TM, TN), F32),
        ],
        compiler_params=_params("arbitrary", "arbitrary"),
        name="inproj_diff",
    )(xs, g, mods, mods, w_all, qkg, cos_t, sin_t)


def _attn_kernel(*refs, n_lat_chunks, lam_init):
    if n_lat_chunks:
        (q_ref, kl_ref, vl_ref, kc_ref, vc_ref, lq1_ref, lk1_ref, lq2_ref, lk2_ref, sg_ref,
         out_ref, acc_ref, m_ref, l_ref) = refs
    else:
        (q_ref, kc_ref, vc_ref, lq1_ref, lk1_ref, lq2_ref, lk2_ref, sg_ref, _,
         out_ref, acc_ref, m_ref, l_ref) = refs
    tq = q_ref.shape[0]
    dv = 2 * DIFF_DH
    q = q_ref[...]

    def scores(kblk):
        return jnp.concatenate(
            [lax.dot_general(q[:, t * DIFF_DH:(t + 1) * DIFF_DH], kblk[:, t * DIFF_DH:(t + 1) * DIFF_DH],
                             NT_DIMS, preferred_element_type=F32) for t in range(2)], axis=0)

    s = scores(kc_ref[...])
    m0 = jnp.max(s, axis=-1, keepdims=True)
    p = jnp.exp2(s - m0)
    m_ref[...] = jnp.broadcast_to(m0, m_ref.shape)
    l_ref[...] = jnp.broadcast_to(jnp.sum(p, axis=-1, keepdims=True), l_ref.shape)
    acc_ref[...] = jnp.dot(p.astype(BF16), vc_ref[...], preferred_element_type=F32)

    if n_lat_chunks:
        def body(c, carry):
            off = pl.multiple_of(c * TK, TK)
            s = scores(kl_ref[pl.ds(off, TK), :])
            m_old = m_ref[...]
            m_new = jnp.maximum(m_old, jnp.max(s, axis=-1, keepdims=True))
            p = jnp.exp2(s - _lane_repeat(m_new, TK // LANES, 1))
            alpha = jnp.exp2(m_old - m_new)
            l_ref[...] = alpha * l_ref[...] + jnp.sum(p, axis=-1, keepdims=True)
            acc_ref[...] = (_lane_repeat(alpha, dv // LANES, 1) * acc_ref[...]
                            + jnp.dot(p.astype(BF16), vl_ref[pl.ds(off, TK), :],
                                      preferred_element_type=F32))
            m_ref[...] = m_new
            return carry
        lax.fori_loop(0, n_lat_chunks, body, 0)

    lam = (jnp.exp(jnp.sum(lq1_ref[...] * lk1_ref[...], axis=-1, keepdims=True))
           - jnp.exp(jnp.sum(lq2_ref[...] * lk2_ref[...], axis=-1, keepdims=True)) + lam_init)
    o_all = acc_ref[...] / _lane_repeat(l_ref[...], dv // LANES, 1)
    o = o_all[:tq] - lam * o_all[tq:]
    y = o * lax.rsqrt(jnp.mean(o * o, axis=-1, keepdims=True) + EPS) * sg_ref[...]
    out_ref[...] = (y * (1.0 - lam_init)).astype(BF16)


def _attention(qkv, lams, subln_g, lam_init, prev_out):
    dv = 2 * DIFF_DH
    kcol = D_MODEL // dv
    vcol = 2 * D_MODEL // dv
    ctx_blk0 = N_LAT // CTX_LEN
    small = [pl.BlockSpec((1, DIFF_DH), lambda b, h, i: (0, 0))] * 4
    small.append(pl.BlockSpec((1, dv), lambda b, h, i: (0, 0)))
    ctx_specs = [
        pl.BlockSpec((CTX_LEN, dv), lambda b, h, i: (ctx_blk0 + b, kcol + h)),
        pl.BlockSpec((CTX_LEN, dv), lambda b, h, i: (ctx_blk0 + b, vcol + h)),
    ]
    if prev_out is None:
        tq, nq = TQ, SEQ // TQ
        q_map = lambda b, h, i: (b * nq + i, h)
        in_specs = [
            pl.BlockSpec((tq, dv), q_map),
            pl.BlockSpec((SEQ, dv), lambda b, h, i: (b, kcol + h)),
            pl.BlockSpec((SEQ, dv), lambda b, h, i: (b, vcol + h)),
        ] + ctx_specs + small
        args = [qkv, qkv, qkv, qkv, qkv] + list(lams) + [subln_g]
        aliases = {}
        n_lat_chunks = SEQ // TK
    else:
        tq, nq = CTX_LEN, 1
        q_map = lambda b, h, i: (ctx_blk0 + b, h)
        in_specs = [pl.BlockSpec((tq, dv), q_map)] + ctx_specs + small
        in_specs.append(pl.BlockSpec(memory_space=pl.ANY))
        args = [qkv, qkv, qkv] + list(lams) + [subln_g, prev_out]
        aliases = {len(args) - 1: 0}
        n_lat_chunks = 0
    return pl.pallas_call(
        functools.partial(_attn_kernel, n_lat_chunks=n_lat_chunks, lam_init=lam_init),
        grid=(BATCH, DIFF_HEADS, nq),
        in_specs=in_specs,
        out_specs=pl.BlockSpec((tq, dv), q_map),
        out_shape=jax.ShapeDtypeStruct((NT, D_MODEL), BF16),
        scratch_shapes=[
            pltpu.VMEM((2 * tq, dv), F32),
            pltpu.VMEM((2 * tq, LANES), F32),
            pltpu.VMEM((2 * tq, LANES), F32),
        ],
        input_output_aliases=aliases,
        compiler_params=_params("arbitrary", "arbitrary", "arbitrary"),
        name="diff_attn_ctx" if n_lat_chunks == 0 else "diff_attn",
    )(*args)


PRE_STEP = 1


def _ffn_kernel(a_ref, wo_ref, x_ref, g_ref, gate1_ref, shift_ref, scale_ref, gate2_ref,
                wg_ref, wu_ref, wd_ref, out_ref, h_ref, h_next_ref, x1_ref):
    i = pl.program_id(0)
    k = pl.program_id(1)

    def prologue():
        x1 = x_ref[...] + gate1_ref[...] * jnp.dot(a_ref[...], wo_ref[...], preferred_element_type=F32)
        x1_ref[...] = x1
        h_next_ref[...] = _norm_modulate(x1, g_ref[...], shift_ref[...], scale_ref[...]).astype(BF16)

    def ffn_step():
        h = h_ref[...]
        gt = jnp.dot(h, wg_ref[...], preferred_element_type=F32)
        up = jnp.dot(h, wu_ref[...], preferred_element_type=F32)
        act = (gt * jax.nn.sigmoid(gt) * up).astype(BF16)
        out_ref[...] += gate2_ref[...] * jnp.dot(act, wd_ref[...], preferred_element_type=F32)

    @pl.when(jnp.logical_and(i == 0, k == 0))
    def _():
        prologue()

    @pl.when(k == 0)
    def _():
        out_ref[...] = x1_ref[...]
        h_ref[...] = h_next_ref[...]

    runs_next_prologue = jnp.logical_and(k == PRE_STEP, i + 1 < pl.num_programs(0))

    @pl.when(runs_next_prologue)
    def _():
        prologue()
        ffn_step()

    @pl.when(jnp.logical_not(runs_next_prologue))
    def _():
        ffn_step()


def _ffn(a, w_out, w_out_layer, xs, g, mods, layer, w_gu, w_down, n_tiles):
    nh = FFN_HIDDEN // TH

    def pro_tile(i, k):
        return jnp.minimum(i + jnp.where(k >= PRE_STEP, 1, 0), n_tiles - 1)

    def pro_mod_spec(chunk):
        return pl.BlockSpec((None, None, 1, D_MODEL),
                            lambda i, k: (layer, _mod_row(pro_tile(i, k), TMF), 0, chunk))

    return pl.pallas_call(
        _ffn_kernel,
        grid=(n_tiles, nh),
        in_specs=[
            pl.BlockSpec((TMF, D_MODEL), lambda i, k: (pro_tile(i, k), 0)),
            pl.BlockSpec((None, D_MODEL, D_MODEL), lambda i, k: (w_out_layer, 0, 0),
                         pipeline_mode=pl.Buffered(1)),
            pl.BlockSpec((TMF, D_MODEL), lambda i, k: (pro_tile(i, k), 0)),
            pl.BlockSpec((1, D_MODEL), lambda i, k: (0, 0)),
            pro_mod_spec(2),
            pro_mod_spec(3),
            pro_mod_spec(4),
            _mod_spec(layer, 5, TMF),
            pl.BlockSpec((None, D_MODEL, TH), lambda i, k: (layer, 0, k)),
            pl.BlockSpec((None, D_MODEL, TH), lambda i, k: (layer, 0, nh + k)),
            pl.BlockSpec((None, TH, D_MODEL), lambda i, k: (layer, k, 0)),
        ],
        out_specs=pl.BlockSpec((TMF, D_MODEL), lambda i, k: (i, 0)),
        out_shape=jax.ShapeDtypeStruct((n_tiles * TMF, D_MODEL), F32),
        scratch_shapes=[
            pltpu.VMEM((TMF, D_MODEL), BF16),
            pltpu.VMEM((TMF, D_MODEL), BF16),
            pltpu.VMEM((TMF, D_MODEL), F32),
        ],
        compiler_params=_params("arbitrary", "arbitrary"),
        name="ffn",
    )(a, w_out, xs, g, mods, mods, mods, mods, w_gu, w_gu, w_down)


def _pair_major(a):
    n_freq = DIFF_DH // 4
    lead = a.shape[:-1]
    a = a.reshape(*lead, a.shape[-1] // DIFF_DH, 2, 2, n_freq)
    return jnp.swapaxes(a, -3, -2).reshape(*lead, -1)


def _rope_tables():
    n_freq = DIFF_DH // 4
    pos = jnp.arange(SEQ)
    freqs = ROPE_BASE ** (-jnp.arange(n_freq, dtype=F32) / n_freq)
    ang = jnp.stack([pos // GRID_W, pos % GRID_W], axis=-1).astype(F32)[:, :, None] * freqs
    cos, sin = jnp.cos(ang), jnp.sin(ang)
    cos_t = jnp.concatenate([cos[:, 0], cos[:, 1], cos[:, 0], cos[:, 1]], axis=-1)
    sin_t = jnp.concatenate([-sin[:, 0], -sin[:, 1], sin[:, 0], sin[:, 1]], axis=-1)
    cos_t = jnp.concatenate([cos_t, jnp.ones((TM, LANES), F32)], axis=0)
    sin_t = jnp.concatenate([sin_t, jnp.zeros((TM, LANES), F32)], axis=0)
    return cos_t, sin_t


def kernel(x, c, ctx, c_ctx, ada_w, ada_b, norm_g, mlstm_w_in, mlstm_gate_b, mlstm_head_g, mlstm_w_out,
           diff_w_in, diff_w_out, diff_q_g, diff_k_g, diff_lq1, diff_lk1, diff_lq2, diff_lk2, diff_subln_g,
           ffn_w_gu, ffn_w_down):
    assert x.shape == (BATCH, SEQ, D_MODEL) and ctx.shape == (BATCH, CTX_LEN, D_MODEL)
    xs = jnp.concatenate([x.reshape(N_LAT, D_MODEL), ctx.reshape(N_CTX, D_MODEL)], axis=0)
    cc = jnp.concatenate([c, c_ctx[None], jnp.zeros((8 - BATCH - 1, D_MODEL), F32)], axis=0)
    mods = _ada(cc, ada_w, ada_b).reshape(DEPTH, 8, 1, 6 * D_MODEL)
    cos_t, sin_t = _rope_tables()
    row = lambda v: v.reshape(1, -1)
    ffn_gu = ffn_w_gu.astype(BF16)
    ffn_down = ffn_w_down.astype(BF16)
    mlstm_in = mlstm_w_in.astype(BF16)
    mlstm_kt = jnp.swapaxes(mlstm_w_in[:, :, MLSTM_QK:2 * MLSTM_QK], 1, 2).astype(BF16)
    mlstm_out = mlstm_w_out.astype(BF16)
    diff_in = jnp.concatenate([_pair_major(diff_w_in[..., :2 * D_MODEL]), diff_w_in[..., 2 * D_MODEL:]],
                              axis=-1).astype(BF16)
    diff_out = diff_w_out.astype(BF16)

    for i in range(DEPTH):
        last = i == DEPTH - 1
        n_rows = N_LAT if last else NT
        j = i // N_MIXERS
        if i % N_MIXERS == 0:
            w_gate = jnp.pad(mlstm_w_in[j, :, MLSTM_MAIN:], ((0, 0), (0, LANES - N_GATES))).astype(BF16)
            gate_b = jnp.pad(mlstm_gate_b[j], (0, LANES - N_GATES)).reshape(1, LANES)
            act, k_t, gates = _inproj_mlstm(xs, row(norm_g[i, 0]), mods, i, mlstm_in, mlstm_kt, j, w_gate)
            hf = _mlstm_scan(act, k_t, gates, gate_b, None, None, reverse=False)
            a = _mlstm_scan(act, k_t, gates, gate_b, row(mlstm_head_g[j]), hf, reverse=True)
            w_out = mlstm_out
        else:
            lam_init = 0.8 - 0.6 * math.exp(-0.3 * i)
            q_gain = _pair_major(diff_q_g[j]) * (DIFF_DH ** -0.5 * math.log2(math.e))
            qkg = jnp.concatenate([jnp.tile(q_gain, 2 * DIFF_HEADS),
                                   jnp.tile(_pair_major(diff_k_g[j]), 2 * DIFF_HEADS)]).reshape(1, 2 * D_MODEL)
            qkv = _inproj_diff(xs, row(norm_g[i, 0]), mods, i, diff_in, j, qkg, cos_t, sin_t)
            lams = [row(diff_lq1[j]), row(diff_lk1[j]), row(diff_lq2[j]), row(diff_lk2[j])]
            a = _attention(qkv, lams, row(diff_subln_g[j]), lam_init, None)
            if not last:
                a = _attention(qkv, lams, row(diff_subln_g[j]), lam_init, a)
            w_out = diff_out
        xs = _ffn(a, w_out, j, xs, row(norm_g[i, 1]), mods, i, ffn_gu, ffn_down, n_rows // TMF)
    return xs.reshape(BATCH, SEQ, D_MODEL)
```

```python
import functools
import math

import jax
import jax.numpy as jnp
from jax import lax
from jax.experimental import pallas as pl
from jax.experimental.pallas import tpu as pltpu

F32 = jnp.float32
BF16 = jnp.bfloat16

D_MODEL = 2048
BATCH = 4
SEQ = 4096
DEPTH = 4
GRID_W = 64
CTX_LEN = 256
N_MIXERS = 2

MLSTM_HEADS = 8
MLSTM_DK = D_MODEL // (2 * MLSTM_HEADS)
MLSTM_DV = D_MODEL // MLSTM_HEADS
GATE_CAP = 15.0
MLSTM_QK = MLSTM_HEADS * MLSTM_DK
MLSTM_MAIN = 2 * MLSTM_QK + 2 * D_MODEL
N_GATES = 4 * MLSTM_HEADS

DIFF_HEADS = 8
DIFF_DH = D_MODEL // (2 * DIFF_HEADS)
ROPE_BASE = 10000.0

FFN_HIDDEN = -(-(8 * D_MODEL) // (3 * 256)) * 256
EPS = 1e-6

N_LAT = BATCH * SEQ
N_CTX = BATCH * CTX_LEN
NT = N_LAT + N_CTX

LANES = 128
TM = 1024
TN = 1024
TMF = 512
TH = 512
CHUNK = 256
TQ = 1024
TK = 1024
VMEM_LIMIT = 56 * 1024 * 1024

NT_DIMS = (((1,), (1,)), ((), ()))
TN_DIMS = (((0,), (0,)), ((), ()))


def _params(*sem):
    return pltpu.CompilerParams(dimension_semantics=sem, vmem_limit_bytes=VMEM_LIMIT)


def _mod_row(i, tm):
    return jnp.minimum(i // (SEQ // tm), BATCH)


def _mod_spec(layer, k, tm=TM):
    return pl.BlockSpec((None, None, 1, D_MODEL), lambda i, j: (layer, _mod_row(i, tm), 0, k))


def _weight_spec(layer, tn):
    return pl.BlockSpec((None, D_MODEL, tn), lambda i, j: (layer, 0, j))


def _lane_repeat(x, n, axis):
    assert axis == 1
    return jnp.concatenate([x] * n, axis=1) if n > 1 else x


def _norm_modulate(x, g, shift, scale):
    y = x * lax.rsqrt(jnp.mean(x * x, axis=-1, keepdims=True) + EPS) * g
    return y * (1.0 + scale) + shift


def _ada_kernel(c_ref, w_ref, b_ref, o_ref):
    c = c_ref[...]
    a = (c * jax.nn.sigmoid(c)).astype(BF16)
    o_ref[...] = jnp.dot(a, w_ref[...].astype(BF16), preferred_element_type=F32) + b_ref[...]


def _ada(cc, ada_w, ada_b):
    tn = 1024
    return pl.pallas_call(
        _ada_kernel,
        grid=(DEPTH, 6 * D_MODEL // tn),
        in_specs=[
            pl.BlockSpec((8, D_MODEL), lambda l, j: (0, 0)),
            pl.BlockSpec((None, D_MODEL, tn), lambda l, j: (l, 0, j)),
            pl.BlockSpec((None, 1, tn), lambda l, j: (l, 0, j)),
        ],
        out_specs=pl.BlockSpec((None, 8, tn), lambda l, j: (l, 0, j)),
        out_shape=jax.ShapeDtypeStruct((DEPTH, 8, 6 * D_MODEL), F32),
        compiler_params=_params("arbitrary", "arbitrary"),
        name="ada_mod",
    )(cc, ada_w, ada_b.reshape(DEPTH, 1, 6 * D_MODEL))


MLSTM_ACT = MLSTM_MAIN - MLSTM_QK
Q_TILE = MLSTM_ACT // TN - 1


def _inproj_mlstm_kernel(x_ref, g_ref, shift_ref, scale_ref, w_ref, wkt_ref, wg_ref,
                         out_ref, kt_ref, gate_ref, h_ref):
    j = pl.program_id(1)

    @pl.when(j == 0)
    def _():
        h_ref[...] = _norm_modulate(x_ref[...], g_ref[...], shift_ref[...], scale_ref[...]).astype(BF16)
        acc = jnp.dot(h_ref[...], w_ref[...], preferred_element_type=F32)
        out_ref[...] = (acc * MLSTM_DK ** -0.5).astype(BF16)
        gate_ref[...] = jnp.dot(h_ref[...], wg_ref[...], preferred_element_type=F32)

    @pl.when(j == 1)
    def _():
        kt_ref[...] = lax.dot_general(wkt_ref[...], h_ref[...], NT_DIMS,
                                      preferred_element_type=F32).astype(BF16)

    @pl.when(j > 1)
    def _():
        out_ref[...] = jnp.dot(h_ref[...], w_ref[...], preferred_element_type=F32).astype(BF16)


def _inproj_mlstm(xs, g, mods, layer, w_all, wkt_all, w_layer, w_gate):
    assert TN == MLSTM_QK
    return pl.pallas_call(
        _inproj_mlstm_kernel,
        grid=(NT // TM, MLSTM_MAIN // TN),
        in_specs=[
            pl.BlockSpec((TM, D_MODEL), lambda i, j: (i, 0)),
            pl.BlockSpec((1, D_MODEL), lambda i, j: (0, 0)),
            _mod_spec(layer, 0),
            _mod_spec(layer, 1),
            pl.BlockSpec((None, D_MODEL, TN), lambda i, j: (w_layer, 0, jnp.where(j == 1, 0, j))),
            pl.BlockSpec((None, MLSTM_QK, D_MODEL), lambda i, j: (w_layer, 0, 0),
                         pipeline_mode=pl.Buffered(1)),
            pl.BlockSpec((D_MODEL, LANES), lambda i, j: (0, 0)),
        ],
        out_specs=[
            pl.BlockSpec((TM, TN), lambda i, j: (i, jnp.where(j <= 1, Q_TILE, j - 2))),
            pl.BlockSpec((MLSTM_QK, TM), lambda i, j: (0, i)),
            pl.BlockSpec((TM, LANES), lambda i, j: (i, 0)),
        ],
        out_shape=[
            jax.ShapeDtypeStruct((NT, MLSTM_ACT), BF16),
            jax.ShapeDtypeStruct((MLSTM_QK, NT), BF16),
            jax.ShapeDtypeStruct((NT, LANES), F32),
        ],
        scratch_shapes=[pltpu.VMEM((TM, D_MODEL), BF16)],
        compiler_params=_params("arbitrary", "arbitrary"),
        name="inproj_mlstm",
    )(xs, g, mods, mods, w_all, wkt_all, w_gate)


def _mlstm_scan_kernel(*refs, reverse, final):
    if final:
        (q_ref, kt_ref, v_ref, gate_ref, gb_ref, o_ref, hf_ref, hg_ref,
         out_ref, c_ref, n_ref, m_ref) = refs
    else:
        q_ref, kt_ref, v_ref, gate_ref, gb_ref, out_ref, c_ref, n_ref, m_ref = refs
    L = CHUNK

    @pl.when(pl.program_id(1) == 0)
    def _():
        c_ref[...] = jnp.zeros_like(c_ref)
        n_ref[...] = jnp.zeros_like(n_ref)
        m_ref[...] = jnp.zeros_like(m_ref)

    gg = GATE_CAP * jnp.tanh((gate_ref[...] + gb_ref[...]) / GATE_CAP)
    lsig = jax.nn.log_sigmoid(gg)
    t_idx = lax.broadcasted_iota(jnp.int32, (L, L), 0)
    s_idx = lax.broadcasted_iota(jnp.int32, (L, L), 1)
    mask = (s_idx >= t_idx) if reverse else (s_idx <= t_idx)
    tri = mask.astype(BF16)
    hi = lsig.astype(BF16)
    rest = lsig - hi.astype(F32)
    mid = rest.astype(BF16)
    lo = (rest - mid.astype(F32)).astype(BF16)
    bcum = (jnp.dot(tri, hi, preferred_element_type=F32) + jnp.dot(tri, mid, preferred_element_type=F32)
            + jnp.dot(tri, lo, preferred_element_type=F32))
    bal = pltpu.roll(bcum, LANES - MLSTM_HEADS, 1)
    r_t = (gg - bal).T
    base = 2 * MLSTM_HEADS if reverse else 0
    last = 0 if reverse else L - 1

    for h in range(MLSTM_HEADS):
        col = base + h
        b_rep = jnp.broadcast_to(bal[:, col:col + 1], (L, LANES))
        r_row = r_t[col:col + 1, :]
        m = m_ref[h]
        m11 = m[:, 0:1]
        b_last = b_rep[last:last + 1, 0:1]

        dmat = jnp.where(mask, _lane_repeat(b_rep, L // LANES, 1) + r_row, -jnp.inf)
        inter = b_rep + m
        m_t = jnp.maximum(inter, jnp.max(dmat, axis=-1, keepdims=True))
        w_intra = jnp.exp(dmat - _lane_repeat(m_t, L // LANES, 1))
        w_inter = jnp.exp(inter - m_t)

        qh = q_ref[:, h * MLSTM_DK:(h + 1) * MLSTM_DK]
        kth = kt_ref[h * MLSTM_DK:(h + 1) * MLSTM_DK, :]
        vh = v_ref[:, h * MLSTM_DV:(h + 1) * MLSTM_DV]
        s = jnp.dot(qh, kth, preferred_element_type=F32) * w_intra
        c_old = c_ref[h]
        num = (_lane_repeat(w_inter, MLSTM_DV // LANES, 1)
               * jnp.dot(qh, c_old.astype(BF16), preferred_element_type=F32)
               + jnp.dot(s.astype(BF16), vh, preferred_element_type=F32))
        n_old = n_ref[h]
        qn = jnp.dot(qh, n_old.astype(BF16), preferred_element_type=F32)
        den = w_inter * qn + jnp.sum(s, axis=-1, keepdims=True)
        inv = 1.0 / jnp.maximum(jnp.abs(den), jnp.exp(-m_t))
        hout = num * _lane_repeat(inv, MLSTM_DV // LANES, 1)

        a_row = b_last + r_row
        m_new = jnp.maximum(b_last + m11, jnp.max(a_row, axis=-1, keepdims=True))
        wk = jnp.exp(a_row - m_new)
        dec = jnp.exp(b_last + m11 - m_new)
        kw_t = kth.astype(F32) * wk
        c_ref[h] = dec * c_old + jnp.dot(kw_t.astype(BF16), vh, preferred_element_type=F32)
        n_ref[h] = dec * n_old + jnp.sum(kw_t, axis=-1, keepdims=True)
        m_ref[h] = jnp.broadcast_to(m_new, (1, LANES))

        sl = slice(h * MLSTM_DV, (h + 1) * MLSTM_DV)
        if final:
            hs = hf_ref[:, sl] + hout
            y = hs * lax.rsqrt(jnp.mean(hs * hs, axis=-1, keepdims=True) + EPS) * hg_ref[:, sl]
            out_ref[:, sl] = (y * jax.nn.sigmoid(o_ref[:, sl].astype(F32))).astype(BF16)
        else:
            out_ref[:, sl] = hout


def _mlstm_scan(act, k_t, gates, gate_b, head_g, hf, *, reverse):
    final = hf is not None
    n_lat_chunks = SEQ // CHUNK
    ctx_blk0 = N_LAT // CHUNK

    def row_blk(b, c):
        j = (n_lat_chunks - c) if reverse else (c - 1)
        return jnp.where(c == 0, ctx_blk0 + b, b * n_lat_chunks + j)

    in_specs = [
        pl.BlockSpec((CHUNK, MLSTM_QK), lambda b, c: (row_blk(b, c), Q_TILE)),
        pl.BlockSpec((MLSTM_QK, CHUNK), lambda b, c: (0, row_blk(b, c))),
        pl.BlockSpec((CHUNK, D_MODEL), lambda b, c: (row_blk(b, c), 0)),
        pl.BlockSpec((CHUNK, LANES), lambda b, c: (row_blk(b, c), 0)),
        pl.BlockSpec((1, LANES), lambda b, c: (0, 0)),
    ]
    args = [act, k_t, act, gates, gate_b]
    if final:
        in_specs += [
            pl.BlockSpec((CHUNK, D_MODEL), lambda b, c: (row_blk(b, c), 1)),
            pl.BlockSpec((CHUNK, D_MODEL), lambda b, c: (row_blk(b, c), 0)),
            pl.BlockSpec((1, D_MODEL), lambda b, c: (0, 0)),
        ]
        args += [act, hf, head_g]
    return pl.pallas_call(
        functools.partial(_mlstm_scan_kernel, reverse=reverse, final=final),
        grid=(BATCH, 1 + n_lat_chunks),
        in_specs=in_specs,
        out_specs=pl.BlockSpec((CHUNK, D_MODEL), lambda b, c: (row_blk(b, c), 0)),
        out_shape=jax.ShapeDtypeStruct((NT, D_MODEL), BF16 if final else F32),
        scratch_shapes=[
            pltpu.VMEM((MLSTM_HEADS, MLSTM_DK, MLSTM_DV), F32),
            pltpu.VMEM((MLSTM_HEADS, MLSTM_DK, LANES), F32),
            pltpu.VMEM((MLSTM_HEADS, 1, LANES), F32),
        ],
        compiler_params=_params("arbitrary", "arbitrary"),
        name="mlstm_scan_bwd" if reverse else "mlstm_scan_fwd",
    )(*args)


def _inproj_diff_kernel(x_ref, g_ref, shift_ref, scale_ref, w_ref, qkg_ref, cos_ref, sin_ref,
                        out_ref, h_ref, acc0_ref, acc1_ref):
    j = pl.program_id(1)
    n_qk_tiles = 2 * D_MODEL // TN
    n_tiles = 3 * D_MODEL // TN

    def matmul_into(acc_ref):
        acc_ref[...] = jnp.dot(h_ref[...], w_ref[...], preferred_element_type=F32)

    def norm_rope_from(acc_ref):
        cos = cos_ref[...]
        sin = sin_ref[...]
        for grp in range(TN // LANES):
            sl = slice(grp * LANES, (grp + 1) * LANES)
            xg = acc_ref[:, sl]
            y = xg * lax.rsqrt(jnp.mean(xg * xg, axis=-1, keepdims=True) + EPS) * qkg_ref[:, sl]
            out_ref[:, sl] = (y * cos + pltpu.roll(y, LANES // 2, 1) * sin).astype(BF16)

    def cast_from(acc_ref):
        out_ref[...] = acc_ref[...].astype(BF16)

    @pl.when(j == 0)
    def _():
        h_ref[...] = _norm_modulate(x_ref[...], g_ref[...], shift_ref[...], scale_ref[...]).astype(BF16)
        matmul_into(acc0_ref)

    is_qk_finish = jnp.logical_and(j >= 1, j <= n_qk_tiles)

    @pl.when(jnp.logical_and(is_qk_finish, j % 2 == 1))
    def _():
        matmul_into(acc1_ref)
        norm_rope_from(acc0_ref)

    @pl.when(jnp.logical_and(is_qk_finish, j % 2 == 0))
    def _():
        matmul_into(acc0_ref)
        norm_rope_from(acc1_ref)

    assert n_qk_tiles % 2 == 0 and n_tiles == n_qk_tiles + 2

    @pl.when(j == n_tiles - 1)
    def _():
        matmul_into(acc1_ref)
        cast_from(acc0_ref)

    @pl.when(j == n_tiles)
    def _():
        cast_from(acc1_ref)


def _inproj_diff(xs, g, mods, layer, w_all, w_layer, qkg, cos_t, sin_t):
    n_qk_tiles = 2 * D_MODEL // TN
    n_tiles = 3 * D_MODEL // TN
    lat_tiles = SEQ // TM

    def rope_blk(i, j):
        return (jnp.where(i < N_LAT // TM, i % lat_tiles, lat_tiles), 0)

    return pl.pallas_call(
        _inproj_diff_kernel,
        grid=(NT // TM, n_tiles + 1),
        in_specs=[
            pl.BlockSpec((TM, D_MODEL), lambda i, j: (i, 0)),
            pl.BlockSpec((1, D_MODEL), lambda i, j: (0, 0)),
            _mod_spec(layer, 0),
            _mod_spec(layer, 1),
            pl.BlockSpec((None, D_MODEL, TN), lambda i, j: (w_layer, 0, jnp.minimum(j, n_tiles - 1))),
            pl.BlockSpec((1, TN), lambda i, j: (0, jnp.clip(j - 1, 0, n_qk_tiles - 1))),
            pl.BlockSpec((TM, LANES), rope_blk),
            pl.BlockSpec((TM, LANES), rope_blk),
        ],
        out_specs=pl.BlockSpec((TM, TN), lambda i, j: (i, jnp.maximum(j - 1, 0))),
        out_shape=jax.ShapeDtypeStruct((NT, 3 * D_MODEL), BF16),
        scratch_shapes=[
            pltpu.VMEM((TM, D_MODEL), BF16),
            pltpu.VMEM((TM, TN), F32),
            pltpu.VMEM((TM, TN), F32),
        ],
        compiler_params=_params("arbitrary", "arbitrary"),
        name="inproj_diff",
    )(xs, g, mods, mods, w_all, qkg, cos_t, sin_t)


def _attn_kernel(*refs, n_lat_chunks, lam_init):
    if n_lat_chunks:
        (q_ref, kl_ref, vl_ref, kc_ref, vc_ref, lq1_ref, lk1_ref, lq2_ref, lk2_ref, sg_ref,
         out_ref, acc_ref, m_ref, l_ref) = refs
    else:
        (q_ref, kc_ref, vc_ref, lq1_ref, lk1_ref, lq2_ref, lk2_ref, sg_ref, _,
         out_ref, acc_ref, m_ref, l_ref) = refs
    tq = q_ref.shape[0]
    dv = 2 * DIFF_DH
    q = q_ref[...]

    def scores(kblk):
        return jnp.concatenate(
            [lax.dot_general(q[:, t * DIFF_DH:(t + 1) * DIFF_DH], kblk[:, t * DIFF_DH:(t + 1) * DIFF_DH],
                             NT_DIMS, preferred_element_type=F32) for t in range(2)], axis=0)

    s = scores(kc_ref[...])
    m0 = jnp.max(s, axis=-1, keepdims=True)
    p = jnp.exp2(s - m0)
    m_ref[...] = jnp.broadcast_to(m0, m_ref.shape)
    l_ref[...] = jnp.broadcast_to(jnp.sum(p, axis=-1, keepdims=True), l_ref.shape)
    acc_ref[...] = jnp.dot(p.astype(BF16), vc_ref[...], preferred_element_type=F32)

    if n_lat_chunks:
        def body(c, carry):
            off = pl.multiple_of(c * TK, TK)
            s = scores(kl_ref[pl.ds(off, TK), :])
            m_old = m_ref[...]
            m_new = jnp.maximum(m_old, jnp.max(s, axis=-1, keepdims=True))
            p = jnp.exp2(s - _lane_repeat(m_new, TK // LANES, 1))
            alpha = jnp.exp2(m_old - m_new)
            l_ref[...] = alpha * l_ref[...] + jnp.sum(p, axis=-1, keepdims=True)
            acc_ref[...] = (_lane_repeat(alpha, dv // LANES, 1) * acc_ref[...]
                            + jnp.dot(p.astype(BF16), vl_ref[pl.ds(off, TK), :],
                                      preferred_element_type=F32))
            m_ref[...] = m_new
            return carry
        lax.fori_loop(0, n_lat_chunks, body, 0)

    lam = (jnp.exp(jnp.sum(lq1_ref[...] * lk1_ref[...], axis=-1, keepdims=True))
           - jnp.exp(jnp.sum(lq2_ref[...] * lk2_ref[...], axis=-1, keepdims=True)) + lam_init)
    o_all = acc_ref[...] / _lane_repeat(l_ref[...], dv // LANES, 1)
    o = o_all[:tq] - lam * o_all[tq:]
    y = o * lax.rsqrt(jnp.mean(o * o, axis=-1, keepdims=True) + EPS) * sg_ref[...]
    out_ref[...] = (y * (1.0 - lam_init)).astype(BF16)


def _attention(qkv, lams, subln_g, lam_init, prev_out):
    dv = 2 * DIFF_DH
    kcol = D_MODEL // dv
    vcol = 2 * D_MODEL // dv
    ctx_blk0 = N_LAT // CTX_LEN
    small = [pl.BlockSpec((1, DIFF_DH), lambda b, h, i: (0, 0))] * 4
    small.append(pl.BlockSpec((1, dv), lambda b, h, i: (0, 0)))
    ctx_specs = [
        pl.BlockSpec((CTX_LEN, dv), lambda b, h, i: (ctx_blk0 + b, kcol + h)),
        pl.BlockSpec((CTX_LEN, dv), lambda b, h, i: (ctx_blk0 + b, vcol + h)),
    ]
    if prev_out is None:
        tq, nq = TQ, SEQ // TQ
        q_map = lambda b, h, i: (b * nq + i, h)
        in_specs = [
            pl.BlockSpec((tq, dv), q_map),
            pl.BlockSpec((SEQ, dv), lambda b, h, i: (b, kcol + h)),
            pl.BlockSpec((SEQ, dv), lambda b, h, i: (b, vcol + h)),
        ] + ctx_specs + small
        args = [qkv, qkv, qkv, qkv, qkv] + list(lams) + [subln_g]
        aliases = {}
        n_lat_chunks = SEQ // TK
    else:
        tq, nq = CTX_LEN, 1
        q_map = lambda b, h, i: (ctx_blk0 + b, h)
        in_specs = [pl.BlockSpec((tq, dv), q_map)] + ctx_specs + small
        in_specs.append(pl.BlockSpec(memory_space=pl.ANY))
        args = [qkv, qkv, qkv] + list(lams) + [subln_g, prev_out]
        aliases = {len(args) - 1: 0}
        n_lat_chunks = 0
    return pl.pallas_call(
        functools.partial(_attn_kernel, n_lat_chunks=n_lat_chunks, lam_init=lam_init),
        grid=(BATCH, DIFF_HEADS, nq),
        in_specs=in_specs,
        out_specs=pl.BlockSpec((tq, dv), q_map),
        out_shape=jax.ShapeDtypeStruct((NT, D_MODEL), BF16),
        scratch_shapes=[
            pltpu.VMEM((2 * tq, dv), F32),
            pltpu.VMEM((2 * tq, LANES), F32),
            pltpu.VMEM((2 * tq, LANES), F32),
        ],
        input_output_aliases=aliases,
        compiler_params=_params("arbitrary", "arbitrary", "arbitrary"),
        name="diff_attn_ctx" if n_lat_chunks == 0 else "diff_attn",
    )(*args)


def _ffn_kernel(a_ref, wo_ref, x_ref, g_ref, gate1_ref, shift_ref, scale_ref, gate2_ref,
                wg_ref, wu_ref, wd_ref, out_ref, h_ref):
    k = pl.program_id(1)

    def hidden_tile(h, base):
        gt = jnp.dot(h, wg_ref[...], preferred_element_type=F32)
        up = jnp.dot(h, wu_ref[...], preferred_element_type=F32)
        act = (gt * jax.nn.sigmoid(gt) * up).astype(BF16)
        out_ref[...] = base + gate2_ref[...] * jnp.dot(act, wd_ref[...], preferred_element_type=F32)

    @pl.when(k == 0)
    def _():
        x1 = x_ref[...] + gate1_ref[...] * jnp.dot(a_ref[...], wo_ref[...], preferred_element_type=F32)
        h = _norm_modulate(x1, g_ref[...], shift_ref[...], scale_ref[...]).astype(BF16)
        h_ref[...] = h
        hidden_tile(h, x1)

    @pl.when(k > 0)
    def _():
        hidden_tile(h_ref[...], out_ref[...])


def _ffn(a, w_out, w_out_layer, xs, g, mods, layer, w_gu, w_down, n_tiles):
    nh = FFN_HIDDEN // TH
    return pl.pallas_call(
        _ffn_kernel,
        grid=(n_tiles, nh),
        in_specs=[
            pl.BlockSpec((TMF, D_MODEL), lambda i, k: (i, 0)),
            pl.BlockSpec((None, D_MODEL, D_MODEL), lambda i, k: (w_out_layer, 0, 0),
                         pipeline_mode=pl.Buffered(1)),
            pl.BlockSpec((TMF, D_MODEL), lambda i, k: (i, 0)),
            pl.BlockSpec((1, D_MODEL), lambda i, k: (0, 0)),
            _mod_spec(layer, 2, TMF),
            _mod_spec(layer, 3, TMF),
            _mod_spec(layer, 4, TMF),
            _mod_spec(layer, 5, TMF),
            pl.BlockSpec((None, D_MODEL, TH), lambda i, k: (layer, 0, k)),
            pl.BlockSpec((None, D_MODEL, TH), lambda i, k: (layer, 0, nh + k)),
            pl.BlockSpec((None, TH, D_MODEL), lambda i, k: (layer, k, 0)),
        ],
        out_specs=pl.BlockSpec((TMF, D_MODEL), lambda i, k: (i, 0)),
        out_shape=jax.ShapeDtypeStruct((n_tiles * TMF, D_MODEL), F32),
        scratch_shapes=[pltpu.VMEM((TMF, D_MODEL), BF16)],
        compiler_params=_params("arbitrary", "arbitrary"),
        name="ffn",
    )(a, w_out, xs, g, mods, mods, mods, mods, w_gu, w_gu, w_down)


def _pair_major(a):
    n_freq = DIFF_DH // 4
    lead = a.shape[:-1]
    a = a.reshape(*lead, a.shape[-1] // DIFF_DH, 2, 2, n_freq)
    return jnp.swapaxes(a, -3, -2).reshape(*lead, -1)


def _rope_tables():
    n_freq = DIFF_DH // 4
    pos = jnp.arange(SEQ)
    freqs = ROPE_BASE ** (-jnp.arange(n_freq, dtype=F32) / n_freq)
    ang = jnp.stack([pos // GRID_W, pos % GRID_W], axis=-1).astype(F32)[:, :, None] * freqs
    cos, sin = jnp.cos(ang), jnp.sin(ang)
    cos_t = jnp.concatenate([cos[:, 0], cos[:, 1], cos[:, 0], cos[:, 1]], axis=-1)
    sin_t = jnp.concatenate([-sin[:, 0], -sin[:, 1], sin[:, 0], sin[:, 1]], axis=-1)
    cos_t = jnp.concatenate([cos_t, jnp.ones((TM, LANES), F32)], axis=0)
    sin_t = jnp.concatenate([sin_t, jnp.zeros((TM, LANES), F32)], axis=0)
    return cos_t, sin_t


def kernel(x, c, ctx, c_ctx, ada_w, ada_b, norm_g, mlstm_w_in, mlstm_gate_b, mlstm_head_g, mlstm_w_out,
           diff_w_in, diff_w_out, diff_q_g, diff_k_g, diff_lq1, diff_lk1, diff_lq2, diff_lk2, diff_subln_g,
           ffn_w_gu, ffn_w_down):
    assert x.shape == (BATCH, SEQ, D_MODEL) and ctx.shape == (BATCH, CTX_LEN, D_MODEL)
    xs = jnp.concatenate([x.reshape(N_LAT, D_MODEL), ctx.reshape(N_CTX, D_MODEL)], axis=0)
    cc = jnp.concatenate([c, c_ctx[None], jnp.zeros((8 - BATCH - 1, D_MODEL), F32)], axis=0)
    mods = _ada(cc, ada_w, ada_b).reshape(DEPTH, 8, 1, 6 * D_MODEL)
    cos_t, sin_t = _rope_tables()
    row = lambda v: v.reshape(1, -1)
    ffn_gu = ffn_w_gu.astype(BF16)
    ffn_down = ffn_w_down.astype(BF16)
    mlstm_in = mlstm_w_in.astype(BF16)
    mlstm_kt = jnp.swapaxes(mlstm_w_in[:, :, MLSTM_QK:2 * MLSTM_QK], 1, 2).astype(BF16)
    mlstm_out = mlstm_w_out.astype(BF16)
    diff_in = jnp.concatenate([_pair_major(diff_w_in[..., :2 * D_MODEL]), diff_w_in[..., 2 * D_MODEL:]],
                              axis=-1).astype(BF16)
    diff_out = diff_w_out.astype(BF16)

    for i in range(DEPTH):
        last = i == DEPTH - 1
        n_rows = N_LAT if last else NT
        j = i // N_MIXERS
        if i % N_MIXERS == 0:
            w_gate = jnp.pad(mlstm_w_in[j, :, MLSTM_MAIN:], ((0, 0), (0, LANES - N_GATES))).astype(BF16)
            gate_b = jnp.pad(mlstm_gate_b[j], (0, LANES - N_GATES)).reshape(1, LANES)
            act, k_t, gates = _inproj_mlstm(xs, row(norm_g[i, 0]), mods, i, mlstm_in, mlstm_kt, j, w_gate)
            hf = _mlstm_scan(act, k_t, gates, gate_b, None, None, reverse=False)
            a = _mlstm_scan(act, k_t, gates, gate_b, row(mlstm_head_g[j]), hf, reverse=True)
            w_out = mlstm_out
        else:
            lam_init = 0.8 - 0.6 * math.exp(-0.3 * i)
            q_gain = _pair_major(diff_q_g[j]) * (DIFF_DH ** -0.5 * math.log2(math.e))
            qkg = jnp.concatenate([jnp.tile(q_gain, 2 * DIFF_HEADS),
                                   jnp.tile(_pair_major(diff_k_g[j]), 2 * DIFF_HEADS)]).reshape(1, 2 * D_MODEL)
            qkv = _inproj_diff(xs, row(norm_g[i, 0]), mods, i, diff_in, j, qkg, cos_t, sin_t)
            lams = [row(diff_lq1[j]), row(diff_lk1[j]), row(diff_lq2[j]), row(diff_lk2[j])]
            a = _attention(qkv, lams, row(diff_subln_g[j]), lam_init, None)
            if not last:
                a = _attention(qkv, lams, row(diff_subln_g[j]), lam_init, a)
            w_out = diff_out
        xs = _ffn(a, w_out, j, xs, row(norm_g[i, 1]), mods, i, ffn_gu, ffn_down, n_rows // TMF)
    return xs.reshape(BATCH, SEQ, D_MODEL)
```

```python
import functools
import math

import jax
import jax.numpy as jnp
from jax import lax
from jax.experimental import pallas as pl
from jax.experimental.pallas import tpu as pltpu

F32 = jnp.float32
BF16 = jnp.bfloat16

D_MODEL = 2048
BATCH = 4
SEQ = 4096
DEPTH = 4
GRID_W = 64
CTX_LEN = 256
N_MIXERS = 2

MLSTM_HEADS = 8
MLSTM_DK = D_MODEL // (2 * MLSTM_HEADS)
MLSTM_DV = D_MODEL // MLSTM_HEADS
GATE_CAP = 15.0
MLSTM_QK = MLSTM_HEADS * MLSTM_DK
MLSTM_MAIN = 2 * MLSTM_QK + 2 * D_MODEL
N_GATES = 4 * MLSTM_HEADS

DIFF_HEADS = 8
DIFF_DH = D_MODEL // (2 * DIFF_HEADS)
ROPE_BASE = 10000.0

FFN_HIDDEN = -(-(8 * D_MODEL) // (3 * 256)) * 256
EPS = 1e-6

N_LAT = BATCH * SEQ
N_CTX = BATCH * CTX_LEN
NT = N_LAT + N_CTX

LANES = 128
TM = 1024
TN = 1024
TMF = 512
TH = 512
CHUNK = 256
TQ = 1024
TK = 1024
VMEM_LIMIT = 56 * 1024 * 1024

NT_DIMS = (((1,), (1,)), ((), ()))
TN_DIMS = (((0,), (0,)), ((), ()))


def _params(*sem):
    return pltpu.CompilerParams(dimension_semantics=sem, vmem_limit_bytes=VMEM_LIMIT)


def _mod_row(i, tm):
    return jnp.minimum(i // (SEQ // tm), BATCH)


def _mod_spec(layer, k, tm=TM, tile=lambda i, j: i):
    return pl.BlockSpec((None, None, 1, D_MODEL), lambda i, j: (layer, _mod_row(tile(i, j), tm), 0, k))


def _early_tile(n_tiles, from_step):
    return lambda i, j: jnp.minimum(i + jnp.where(j >= from_step, 1, 0), n_tiles - 1)


def _weight_spec(layer, tn):
    return pl.BlockSpec((None, D_MODEL, tn), lambda i, j: (layer, 0, j))


def _lane_repeat(x, n, axis):
    assert axis == 1
    return jnp.concatenate([x] * n, axis=1) if n > 1 else x


def _norm_modulate(x, g, shift, scale):
    y = x * lax.rsqrt(jnp.mean(x * x, axis=-1, keepdims=True) + EPS) * g
    return y * (1.0 + scale) + shift


def _ada_kernel(c_ref, w_ref, b_ref, o_ref):
    c = c_ref[...]
    a = (c * jax.nn.sigmoid(c)).astype(BF16)
    o_ref[...] = jnp.dot(a, w_ref[...].astype(BF16), preferred_element_type=F32) + b_ref[...]


def _ada(cc, ada_w, ada_b):
    tn = 1024
    return pl.pallas_call(
        _ada_kernel,
        grid=(DEPTH, 6 * D_MODEL // tn),
        in_specs=[
            pl.BlockSpec((8, D_MODEL), lambda l, j: (0, 0)),
            pl.BlockSpec((None, D_MODEL, tn), lambda l, j: (l, 0, j)),
            pl.BlockSpec((None, 1, tn), lambda l, j: (l, 0, j)),
        ],
        out_specs=pl.BlockSpec((None, 8, tn), lambda l, j: (l, 0, j)),
        out_shape=jax.ShapeDtypeStruct((DEPTH, 8, 6 * D_MODEL), F32),
        compiler_params=_params("arbitrary", "arbitrary"),
        name="ada_mod",
    )(cc, ada_w, ada_b.reshape(DEPTH, 1, 6 * D_MODEL))


MLSTM_ACT = MLSTM_MAIN - MLSTM_QK
Q_TILE = MLSTM_ACT // TN - 1


def _inproj_mlstm_kernel(x_ref, g_ref, shift_ref, scale_ref, w_ref, wkt_ref, wg_ref,
                         out_ref, kt_ref, gate_ref, h_ref):
    j = pl.program_id(1)

    @pl.when(j == 0)
    def _():
        h_ref[...] = _norm_modulate(x_ref[...], g_ref[...], shift_ref[...], scale_ref[...]).astype(BF16)
        acc = jnp.dot(h_ref[...], w_ref[...], preferred_element_type=F32)
        out_ref[...] = (acc * MLSTM_DK ** -0.5).astype(BF16)
        gate_ref[...] = jnp.dot(h_ref[...], wg_ref[...], preferred_element_type=F32)

    @pl.when(j == 1)
    def _():
        kt_ref[...] = lax.dot_general(wkt_ref[...], h_ref[...], NT_DIMS,
                                      preferred_element_type=F32).astype(BF16)

    @pl.when(j > 1)
    def _():
        out_ref[...] = jnp.dot(h_ref[...], w_ref[...], preferred_element_type=F32).astype(BF16)


def _inproj_mlstm(xs, g, mods, layer, w_all, wkt_all, w_layer, w_gate):
    assert TN == MLSTM_QK
    tile = _early_tile(NT // TM, 3)
    return pl.pallas_call(
        _inproj_mlstm_kernel,
        grid=(NT // TM, MLSTM_MAIN // TN),
        in_specs=[
            pl.BlockSpec((TM, D_MODEL), lambda i, j: (tile(i, j), 0)),
            pl.BlockSpec((1, D_MODEL), lambda i, j: (0, 0)),
            _mod_spec(layer, 0, tile=tile),
            _mod_spec(layer, 1, tile=tile),
            pl.BlockSpec((None, D_MODEL, TN), lambda i, j: (w_layer, 0, jnp.where(j == 1, 0, j))),
            pl.BlockSpec((None, MLSTM_QK, D_MODEL), lambda i, j: (w_layer, 0, 0),
                         pipeline_mode=pl.Buffered(1)),
            pl.BlockSpec((D_MODEL, LANES), lambda i, j: (0, 0)),
        ],
        out_specs=[
            pl.BlockSpec((TM, TN), lambda i, j: (i, jnp.where(j <= 1, Q_TILE, j - 2))),
            pl.BlockSpec((MLSTM_QK, TM), lambda i, j: (0, i)),
            pl.BlockSpec((TM, LANES), lambda i, j: (i, 0)),
        ],
        out_shape=[
            jax.ShapeDtypeStruct((NT, MLSTM_ACT), BF16),
            jax.ShapeDtypeStruct((MLSTM_QK, NT), BF16),
            jax.ShapeDtypeStruct((NT, LANES), F32),
        ],
        scratch_shapes=[pltpu.VMEM((TM, D_MODEL), BF16)],
        compiler_params=_params("arbitrary", "arbitrary"),
        name="inproj_mlstm",
    )(xs, g, mods, mods, w_all, wkt_all, w_gate)


def _mlstm_scan_kernel(*refs, reverse, final):
    if final:
        (q_ref, kt_ref, v_ref, gate_ref, gb_ref, o_ref, hf_ref, hg_ref,
         out_ref, c_ref, n_ref, m_ref) = refs
    else:
        q_ref, kt_ref, v_ref, gate_ref, gb_ref, out_ref, c_ref, n_ref, m_ref = refs
    L = CHUNK

    @pl.when(pl.program_id(1) == 0)
    def _():
        c_ref[...] = jnp.zeros_like(c_ref)
        n_ref[...] = jnp.zeros_like(n_ref)
        m_ref[...] = jnp.zeros_like(m_ref)

    gg = GATE_CAP * jnp.tanh((gate_ref[...] + gb_ref[...]) / GATE_CAP)
    lsig = jax.nn.log_sigmoid(gg)
    t_idx = lax.broadcasted_iota(jnp.int32, (L, L), 0)
    s_idx = lax.broadcasted_iota(jnp.int32, (L, L), 1)
    mask = (s_idx >= t_idx) if reverse else (s_idx <= t_idx)
    tri = mask.astype(BF16)
    hi = lsig.astype(BF16)
    rest = lsig - hi.astype(F32)
    mid = rest.astype(BF16)
    lo = (rest - mid.astype(F32)).astype(BF16)
    bcum = (jnp.dot(tri, hi, preferred_element_type=F32) + jnp.dot(tri, mid, preferred_element_type=F32)
            + jnp.dot(tri, lo, preferred_element_type=F32))
    bal = pltpu.roll(bcum, LANES - MLSTM_HEADS, 1)
    r_t = (gg - bal).T
    base = 2 * MLSTM_HEADS if reverse else 0
    last = 0 if reverse else L - 1

    for h in range(MLSTM_HEADS):
        col = base + h
        b_rep = jnp.broadcast_to(bal[:, col:col + 1], (L, LANES))
        r_row = r_t[col:col + 1, :]
        m = m_ref[h]
        m11 = m[:, 0:1]
        b_last = b_rep[last:last + 1, 0:1]

        dmat = jnp.where(mask, _lane_repeat(b_rep, L // LANES, 1) + r_row, -jnp.inf)
        inter = b_rep + m
        m_t = jnp.maximum(inter, jnp.max(dmat, axis=-1, keepdims=True))
        w_intra = jnp.exp(dmat - _lane_repeat(m_t, L // LANES, 1))
        w_inter = jnp.exp(inter - m_t)

        qh = q_ref[:, h * MLSTM_DK:(h + 1) * MLSTM_DK]
        kth = kt_ref[h * MLSTM_DK:(h + 1) * MLSTM_DK, :]
        vh = v_ref[:, h * MLSTM_DV:(h + 1) * MLSTM_DV]
        s = jnp.dot(qh, kth, preferred_element_type=F32) * w_intra
        c_old = c_ref[h]
        num = (_lane_repeat(w_inter, MLSTM_DV // LANES, 1)
               * jnp.dot(qh, c_old.astype(BF16), preferred_element_type=F32)
               + jnp.dot(s.astype(BF16), vh, preferred_element_type=F32))
        n_old = n_ref[h]
        qn = jnp.dot(qh, n_old.astype(BF16), preferred_element_type=F32)
        den = w_inter * qn + jnp.sum(s, axis=-1, keepdims=True)
        inv = 1.0 / jnp.maximum(jnp.abs(den), jnp.exp(-m_t))
        hout = num * _lane_repeat(inv, MLSTM_DV // LANES, 1)

        a_row = b_last + r_row
        m_new = jnp.maximum(b_last + m11, jnp.max(a_row, axis=-1, keepdims=True))
        wk = jnp.exp(a_row - m_new)
        dec = jnp.exp(b_last + m11 - m_new)
        kw_t = kth.astype(F32) * wk
        c_ref[h] = dec * c_old + jnp.dot(kw_t.astype(BF16), vh, preferred_element_type=F32)
        n_ref[h] = dec * n_old + jnp.sum(kw_t, axis=-1, keepdims=True)
        m_ref[h] = jnp.broadcast_to(m_new, (1, LANES))

        sl = slice(h * MLSTM_DV, (h + 1) * MLSTM_DV)
        if final:
            hs = hf_ref[:, sl] + hout
            y = hs * lax.rsqrt(jnp.mean(hs * hs, axis=-1, keepdims=True) + EPS) * hg_ref[:, sl]
            out_ref[:, sl] = (y * jax.nn.sigmoid(o_ref[:, sl].astype(F32))).astype(BF16)
        else:
            out_ref[:, sl] = hout


def _mlstm_scan(act, k_t, gates, gate_b, head_g, hf, *, reverse):
    final = hf is not None
    n_lat_chunks = SEQ // CHUNK
    ctx_blk0 = N_LAT // CHUNK

    def row_blk(b, c):
        j = (n_lat_chunks - c) if reverse else (c - 1)
        return jnp.where(c == 0, ctx_blk0 + b, b * n_lat_chunks + j)

    in_specs = [
        pl.BlockSpec((CHUNK, MLSTM_QK), lambda b, c: (row_blk(b, c), Q_TILE)),
        pl.BlockSpec((MLSTM_QK, CHUNK), lambda b, c: (0, row_blk(b, c))),
        pl.BlockSpec((CHUNK, D_MODEL), lambda b, c: (row_blk(b, c), 0)),
        pl.BlockSpec((CHUNK, LANES), lambda b, c: (row_blk(b, c), 0)),
        pl.BlockSpec((1, LANES), lambda b, c: (0, 0)),
    ]
    args = [act, k_t, act, gates, gate_b]
    if final:
        in_specs += [
            pl.BlockSpec((CHUNK, D_MODEL), lambda b, c: (row_blk(b, c), 1)),
            pl.BlockSpec((CHUNK, D_MODEL), lambda b, c: (row_blk(b, c), 0)),
            pl.BlockSpec((1, D_MODEL), lambda b, c: (0, 0)),
        ]
        args += [act, hf, head_g]
    return pl.pallas_call(
        functools.partial(_mlstm_scan_kernel, reverse=reverse, final=final),
        grid=(BATCH, 1 + n_lat_chunks),
        in_specs=in_specs,
        out_specs=pl.BlockSpec((CHUNK, D_MODEL), lambda b, c: (row_blk(b, c), 0)),
        out_shape=jax.ShapeDtypeStruct((NT, D_MODEL), BF16 if final else F32),
        scratch_shapes=[
            pltpu.VMEM((MLSTM_HEADS, MLSTM_DK, MLSTM_DV), F32),
            pltpu.VMEM((MLSTM_HEADS, MLSTM_DK, LANES), F32),
            pltpu.VMEM((MLSTM_HEADS, 1, LANES), F32),
        ],
        compiler_params=_params("arbitrary", "arbitrary"),
        name="mlstm_scan_bwd" if reverse else "mlstm_scan_fwd",
    )(*args)


def _inproj_diff_kernel(x_ref, g_ref, shift_ref, scale_ref, w_ref, qkg_ref, cos_ref, sin_ref,
                        out_ref, h_ref, acc0_ref, acc1_ref):
    j = pl.program_id(1)
    n_qk_tiles = 2 * D_MODEL // TN
    n_tiles = 3 * D_MODEL // TN

    def matmul_into(acc_ref):
        acc_ref[...] = jnp.dot(h_ref[...], w_ref[...], preferred_element_type=F32)

    def norm_rope_from(acc_ref):
        cos = cos_ref[...]
        sin = sin_ref[...]
        for grp in range(TN // LANES):
            sl = slice(grp * LANES, (grp + 1) * LANES)
            xg = acc_ref[:, sl]
            y = xg * lax.rsqrt(jnp.mean(xg * xg, axis=-1, keepdims=True) + EPS) * qkg_ref[:, sl]
            out_ref[:, sl] = (y * cos + pltpu.roll(y, LANES // 2, 1) * sin).astype(BF16)

    def cast_from(acc_ref):
        out_ref[...] = acc_ref[...].astype(BF16)

    @pl.when(j == 0)
    def _():
        h_ref[...] = _norm_modulate(x_ref[...], g_ref[...], shift_ref[...], scale_ref[...]).astype(BF16)
        matmul_into(acc0_ref)

    is_qk_finish = jnp.logical_and(j >= 1, j <= n_qk_tiles)

    @pl.when(jnp.logical_and(is_qk_finish, j % 2 == 1))
    def _():
        matmul_into(acc1_ref)
        norm_rope_from(acc0_ref)

    @pl.when(jnp.logical_and(is_qk_finish, j % 2 == 0))
    def _():
        matmul_into(acc0_ref)
        norm_rope_from(acc1_ref)

    assert n_qk_tiles % 2 == 0 and n_tiles == n_qk_tiles + 2

    @pl.when(j == n_tiles - 1)
    def _():
        matmul_into(acc1_ref)
        cast_from(acc0_ref)

    @pl.when(j == n_tiles)
    def _():
        cast_from(acc1_ref)


def _inproj_diff(xs, g, mods, layer, w_all, w_layer, qkg, cos_t, sin_t):
    n_qk_tiles = 2 * D_MODEL // TN
    n_tiles = 3 * D_MODEL // TN
    lat_tiles = SEQ // TM

    tile = _early_tile(NT // TM, 3)
    rope_tile = _early_tile(NT // TM, n_qk_tiles + 1)

    def rope_blk(i, j):
        t = rope_tile(i, j)
        return (jnp.where(t < N_LAT // TM, t % lat_tiles, lat_tiles), 0)

    return pl.pallas_call(
        _inproj_diff_kernel,
        grid=(NT // TM, n_tiles + 1),
        in_specs=[
            pl.BlockSpec((TM, D_MODEL), lambda i, j: (tile(i, j), 0)),
            pl.BlockSpec((1, D_MODEL), lambda i, j: (0, 0)),
            _mod_spec(layer, 0, tile=tile),
            _mod_spec(layer, 1, tile=tile),
            pl.BlockSpec((None, D_MODEL, TN), lambda i, j: (w_layer, 0, jnp.where(j == n_tiles, 0, j))),
            pl.BlockSpec((1, TN), lambda i, j: (0, jnp.clip(j - 1, 0, n_qk_tiles - 1))),
            pl.BlockSpec((TM, LANES), rope_blk),
            pl.BlockSpec((TM, LANES), rope_blk),
        ],
        out_specs=pl.BlockSpec((TM, TN), lambda i, j: (i, jnp.maximum(j - 1, 0))),
        out_shape=jax.ShapeDtypeStruct((NT, 3 * D_MODEL), BF16),
        scratch_shapes=[
            pltpu.VMEM((TM, D_MODEL), BF16),
            pltpu.VMEM((TM, TN), F32),
            pltpu.VMEM((TM, TN), F32),
        ],
        compiler_params=_params("arbitrary", "arbitrary"),
        name="inproj_diff",
    )(xs, g, mods, mods, w_all, qkg, cos_t, sin_t)


def _attn_kernel(*refs, n_lat_chunks, lam_init):
    if n_lat_chunks:
        (q_ref, kl_ref, vl_ref, kc_ref, vc_ref, lq1_ref, lk1_ref, lq2_ref, lk2_ref, sg_ref,
         out_ref, acc_ref, m_ref, l_ref) = refs
    else:
        (q_ref, kc_ref, vc_ref, lq1_ref, lk1_ref, lq2_ref, lk2_ref, sg_ref, _,
         out_ref, acc_ref, m_ref, l_ref) = refs
    tq = q_ref.shape[0]
    dv = 2 * DIFF_DH
    q = q_ref[...]

    def scores(kblk):
        return jnp.concatenate(
            [lax.dot_general(q[:, t * DIFF_DH:(t + 1) * DIFF_DH], kblk[:, t * DIFF_DH:(t + 1) * DIFF_DH],
                             NT_DIMS, preferred_element_type=F32) for t in range(2)], axis=0)

    s = scores(kc_ref[...])
    m0 = jnp.max(s, axis=-1, keepdims=True)
    p = jnp.exp2(s - m0)
    m_ref[...] = jnp.broadcast_to(m0, m_ref.shape)
    l_ref[...] = jnp.broadcast_to(jnp.sum(p, axis=-1, keepdims=True), l_ref.shape)
    acc_ref[...] = jnp.dot(p.astype(BF16), vc_ref[...], preferred_element_type=F32)

    if n_lat_chunks:
        def body(c, carry):
            off = pl.multiple_of(c * TK, TK)
            s = scores(kl_ref[pl.ds(off, TK), :])
            m_old = m_ref[...]
            m_new = jnp.maximum(m_old, jnp.max(s, axis=-1, keepdims=True))
            p = jnp.exp2(s - _lane_repeat(m_new, TK // LANES, 1))
            alpha = jnp.exp2(m_old - m_new)
            l_ref[...] = alpha * l_ref[...] + jnp.sum(p, axis=-1, keepdims=True)
            acc_ref[...] = (_lane_repeat(alpha, dv // LANES, 1) * acc_ref[...]
                            + jnp.dot(p.astype(BF16), vl_ref[pl.ds(off, TK), :],
                                      preferred_element_type=F32))
            m_ref[...] = m_new
            return carry
        lax.fori_loop(0, n_lat_chunks, body, 0)

    lam = (jnp.exp(jnp.sum(lq1_ref[...] * lk1_ref[...], axis=-1, keepdims=True))
           - jnp.exp(jnp.sum(lq2_ref[...] * lk2_ref[...], axis=-1, keepdims=True)) + lam_init)
    o_all = acc_ref[...] * _lane_repeat(1.0 / l_ref[...], dv // LANES, 1)
    o = o_all[:tq] - lam * o_all[tq:]
    y = o * lax.rsqrt(jnp.mean(o * o, axis=-1, keepdims=True) + EPS) * sg_ref[...]
    out_ref[...] = (y * (1.0 - lam_init)).astype(BF16)


def _attention(qkv, lams, subln_g, lam_init, prev_out):
    dv = 2 * DIFF_DH
    kcol = D_MODEL // dv
    vcol = 2 * D_MODEL // dv
    ctx_blk0 = N_LAT // CTX_LEN
    small = [pl.BlockSpec((1, DIFF_DH), lambda b, h, i: (0, 0))] * 4
    small.append(pl.BlockSpec((1, dv), lambda b, h, i: (0, 0)))
    ctx_specs = [
        pl.BlockSpec((CTX_LEN, dv), lambda b, h, i: (ctx_blk0 + b, kcol + h)),
        pl.BlockSpec((CTX_LEN, dv), lambda b, h, i: (ctx_blk0 + b, vcol + h)),
    ]
    if prev_out is None:
        tq, nq = TQ, SEQ // TQ
        q_map = lambda b, h, i: (b * nq + i, h)
        in_specs = [
            pl.BlockSpec((tq, dv), q_map),
            pl.BlockSpec((SEQ, dv), lambda b, h, i: (b, kcol + h)),
            pl.BlockSpec((SEQ, dv), lambda b, h, i: (b, vcol + h)),
        ] + ctx_specs + small
        args = [qkv, qkv, qkv, qkv, qkv] + list(lams) + [subln_g]
        aliases = {}
        n_lat_chunks = SEQ // TK
    else:
        tq, nq = CTX_LEN, 1
        q_map = lambda b, h, i: (ctx_blk0 + b, h)
        in_specs = [pl.BlockSpec((tq, dv), q_map)] + ctx_specs + small
        in_specs.append(pl.BlockSpec(memory_space=pl.ANY))
        args = [qkv, qkv, qkv] + list(lams) + [subln_g, prev_out]
        aliases = {len(args) - 1: 0}
        n_lat_chunks = 0
    return pl.pallas_call(
        functools.partial(_attn_kernel, n_lat_chunks=n_lat_chunks, lam_init=lam_init),
        grid=(BATCH, DIFF_HEADS, nq),
        in_specs=in_specs,
        out_specs=pl.BlockSpec((tq, dv), q_map),
        out_shape=jax.ShapeDtypeStruct((NT, D_MODEL), BF16),
        scratch_shapes=[
            pltpu.VMEM((2 * tq, dv), F32),
            pltpu.VMEM((2 * tq, LANES), F32),
            pltpu.VMEM((2 * tq, LANES), F32),
        ],
        input_output_aliases=aliases,
        compiler_params=_params("arbitrary", "arbitrary", "arbitrary"),
        name="diff_attn_ctx" if n_lat_chunks == 0 else "diff_attn",
    )(*args)


def _ffn_kernel(a_ref, wo_ref, x_ref, g_ref, gate1_ref, shift_ref, scale_ref, gate2_ref,
                wg_ref, wu_ref, wd_ref, out_ref, h_ref):
    k = pl.program_id(1)

    def hidden_tile(h, base):
        gt = jnp.dot(h, wg_ref[...], preferred_element_type=F32)
        up = jnp.dot(h, wu_ref[...], preferred_element_type=F32)
        act = (gt * jax.nn.sigmoid(gt) * up).astype(BF16)
        out_ref[...] = base + gate2_ref[...] * jnp.dot(act, wd_ref[...], preferred_element_type=F32)

    @pl.when(k == 0)
    def _():
        x1 = x_ref[...] + gate1_ref[...] * jnp.dot(a_ref[...], wo_ref[...], preferred_element_type=F32)
        h = _norm_modulate(x1, g_ref[...], shift_ref[...], scale_ref[...]).astype(BF16)
        h_ref[...] = h
        hidden_tile(h, x1)

    @pl.when(k > 0)
    def _():
        hidden_tile(h_ref[...], out_ref[...])


def _ffn(a, w_out, w_out_layer, xs, g, mods, layer, w_gu, w_down, n_tiles):
    nh = FFN_HIDDEN // TH
    tile = _early_tile(n_tiles, nh // 2)
    return pl.pallas_call(
        _ffn_kernel,
        grid=(n_tiles, nh),
        in_specs=[
            pl.BlockSpec((TMF, D_MODEL), lambda i, k: (tile(i, k), 0)),
            pl.BlockSpec((None, D_MODEL, D_MODEL), lambda i, k: (w_out_layer, 0, 0),
                         pipeline_mode=pl.Buffered(1)),
            pl.BlockSpec((TMF, D_MODEL), lambda i, k: (tile(i, k), 0)),
            pl.BlockSpec((1, D_MODEL), lambda i, k: (0, 0)),
            _mod_spec(layer, 2, TMF, tile),
            _mod_spec(layer, 3, TMF, tile),
            _mod_spec(layer, 4, TMF, tile),
            _mod_spec(layer, 5, TMF),
            pl.BlockSpec((None, D_MODEL, TH), lambda i, k: (layer, 0, k)),
            pl.BlockSpec((None, D_MODEL, TH), lambda i, k: (layer, 0, nh + k)),
            pl.BlockSpec((None, TH, D_MODEL), lambda i, k: (layer, k, 0)),
        ],
        out_specs=pl.BlockSpec((TMF, D_MODEL), lambda i, k: (i, 0)),
        out_shape=jax.ShapeDtypeStruct((n_tiles * TMF, D_MODEL), F32),
        scratch_shapes=[pltpu.VMEM((TMF, D_MODEL), BF16)],
        compiler_params=_params("arbitrary", "arbitrary"),
        name="ffn",
    )(a, w_out, xs, g, mods, mods, mods, mods, w_gu, w_gu, w_down)


def _pair_major(a):
    n_freq = DIFF_DH // 4
    lead = a.shape[:-1]
    a = a.reshape(*lead, a.shape[-1] // DIFF_DH, 2, 2, n_freq)
    return jnp.swapaxes(a, -3, -2).reshape(*lead, -1)


def _rope_tables():
    n_freq = DIFF_DH // 4
    pos = jnp.arange(SEQ)
    freqs = ROPE_BASE ** (-jnp.arange(n_freq, dtype=F32) / n_freq)
    ang = jnp.stack([pos // GRID_W, pos % GRID_W], axis=-1).astype(F32)[:, :, None] * freqs
    cos, sin = jnp.cos(ang), jnp.sin(ang)
    cos_t = jnp.concatenate([cos[:, 0], cos[:, 1], cos[:, 0], cos[:, 1]], axis=-1)
    sin_t = jnp.concatenate([-sin[:, 0], -sin[:, 1], sin[:, 0], sin[:, 1]], axis=-1)
    cos_t = jnp.concatenate([cos_t, jnp.ones((TM, LANES), F32)], axis=0)
    sin_t = jnp.concatenate([sin_t, jnp.zeros((TM, LANES), F32)], axis=0)
    return cos_t, sin_t


def kernel(x, c, ctx, c_ctx, ada_w, ada_b, norm_g, mlstm_w_in, mlstm_gate_b, mlstm_head_g, mlstm_w_out,
           diff_w_in, diff_w_out, diff_q_g, diff_k_g, diff_lq1, diff_lk1, diff_lq2, diff_lk2, diff_subln_g,
           ffn_w_gu, ffn_w_down):
    assert x.shape == (BATCH, SEQ, D_MODEL) and ctx.shape == (BATCH, CTX_LEN, D_MODEL)
    xs = jnp.concatenate([x.reshape(N_LAT, D_MODEL), ctx.reshape(N_CTX, D_MODEL)], axis=0)
    cc = jnp.concatenate([c, c_ctx[None], jnp.zeros((8 - BATCH - 1, D_MODEL), F32)], axis=0)
    mods = _ada(cc, ada_w, ada_b).reshape(DEPTH, 8, 1, 6 * D_MODEL)
    cos_t, sin_t = _rope_tables()
    row = lambda v: v.reshape(1, -1)
    ffn_gu = ffn_w_gu.astype(BF16)
    ffn_down = ffn_w_down.astype(BF16)
    mlstm_in = mlstm_w_in.astype(BF16)
    mlstm_kt = jnp.swapaxes(mlstm_w_in[:, :, MLSTM_QK:2 * MLSTM_QK], 1, 2).astype(BF16)
    mlstm_out = mlstm_w_out.astype(BF16)
    diff_in = jnp.concatenate([_pair_major(diff_w_in[..., :2 * D_MODEL]), diff_w_in[..., 2 * D_MODEL:]],
                              axis=-1).astype(BF16)
    diff_out = diff_w_out.astype(BF16)

    for i in range(DEPTH):
        last = i == DEPTH - 1
        n_rows = N_LAT if last else NT
        j = i // N_MIXERS
        if i % N_MIXERS == 0:
            w_gate = jnp.pad(mlstm_w_in[j, :, MLSTM_MAIN:], ((0, 0), (0, LANES - N_GATES))).astype(BF16)
            gate_b = jnp.pad(mlstm_gate_b[j], (0, LANES - N_GATES)).reshape(1, LANES)
            act, k_t, gates = _inproj_mlstm(xs, row(norm_g[i, 0]), mods, i, mlstm_in, mlstm_kt, j, w_gate)
            hf = _mlstm_scan(act, k_t, gates, gate_b, None, None, reverse=False)
            a = _mlstm_scan(act, k_t, gates, gate_b, row(mlstm_head_g[j]), hf, reverse=True)
            w_out = mlstm_out
        else:
            lam_init = 0.8 - 0.6 * math.exp(-0.3 * i)
            q_gain = _pair_major(diff_q_g[j]) * (DIFF_DH ** -0.5 * math.log2(math.e))
            qkg = jnp.concatenate([jnp.tile(q_gain, 2 * DIFF_HEADS),
                                   jnp.tile(_pair_major(diff_k_g[j]), 2 * DIFF_HEADS)]).reshape(1, 2 * D_MODEL)
            qkv = _inproj_diff(xs, row(norm_g[i, 0]), mods, i, diff_in, j, qkg, cos_t, sin_t)
            lams = [row(diff_lq1[j]), row(diff_lk1[j]), row(diff_lq2[j]), row(diff_lk2[j])]
            a = _attention(qkv, lams, row(diff_subln_g[j]), lam_init, None)
            if not last:
                a = _attention(qkv, lams, row(diff_subln_g[j]), lam_init, a)
            w_out = diff_out
        xs = _ffn(a, w_out, j, xs, row(norm_g[i, 1]), mods, i, ffn_gu, ffn_down, n_rows // TMF)
    return xs.reshape(BATCH, SEQ, D_MODEL)
```

```python
import functools
import math

import jax
import jax.numpy as jnp
from jax import lax
from jax.experimental import pallas as pl
from jax.experimental.pallas import tpu as pltpu

F32 = jnp.float32
BF16 = jnp.bfloat16

D_MODEL = 2048
BATCH = 4
SEQ = 4096
DEPTH = 4
GRID_W = 64
CTX_LEN = 256
N_MIXERS = 2

MLSTM_HEADS = 8
MLSTM_DK = D_MODEL // (2 * MLSTM_HEADS)
MLSTM_DV = D_MODEL // MLSTM_HEADS
GATE_CAP = 15.0
MLSTM_QK = MLSTM_HEADS * MLSTM_DK
MLSTM_MAIN = 2 * MLSTM_QK + 2 * D_MODEL
N_GATES = 4 * MLSTM_HEADS

DIFF_HEADS = 8
DIFF_DH = D_MODEL // (2 * DIFF_HEADS)
ROPE_BASE = 10000.0

FFN_HIDDEN = -(-(8 * D_MODEL) // (3 * 256)) * 256
EPS = 1e-6

N_LAT = BATCH * SEQ
N_CTX = BATCH * CTX_LEN
NT = N_LAT + N_CTX

LANES = 128
TM = 1024
TN = 1024
TMF = 512
TH = 512
CHUNK = 256
TQ = 1024
TK = 1024
VMEM_LIMIT = 56 * 1024 * 1024

NT_DIMS = (((1,), (1,)), ((), ()))
TN_DIMS = (((0,), (0,)), ((), ()))


def _params(*sem):
    return pltpu.CompilerParams(dimension_semantics=sem, vmem_limit_bytes=VMEM_LIMIT)


def _mod_row(i, tm):
    return jnp.minimum(i // (SEQ // tm), BATCH)


def _mod_spec(layer, k, tm=TM, tile=lambda i, j: i):
    return pl.BlockSpec((None, None, 1, D_MODEL), lambda i, j: (layer, _mod_row(tile(i, j), tm), 0, k))


def _early_tile(n_tiles, from_step):
    return lambda i, j: jnp.minimum(i + jnp.where(j >= from_step, 1, 0), n_tiles - 1)


def _weight_spec(layer, tn):
    return pl.BlockSpec((None, D_MODEL, tn), lambda i, j: (layer, 0, j))


def _lane_repeat(x, n, axis):
    assert axis == 1
    return jnp.concatenate([x] * n, axis=1) if n > 1 else x


def _norm_modulate(x, g, shift, scale):
    y = x * lax.rsqrt(jnp.mean(x * x, axis=-1, keepdims=True) + EPS) * g
    return y * (1.0 + scale) + shift


def _ada_kernel(c_ref, w_ref, b_ref, o_ref):
    c = c_ref[...]
    a = (c * jax.nn.sigmoid(c)).astype(BF16)
    o_ref[...] = jnp.dot(a, w_ref[...].astype(BF16), preferred_element_type=F32) + b_ref[...]


def _ada(cc, ada_w, ada_b):
    tn = 1024
    return pl.pallas_call(
        _ada_kernel,
        grid=(DEPTH, 6 * D_MODEL // tn),
        in_specs=[
            pl.BlockSpec((8, D_MODEL), lambda l, j: (0, 0)),
            pl.BlockSpec((None, D_MODEL, tn), lambda l, j: (l, 0, j)),
            pl.BlockSpec((None, 1, tn), lambda l, j: (l, 0, j)),
        ],
        out_specs=pl.BlockSpec((None, 8, tn), lambda l, j: (l, 0, j)),
        out_shape=jax.ShapeDtypeStruct((DEPTH, 8, 6 * D_MODEL), F32),
        compiler_params=_params("arbitrary", "arbitrary"),
        name="ada_mod",
    )(cc, ada_w, ada_b.reshape(DEPTH, 1, 6 * D_MODEL))


MLSTM_ACT = MLSTM_MAIN - MLSTM_QK
Q_TILE = MLSTM_ACT // TN - 1


def _inproj_mlstm_kernel(x_ref, g_ref, shift_ref, scale_ref, w_ref, wkt_ref, wg_ref,
                         out_ref, kt_ref, gate_ref, h_ref):
    j = pl.program_id(1)

    @pl.when(j == 0)
    def _():
        h_ref[...] = _norm_modulate(x_ref[...], g_ref[...], shift_ref[...], scale_ref[...]).astype(BF16)
        acc = jnp.dot(h_ref[...], w_ref[...], preferred_element_type=F32)
        out_ref[...] = (acc * MLSTM_DK ** -0.5).astype(BF16)
        gate_ref[...] = jnp.dot(h_ref[...], wg_ref[...], preferred_element_type=F32)

    @pl.when(j == 1)
    def _():
        kt_ref[...] = lax.dot_general(wkt_ref[...], h_ref[...], NT_DIMS,
                                      preferred_element_type=F32).astype(BF16)

    @pl.when(j > 1)
    def _():
        out_ref[...] = jnp.dot(h_ref[...], w_ref[...], preferred_element_type=F32).astype(BF16)


def _inproj_mlstm(xs, g, mods, layer, w_all, wkt_all, w_layer, w_gate):
    assert TN == MLSTM_QK
    tile = _early_tile(NT // TM, 3)
    return pl.pallas_call(
        _inproj_mlstm_kernel,
        grid=(NT // TM, MLSTM_MAIN // TN),
        in_specs=[
            pl.BlockSpec((TM, D_MODEL), lambda i, j: (tile(i, j), 0)),
            pl.BlockSpec((1, D_MODEL), lambda i, j: (0, 0)),
            _mod_spec(layer, 0, tile=tile),
            _mod_spec(layer, 1, tile=tile),
            pl.BlockSpec((None, D_MODEL, TN), lambda i, j: (w_layer, 0, jnp.where(j == 1, 0, j))),
            pl.BlockSpec((None, MLSTM_QK, D_MODEL), lambda i, j: (w_layer, 0, 0),
                         pipeline_mode=pl.Buffered(1)),
            pl.BlockSpec((D_MODEL, LANES), lambda i, j: (0, 0)),
        ],
        out_specs=[
            pl.BlockSpec((TM, TN), lambda i, j: (i, jnp.where(j <= 1, Q_TILE, j - 2))),
            pl.BlockSpec((MLSTM_QK, TM), lambda i, j: (0, i)),
            pl.BlockSpec((TM, LANES), lambda i, j: (i, 0)),
        ],
        out_shape=[
            jax.ShapeDtypeStruct((NT, MLSTM_ACT), BF16),
            jax.ShapeDtypeStruct((MLSTM_QK, NT), BF16),
            jax.ShapeDtypeStruct((NT, LANES), F32),
        ],
        scratch_shapes=[pltpu.VMEM((TM, D_MODEL), BF16)],
        compiler_params=_params("arbitrary", "arbitrary"),
        name="inproj_mlstm",
    )(xs, g, mods, mods, w_all, wkt_all, w_gate)


def _mlstm_scan_kernel(*refs, reverse, final):
    if final:
        (q_ref, kt_ref, v_ref, gate_ref, gb_ref, o_ref, hf_ref, hg_ref,
         out_ref, c_ref, n_ref, m_ref) = refs
    else:
        q_ref, kt_ref, v_ref, gate_ref, gb_ref, out_ref, c_ref, n_ref, m_ref = refs
    L = CHUNK

    @pl.when(pl.program_id(1) == 0)
    def _():
        c_ref[...] = jnp.zeros_like(c_ref)
        n_ref[...] = jnp.zeros_like(n_ref)
        m_ref[...] = jnp.zeros_like(m_ref)

    gg = GATE_CAP * jnp.tanh((gate_ref[...] + gb_ref[...]) / GATE_CAP)
    lsig = jax.nn.log_sigmoid(gg)
    t_idx = lax.broadcasted_iota(jnp.int32, (L, L), 0)
    s_idx = lax.broadcasted_iota(jnp.int32, (L, L), 1)
    mask = (s_idx >= t_idx) if reverse else (s_idx <= t_idx)
    tri = mask.astype(BF16)
    hi = lsig.astype(BF16)
    rest = lsig - hi.astype(F32)
    mid = rest.astype(BF16)
    lo = (rest - mid.astype(F32)).astype(BF16)
    bcum = (jnp.dot(tri, hi, preferred_element_type=F32) + jnp.dot(tri, mid, preferred_element_type=F32)
            + jnp.dot(tri, lo, preferred_element_type=F32))
    bal = pltpu.roll(bcum, LANES - MLSTM_HEADS, 1)
    r_t = (gg - bal).T
    base = 2 * MLSTM_HEADS if reverse else 0
    last = 0 if reverse else L - 1

    for h in range(MLSTM_HEADS):
        col = base + h
        b_rep = jnp.broadcast_to(bal[:, col:col + 1], (L, LANES))
        r_row = r_t[col:col + 1, :]
        m = m_ref[h]
        m11 = m[:, 0:1]
        b_last = b_rep[last:last + 1, 0:1]

        dmat = jnp.where(mask, _lane_repeat(b_rep, L // LANES, 1) + r_row, -jnp.inf)
        inter = b_rep + m
        m_t = jnp.maximum(inter, jnp.max(dmat, axis=-1, keepdims=True))
        w_intra = jnp.exp(dmat - _lane_repeat(m_t, L // LANES, 1))
        w_inter = jnp.exp(inter - m_t)

        qh = q_ref[:, h * MLSTM_DK:(h + 1) * MLSTM_DK]
        kth = kt_ref[h * MLSTM_DK:(h + 1) * MLSTM_DK, :]
        vh = v_ref[:, h * MLSTM_DV:(h + 1) * MLSTM_DV]
        s = jnp.dot(qh, kth, preferred_element_type=F32) * w_intra
        c_old = c_ref[h]
        num = (_lane_repeat(w_inter, MLSTM_DV // LANES, 1)
               * jnp.dot(qh, c_old.astype(BF16), preferred_element_type=F32)
               + jnp.dot(s.astype(BF16), vh, preferred_element_type=F32))
        n_old = n_ref[h]
        qn = jnp.dot(qh, n_old.astype(BF16), preferred_element_type=F32)
        den = w_inter * qn + jnp.sum(s, axis=-1, keepdims=True)
        inv = 1.0 / jnp.maximum(jnp.abs(den), jnp.exp(-m_t))
        hout = num * _lane_repeat(inv, MLSTM_DV // LANES, 1)

        a_row = b_last + r_row
        m_new = jnp.maximum(b_last + m11, jnp.max(a_row, axis=-1, keepdims=True))
        wk = jnp.exp(a_row - m_new)
        dec = jnp.exp(b_last + m11 - m_new)
        kw_t = kth.astype(F32) * wk
        c_ref[h] = dec * c_old + jnp.dot(kw_t.astype(BF16), vh, preferred_element_type=F32)
        n_ref[h] = dec * n_old + jnp.sum(kw_t, axis=-1, keepdims=True)
        m_ref[h] = jnp.broadcast_to(m_new, (1, LANES))

        sl = slice(h * MLSTM_DV, (h + 1) * MLSTM_DV)
        if final:
            hs = hf_ref[:, sl] + hout
            y = hs * lax.rsqrt(jnp.mean(hs * hs, axis=-1, keepdims=True) + EPS) * hg_ref[:, sl]
            out_ref[:, sl] = (y * jax.nn.sigmoid(o_ref[:, sl].astype(F32))).astype(BF16)
        else:
            out_ref[:, sl] = hout


def _mlstm_scan(act, k_t, gates, gate_b, head_g, hf, *, reverse):
    final = hf is not None
    n_lat_chunks = SEQ // CHUNK
    ctx_blk0 = N_LAT // CHUNK

    def row_blk(b, c):
        j = (n_lat_chunks - c) if reverse else (c - 1)
        return jnp.where(c == 0, ctx_blk0 + b, b * n_lat_chunks + j)

    in_specs = [
        pl.BlockSpec((CHUNK, MLSTM_QK), lambda b, c: (row_blk(b, c), Q_TILE)),
        pl.BlockSpec((MLSTM_QK, CHUNK), lambda b, c: (0, row_blk(b, c))),
        pl.BlockSpec((CHUNK, D_MODEL), lambda b, c: (row_blk(b, c), 0)),
        pl.BlockSpec((CHUNK, LANES), lambda b, c: (row_blk(b, c), 0)),
        pl.BlockSpec((1, LANES), lambda b, c: (0, 0)),
    ]
    args = [act, k_t, act, gates, gate_b]
    if final:
        in_specs += [
            pl.BlockSpec((CHUNK, D_MODEL), lambda b, c: (row_blk(b, c), 1)),
            pl.BlockSpec((CHUNK, D_MODEL), lambda b, c: (row_blk(b, c), 0)),
            pl.BlockSpec((1, D_MODEL), lambda b, c: (0, 0)),
        ]
        args += [act, hf, head_g]
    return pl.pallas_call(
        functools.partial(_mlstm_scan_kernel, reverse=reverse, final=final),
        grid=(BATCH, 1 + n_lat_chunks),
        in_specs=in_specs,
        out_specs=pl.BlockSpec((CHUNK, D_MODEL), lambda b, c: (row_blk(b, c), 0)),
        out_shape=jax.ShapeDtypeStruct((NT, D_MODEL), BF16 if final else F32),
        scratch_shapes=[
            pltpu.VMEM((MLSTM_HEADS, MLSTM_DK, MLSTM_DV), F32),
            pltpu.VMEM((MLSTM_HEADS, MLSTM_DK, LANES), F32),
            pltpu.VMEM((MLSTM_HEADS, 1, LANES), F32),
        ],
        compiler_params=_params("arbitrary", "arbitrary"),
        name="mlstm_scan_bwd" if reverse else "mlstm_scan_fwd",
    )(*args)


def _inproj_diff_kernel(x_ref, g_ref, shift_ref, scale_ref, w_ref, qkg_ref, cos_ref, sin_ref,
                        out_ref, h_ref, acc0_ref, acc1_ref):
    j = pl.program_id(1)
    n_qk_tiles = 2 * D_MODEL // TN
    n_tiles = 3 * D_MODEL // TN

    def matmul_into(acc_ref):
        acc_ref[...] = jnp.dot(h_ref[...], w_ref[...], preferred_element_type=F32)

    def norm_rope_from(acc_ref):
        cos = cos_ref[...]
        sin = sin_ref[...]
        for grp in range(TN // LANES):
            sl = slice(grp * LANES, (grp + 1) * LANES)
            xg = acc_ref[:, sl]
            y = xg * lax.rsqrt(jnp.mean(xg * xg, axis=-1, keepdims=True) + EPS) * qkg_ref[:, sl]
            out_ref[:, sl] = (y * cos + pltpu.roll(y, LANES // 2, 1) * sin).astype(BF16)

    def cast_from(acc_ref):
        out_ref[...] = acc_ref[...].astype(BF16)

    @pl.when(j == 0)
    def _():
        h_ref[...] = _norm_modulate(x_ref[...], g_ref[...], shift_ref[...], scale_ref[...]).astype(BF16)
        matmul_into(acc0_ref)

    is_qk_finish = jnp.logical_and(j >= 1, j <= n_qk_tiles)

    @pl.when(jnp.logical_and(is_qk_finish, j % 2 == 1))
    def _():
        matmul_into(acc1_ref)
        norm_rope_from(acc0_ref)

    @pl.when(jnp.logical_and(is_qk_finish, j % 2 == 0))
    def _():
        matmul_into(acc0_ref)
        norm_rope_from(acc1_ref)

    assert n_qk_tiles % 2 == 0 and n_tiles == n_qk_tiles + 2

    @pl.when(j == n_tiles - 1)
    def _():
        matmul_into(acc1_ref)
        cast_from(acc0_ref)

    @pl.when(j == n_tiles)
    def _():
        cast_from(acc1_ref)


def _inproj_diff(xs, g, mods, layer, w_all, w_layer, qkg, cos_t, sin_t):
    n_qk_tiles = 2 * D_MODEL // TN
    n_tiles = 3 * D_MODEL // TN
    lat_tiles = SEQ // TM

    tile = _early_tile(NT // TM, 3)
    rope_tile = _early_tile(NT // TM, n_qk_tiles + 1)

    def rope_blk(i, j):
        t = rope_tile(i, j)
        return (jnp.where(t < N_LAT // TM, t % lat_tiles, lat_tiles), 0)

    return pl.pallas_call(
        _inproj_diff_kernel,
        grid=(NT // TM, n_tiles + 1),
        in_specs=[
            pl.BlockSpec((TM, D_MODEL), lambda i, j: (tile(i, j), 0)),
            pl.BlockSpec((1, D_MODEL), lambda i, j: (0, 0)),
            _mod_spec(layer, 0, tile=tile),
            _mod_spec(layer, 1, tile=tile),
            pl.BlockSpec((None, D_MODEL, TN), lambda i, j: (w_layer, 0, jnp.where(j == n_tiles, 0, j))),
            pl.BlockSpec((1, TN), lambda i, j: (0, jnp.clip(j - 1, 0, n_qk_tiles - 1))),
            pl.BlockSpec((TM, LANES), rope_blk),
            pl.BlockSpec((TM, LANES), rope_blk),
        ],
        out_specs=pl.BlockSpec((TM, TN), lambda i, j: (i, jnp.maximum(j - 1, 0))),
        out_shape=jax.ShapeDtypeStruct((NT, 3 * D_MODEL), BF16),
        scratch_shapes=[
            pltpu.VMEM((TM, D_MODEL), BF16),
            pltpu.VMEM((TM, TN), F32),
            pltpu.VMEM((TM, TN), F32),
        ],
        compiler_params=_params("arbitrary", "arbitrary"),
        name="inproj_diff",
    )(xs, g, mods, mods, w_all, qkg, cos_t, sin_t)


def _attn_kernel(*refs, n_lat_chunks, lam_init):
    if n_lat_chunks:
        (q_ref, kl_ref, vl_ref, kc_ref, vc_ref, lq1_ref, lk1_ref, lq2_ref, lk2_ref, sg_ref,
         out_ref, acc_ref, m_ref, l_ref) = refs
    else:
        (q_ref, kc_ref, vc_ref, lq1_ref, lk1_ref, lq2_ref, lk2_ref, sg_ref, _,
         out_ref, acc_ref, m_ref, l_ref) = refs
    tq = q_ref.shape[0]
    dv = 2 * DIFF_DH
    q = q_ref[...]

    def scores(kblk):
        return jnp.concatenate(
            [lax.dot_general(q[:, t * DIFF_DH:(t + 1) * DIFF_DH], kblk[:, t * DIFF_DH:(t + 1) * DIFF_DH],
                             NT_DIMS, preferred_element_type=F32) for t in range(2)], axis=0)

    s = scores(kc_ref[...])
    m0 = jnp.max(s, axis=-1, keepdims=True)
    p = jnp.exp2(s - m0)
    m_ref[...] = jnp.broadcast_to(m0, m_ref.shape)
    l_ref[...] = jnp.broadcast_to(jnp.sum(p, axis=-1, keepdims=True), l_ref.shape)
    acc_ref[...] = jnp.dot(p.astype(BF16), vc_ref[...], preferred_element_type=F32)

    if n_lat_chunks:
        def body(c, carry):
            off = pl.multiple_of(c * TK, TK)
            s = scores(kl_ref[pl.ds(off, TK), :])
            m_old = m_ref[...]
            m_new = jnp.maximum(m_old, jnp.max(s, axis=-1, keepdims=True))
            p = jnp.exp2(s - _lane_repeat(m_new, TK // LANES, 1))
            alpha = jnp.exp2(m_old - m_new)
            l_ref[...] = alpha * l_ref[...] + jnp.sum(p, axis=-1, keepdims=True)
            acc_ref[...] = (_lane_repeat(alpha, dv // LANES, 1) * acc_ref[...]
                            + jnp.dot(p.astype(BF16), vl_ref[pl.ds(off, TK), :],
                                      preferred_element_type=F32))
            m_ref[...] = m_new
            return carry
        lax.fori_loop(0, n_lat_chunks, body, 0)

    lam = (jnp.exp(jnp.sum(lq1_ref[...] * lk1_ref[...], axis=-1, keepdims=True))
           - jnp.exp(jnp.sum(lq2_ref[...] * lk2_ref[...], axis=-1, keepdims=True)) + lam_init)
    o_all = acc_ref[...] * _lane_repeat(1.0 / l_ref[...], dv // LANES, 1)
    o = o_all[:tq] - lam * o_all[tq:]
    y = o * lax.rsqrt(jnp.mean(o * o, axis=-1, keepdims=True) + EPS) * sg_ref[...]
    out_ref[...] = (y * (1.0 - lam_init)).astype(BF16)


def _attention(qkv, lams, subln_g, lam_init, prev_out):
    dv = 2 * DIFF_DH
    kcol = D_MODEL // dv
    vcol = 2 * D_MODEL // dv
    ctx_blk0 = N_LAT // CTX_LEN
    small = [pl.BlockSpec((1, DIFF_DH), lambda b, h, i: (0, 0))] * 4
    small.append(pl.BlockSpec((1, dv), lambda b, h, i: (0, 0)))
    ctx_specs = [
        pl.BlockSpec((CTX_LEN, dv), lambda b, h, i: (ctx_blk0 + b, kcol + h)),
        pl.BlockSpec((CTX_LEN, dv), lambda b, h, i: (ctx_blk0 + b, vcol + h)),
    ]
    if prev_out is None:
        tq, nq = TQ, SEQ // TQ
        q_map = lambda b, h, i: (b * nq + i, h)
        in_specs = [
            pl.BlockSpec((tq, dv), q_map),
            pl.BlockSpec((SEQ, dv), lambda b, h, i: (b, kcol + h)),
            pl.BlockSpec((SEQ, dv), lambda b, h, i: (b, vcol + h)),
        ] + ctx_specs + small
        args = [qkv, qkv, qkv, qkv, qkv] + list(lams) + [subln_g]
        aliases = {}
        n_lat_chunks = SEQ // TK
    else:
        tq, nq = CTX_LEN, 1
        q_map = lambda b, h, i: (ctx_blk0 + b, h)
        in_specs = [pl.BlockSpec((tq, dv), q_map)] + ctx_specs + small
        in_specs.append(pl.BlockSpec(memory_space=pl.ANY))
        args = [qkv, qkv, qkv] + list(lams) + [subln_g, prev_out]
        aliases = {len(args) - 1: 0}
        n_lat_chunks = 0
    return pl.pallas_call(
        functools.partial(_attn_kernel, n_lat_chunks=n_lat_chunks, lam_init=lam_init),
        grid=(BATCH, DIFF_HEADS, nq),
        in_specs=in_specs,
        out_specs=pl.BlockSpec((tq, dv), q_map),
        out_shape=jax.ShapeDtypeStruct((NT, D_MODEL), BF16),
        scratch_shapes=[
            pltpu.VMEM((2 * tq, dv), F32),
            pltpu.VMEM((2 * tq, LANES), F32),
            pltpu.VMEM((2 * tq, LANES), F32),
        ],
        input_output_aliases=aliases,
        compiler_params=_params("arbitrary", "arbitrary", "arbitrary"),
        name="diff_attn_ctx" if n_lat_chunks == 0 else "diff_attn",
    )(*args)


def _ffn_kernel(a_ref, wo_ref, x_ref, g_ref, gate1_ref, shift_ref, scale_ref, gate2_ref,
                wg_ref, wu_ref, wd_ref, out_ref, h_ref):
    k = pl.program_id(1)

    def hidden_tile(h, base):
        gt = jnp.dot(h, wg_ref[...], preferred_element_type=F32)
        up = jnp.dot(h, wu_ref[...], preferred_element_type=F32)
        act = (gt * jax.nn.sigmoid(gt) * up).astype(BF16)
        out_ref[...] = base + gate2_ref[...] * jnp.dot(act, wd_ref[...], preferred_element_type=F32)

    @pl.when(k == 0)
    def _():
        x1 = x_ref[...] + gate1_ref[...] * jnp.dot(a_ref[...], wo_ref[...], preferred_element_type=F32)
        h = _norm_modulate(x1, g_ref[...], shift_ref[...], scale_ref[...]).astype(BF16)
        h_ref[...] = h
        hidden_tile(h, x1)

    @pl.when(k > 0)
    def _():
        hidden_tile(h_ref[...], out_ref[...])


def _ffn(a, w_out, w_out_layer, xs, g, mods, layer, w_gu, w_down, n_tiles):
    nh = FFN_HIDDEN // TH
    tile = _early_tile(n_tiles, nh // 2)
    return pl.pallas_call(
        _ffn_kernel,
        grid=(n_tiles, nh),
        in_specs=[
            pl.BlockSpec((TMF, D_MODEL), lambda i, k: (tile(i, k), 0)),
            pl.BlockSpec((None, D_MODEL, D_MODEL), lambda i, k: (w_out_layer, 0, 0),
                         pipeline_mode=pl.Buffered(1)),
            pl.BlockSpec((TMF, D_MODEL), lambda i, k: (tile(i, k), 0)),
            pl.BlockSpec((1, D_MODEL), lambda i, k: (0, 0)),
            _mod_spec(layer, 2, TMF, tile),
            _mod_spec(layer, 3, TMF, tile),
            _mod_spec(layer, 4, TMF, tile),
            _mod_spec(layer, 5, TMF),
            pl.BlockSpec((None, D_MODEL, TH), lambda i, k: (layer, 0, k)),
            pl.BlockSpec((None, D_MODEL, TH), lambda i, k: (layer, 0, nh + k)),
            pl.BlockSpec((None, TH, D_MODEL), lambda i, k: (layer, k, 0)),
        ],
        out_specs=pl.BlockSpec((TMF, D_MODEL), lambda i, k: (i, 0)),
        out_shape=jax.ShapeDtypeStruct((n_tiles * TMF, D_MODEL), F32),
        scratch_shapes=[pltpu.VMEM((TMF, D_MODEL), BF16)],
        compiler_params=_params("arbitrary", "arbitrary"),
        name="ffn",
    )(a, w_out, xs, g, mods, mods, mods, mods, w_gu, w_gu, w_down)


def _pair_major(a, axis=-1):
    n_freq = DIFF_DH // 4
    axis = axis % a.ndim
    shape = a.shape
    a = a.reshape(*shape[:axis], shape[axis] // DIFF_DH, 2, 2, n_freq, *shape[axis + 1:])
    return jnp.swapaxes(a, axis + 1, axis + 2).reshape(shape)


def _rope_tables():
    n_freq = DIFF_DH // 4
    pos = jnp.arange(SEQ)
    freqs = ROPE_BASE ** (-jnp.arange(n_freq, dtype=F32) / n_freq)
    ang = jnp.stack([pos // GRID_W, pos % GRID_W], axis=-1).astype(F32)[:, :, None] * freqs
    cos, sin = jnp.cos(ang), jnp.sin(ang)
    cos_t = jnp.concatenate([cos[:, 0], cos[:, 1], cos[:, 0], cos[:, 1]], axis=-1)
    sin_t = jnp.concatenate([-sin[:, 0], -sin[:, 1], sin[:, 0], sin[:, 1]], axis=-1)
    cos_t = jnp.concatenate([cos_t, jnp.ones((TM, LANES), F32)], axis=0)
    sin_t = jnp.concatenate([sin_t, jnp.zeros((TM, LANES), F32)], axis=0)
    return cos_t, sin_t


def kernel(x, c, ctx, c_ctx, ada_w, ada_b, norm_g, mlstm_w_in, mlstm_gate_b, mlstm_head_g, mlstm_w_out,
           diff_w_in, diff_w_out, diff_q_g, diff_k_g, diff_lq1, diff_lk1, diff_lq2, diff_lk2, diff_subln_g,
           ffn_w_gu, ffn_w_down):
    assert x.shape == (BATCH, SEQ, D_MODEL) and ctx.shape == (BATCH, CTX_LEN, D_MODEL)
    xs = jnp.concatenate([x.reshape(N_LAT, D_MODEL), ctx.reshape(N_CTX, D_MODEL)], axis=0)
    cc = jnp.concatenate([c, c_ctx[None], jnp.zeros((8 - BATCH - 1, D_MODEL), F32)], axis=0)
    mods = _ada(cc, ada_w, ada_b).reshape(DEPTH, 8, 1, 6 * D_MODEL)
    cos_t, sin_t = _rope_tables()
    row = lambda v: v.reshape(1, -1)
    ffn_gu = ffn_w_gu.astype(BF16)
    ffn_down = ffn_w_down.astype(BF16)
    mlstm_in = mlstm_w_in[:, :, :MLSTM_MAIN].astype(BF16)
    mlstm_kt = jnp.swapaxes(mlstm_w_in[:, :, MLSTM_QK:2 * MLSTM_QK], 1, 2).astype(BF16)
    mlstm_out = mlstm_w_out.astype(BF16)
    diff_in = _pair_major(diff_w_in).astype(BF16)
    diff_out = _pair_major(diff_w_out, axis=1).astype(BF16)

    for i in range(DEPTH):
        last = i == DEPTH - 1
        n_rows = N_LAT if last else NT
        j = i // N_MIXERS
        if i % N_MIXERS == 0:
            w_gate = jnp.pad(mlstm_w_in[j, :, MLSTM_MAIN:], ((0, 0), (0, LANES - N_GATES))).astype(BF16)
            gate_b = jnp.pad(mlstm_gate_b[j], (0, LANES - N_GATES)).reshape(1, LANES)
            act, k_t, gates = _inproj_mlstm(xs, row(norm_g[i, 0]), mods, i, mlstm_in, mlstm_kt, j, w_gate)
            hf = _mlstm_scan(act, k_t, gates, gate_b, None, None, reverse=False)
            a = _mlstm_scan(act, k_t, gates, gate_b, row(mlstm_head_g[j]), hf, reverse=True)
            w_out = mlstm_out
        else:
            lam_init = 0.8 - 0.6 * math.exp(-0.3 * i)
            q_gain = _pair_major(diff_q_g[j]) * (DIFF_DH ** -0.5 * math.log2(math.e))
            qkg = jnp.concatenate([jnp.tile(q_gain, 2 * DIFF_HEADS),
                                   jnp.tile(_pair_major(diff_k_g[j]), 2 * DIFF_HEADS)]).reshape(1, 2 * D_MODEL)
            qkv = _inproj_diff(xs, row(norm_g[i, 0]), mods, i, diff_in, j, qkg, cos_t, sin_t)
            lams = [row(diff_lq1[j]), row(diff_lk1[j]), row(diff_lq2[j]), row(diff_lk2[j])]
            subln_g = row(_pair_major(diff_subln_g[j]))
            a = _attention(qkv, lams, subln_g, lam_init, None)
            if not last:
                a = _attention(qkv, lams, subln_g, lam_init, a)
            w_out = diff_out
        xs = _ffn(a, w_out, j, xs, row(norm_g[i, 1]), mods, i, ffn_gu, ffn_down, n_rows // TMF)
    return xs.reshape(BATCH, SEQ, D_MODEL)
```

```python
import functools
import math

import jax
import jax.numpy as jnp
from jax import lax
from jax.experimental import pallas as pl
from jax.experimental.pallas import tpu as pltpu

F32 = jnp.float32
BF16 = jnp.bfloat16

D_MODEL = 2048
BATCH = 4
SEQ = 4096
DEPTH = 4
GRID_W = 64
CTX_LEN = 256
N_MIXERS = 2

MLSTM_HEADS = 8
MLSTM_DK = D_MODEL // (2 * MLSTM_HEADS)
MLSTM_DV = D_MODEL // MLSTM_HEADS
GATE_CAP = 15.0
MLSTM_QK = MLSTM_HEADS * MLSTM_DK
MLSTM_MAIN = 2 * MLSTM_QK + 2 * D_MODEL
N_GATES = 4 * MLSTM_HEADS

DIFF_HEADS = 8
DIFF_DH = D_MODEL // (2 * DIFF_HEADS)
ROPE_BASE = 10000.0

FFN_HIDDEN = -(-(8 * D_MODEL) // (3 * 256)) * 256
EPS = 1e-6

N_LAT = BATCH * SEQ
N_CTX = BATCH * CTX_LEN
NT = N_LAT + N_CTX

LANES = 128
TM = 1024
TN = 1024
TMF = 512
TH = 512
CHUNK = 256
TQ = 1024
TK = 1024
VMEM_LIMIT = 56 * 1024 * 1024

NT_DIMS = (((1,), (1,)), ((), ()))
TN_DIMS = (((0,), (0,)), ((), ()))


def _params(*sem):
    return pltpu.CompilerParams(dimension_semantics=sem, vmem_limit_bytes=VMEM_LIMIT)


def _mod_row(i, tm):
    return jnp.minimum(i // (SEQ // tm), BATCH)


def _mod_spec(layer, k, tm=TM, tile=lambda i, j: i):
    return pl.BlockSpec((None, None, 1, D_MODEL), lambda i, j: (layer, _mod_row(tile(i, j), tm), 0, k))


def _early_tile(n_tiles, from_step):
    return lambda i, j: jnp.minimum(i + jnp.where(j >= from_step, 1, 0), n_tiles - 1)


def _weight_spec(layer, tn):
    return pl.BlockSpec((None, D_MODEL, tn), lambda i, j: (layer, 0, j))


def _lane_repeat(x, n, axis):
    assert axis == 1
    return jnp.concatenate([x] * n, axis=1) if n > 1 else x


def _norm_modulate(x, g, shift, scale):
    y = x * lax.rsqrt(jnp.mean(x * x, axis=-1, keepdims=True) + EPS) * g
    return y * (1.0 + scale) + shift


def _ada_kernel(c_ref, w_ref, b_ref, o_ref):
    c = c_ref[...]
    a = (c * jax.nn.sigmoid(c)).astype(BF16)
    o_ref[...] = jnp.dot(a, w_ref[...].astype(BF16), preferred_element_type=F32) + b_ref[...]


def _ada(cc, ada_w, ada_b):
    tn = 1024
    return pl.pallas_call(
        _ada_kernel,
        grid=(DEPTH, 6 * D_MODEL // tn),
        in_specs=[
            pl.BlockSpec((8, D_MODEL), lambda l, j: (0, 0)),
            pl.BlockSpec((None, D_MODEL, tn), lambda l, j: (l, 0, j)),
            pl.BlockSpec((None, 1, tn), lambda l, j: (l, 0, j)),
        ],
        out_specs=pl.BlockSpec((None, 8, tn), lambda l, j: (l, 0, j)),
        out_shape=jax.ShapeDtypeStruct((DEPTH, 8, 6 * D_MODEL), F32),
        compiler_params=_params("arbitrary", "arbitrary"),
        name="ada_mod",
    )(cc, ada_w, ada_b.reshape(DEPTH, 1, 6 * D_MODEL))


MLSTM_ACT = MLSTM_MAIN - MLSTM_QK
Q_TILE = MLSTM_ACT // TN - 1


def _inproj_mlstm_kernel(x_ref, g_ref, shift_ref, scale_ref, w_ref, wkt_ref, wg_ref,
                         out_ref, kt_ref, gate_ref, h_ref):
    j = pl.program_id(1)

    @pl.when(j == 0)
    def _():
        h_ref[...] = _norm_modulate(x_ref[...], g_ref[...], shift_ref[...], scale_ref[...]).astype(BF16)
        acc = jnp.dot(h_ref[...], w_ref[...], preferred_element_type=F32)
        out_ref[...] = (acc * MLSTM_DK ** -0.5).astype(BF16)
        gate_ref[...] = jnp.dot(h_ref[...], wg_ref[...], preferred_element_type=F32)

    @pl.when(j == 1)
    def _():
        kt_ref[...] = lax.dot_general(wkt_ref[...], h_ref[...], NT_DIMS,
                                      preferred_element_type=F32).astype(BF16)

    @pl.when(j > 1)
    def _():
        out_ref[...] = jnp.dot(h_ref[...], w_ref[...], preferred_element_type=F32).astype(BF16)


def _inproj_mlstm(xs, g, mods, layer, w_all, wkt_all, w_layer, w_gate):
    assert TN == MLSTM_QK
    tile = _early_tile(NT // TM, 3)
    return pl.pallas_call(
        _inproj_mlstm_kernel,
        grid=(NT // TM, MLSTM_MAIN // TN),
        in_specs=[
            pl.BlockSpec((TM, D_MODEL), lambda i, j: (tile(i, j), 0)),
            pl.BlockSpec((1, D_MODEL), lambda i, j: (0, 0)),
            _mod_spec(layer, 0, tile=tile),
            _mod_spec(layer, 1, tile=tile),
            pl.BlockSpec((None, D_MODEL, TN), lambda i, j: (w_layer, 0, jnp.where(j == 1, 0, j))),
            pl.BlockSpec((None, MLSTM_QK, D_MODEL), lambda i, j: (w_layer, 0, 0),
                         pipeline_mode=pl.Buffered(1)),
            pl.BlockSpec((D_MODEL, LANES), lambda i, j: (0, 0)),
        ],
        out_specs=[
            pl.BlockSpec((TM, TN), lambda i, j: (i, jnp.where(j <= 1, Q_TILE, j - 2))),
            pl.BlockSpec((MLSTM_QK, TM), lambda i, j: (0, i)),
            pl.BlockSpec((TM, LANES), lambda i, j: (i, 0)),
        ],
        out_shape=[
            jax.ShapeDtypeStruct((NT, MLSTM_ACT), BF16),
            jax.ShapeDtypeStruct((MLSTM_QK, NT), BF16),
            jax.ShapeDtypeStruct((NT, LANES), F32),
        ],
        scratch_shapes=[pltpu.VMEM((TM, D_MODEL), BF16)],
        compiler_params=_params("arbitrary", "arbitrary"),
        name="inproj_mlstm",
    )(xs, g, mods, mods, w_all, wkt_all, w_gate)


def _mlstm_scan_kernel(*refs, reverse, final):
    if final:
        (q_ref, kt_ref, v_ref, gate_ref, gb_ref, o_ref, hf_ref, hg_ref,
         out_ref, c_ref, n_ref, m_ref) = refs
    else:
        q_ref, kt_ref, v_ref, gate_ref, gb_ref, out_ref, c_ref, n_ref, m_ref = refs
    L = CHUNK

    @pl.when(pl.program_id(1) == 0)
    def _():
        c_ref[...] = jnp.zeros_like(c_ref)
        n_ref[...] = jnp.zeros_like(n_ref)
        m_ref[...] = jnp.zeros_like(m_ref)

    gg = GATE_CAP * jnp.tanh((gate_ref[...] + gb_ref[...]) / GATE_CAP)
    lsig = jax.nn.log_sigmoid(gg)
    t_idx = lax.broadcasted_iota(jnp.int32, (L, L), 0)
    s_idx = lax.broadcasted_iota(jnp.int32, (L, L), 1)
    mask = (s_idx >= t_idx) if reverse else (s_idx <= t_idx)
    tri = mask.astype(BF16)
    hi = lsig.astype(BF16)
    rest = lsig - hi.astype(F32)
    mid = rest.astype(BF16)
    lo = (rest - mid.astype(F32)).astype(BF16)
    bcum = (jnp.dot(tri, hi, preferred_element_type=F32) + jnp.dot(tri, mid, preferred_element_type=F32)
            + jnp.dot(tri, lo, preferred_element_type=F32))
    bal = pltpu.roll(bcum, LANES - MLSTM_HEADS, 1)
    r_t = (gg - bal).T
    base = 2 * MLSTM_HEADS if reverse else 0
    last = 0 if reverse else L - 1

    for h in range(MLSTM_HEADS):
        col = base + h
        b_rep = jnp.broadcast_to(bal[:, col:col + 1], (L, LANES))
        r_row = r_t[col:col + 1, :]
        m = m_ref[h]
        m11 = m[:, 0:1]
        b_last = b_rep[last:last + 1, 0:1]

        dmat = jnp.where(mask, _lane_repeat(b_rep, L // LANES, 1) + r_row, -jnp.inf)
        inter = b_rep + m
        m_t = jnp.maximum(inter, jnp.max(dmat, axis=-1, keepdims=True))
        w_intra = jnp.exp(dmat - _lane_repeat(m_t, L // LANES, 1))
        w_inter = jnp.exp(inter - m_t)

        qh = q_ref[:, h * MLSTM_DK:(h + 1) * MLSTM_DK]
        kth = kt_ref[h * MLSTM_DK:(h + 1) * MLSTM_DK, :]
        vh = v_ref[:, h * MLSTM_DV:(h + 1) * MLSTM_DV]
        s = jnp.dot(qh, kth, preferred_element_type=F32) * w_intra
        c_old = c_ref[h]
        num = (_lane_repeat(w_inter, MLSTM_DV // LANES, 1)
               * jnp.dot(qh, c_old.astype(BF16), preferred_element_type=F32)
               + jnp.dot(s.astype(BF16), vh, preferred_element_type=F32))
        n_old = n_ref[h]
        qn = jnp.dot(qh, n_old.astype(BF16), preferred_element_type=F32)
        den = w_inter * qn + jnp.sum(s, axis=-1, keepdims=True)
        inv = 1.0 / jnp.maximum(jnp.abs(den), jnp.exp(-m_t))
        hout = num * _lane_repeat(inv, MLSTM_DV // LANES, 1)

        a_row = b_last + r_row
        m_new = jnp.maximum(b_last + m11, jnp.max(a_row, axis=-1, keepdims=True))
        wk = jnp.exp(a_row - m_new)
        dec = jnp.exp(b_last + m11 - m_new)
        kw_t = kth.astype(F32) * wk
        c_ref[h] = dec * c_old + jnp.dot(kw_t.astype(BF16), vh, preferred_element_type=F32)
        n_ref[h] = dec * n_old + jnp.sum(kw_t, axis=-1, keepdims=True)
        m_ref[h] = jnp.broadcast_to(m_new, (1, LANES))

        sl = slice(h * MLSTM_DV, (h + 1) * MLSTM_DV)
        if final:
            hs = hf_ref[:, sl] + hout
            y = hs * lax.rsqrt(jnp.mean(hs * hs, axis=-1, keepdims=True) + EPS) * hg_ref[:, sl]
            out_ref[:, sl] = (y * jax.nn.sigmoid(o_ref[:, sl].astype(F32))).astype(BF16)
        else:
            out_ref[:, sl] = hout


def _mlstm_scan(act, k_t, gates, gate_b, head_g, hf, *, reverse):
    final = hf is not None
    n_lat_chunks = SEQ // CHUNK
    ctx_blk0 = N_LAT // CHUNK

    def row_blk(b, c):
        j = (n_lat_chunks - c) if reverse else (c - 1)
        return jnp.where(c == 0, ctx_blk0 + b, b * n_lat_chunks + j)

    in_specs = [
        pl.BlockSpec((CHUNK, MLSTM_QK), lambda b, c: (row_blk(b, c), Q_TILE)),
        pl.BlockSpec((MLSTM_QK, CHUNK), lambda b, c: (0, row_blk(b, c))),
        pl.BlockSpec((CHUNK, D_MODEL), lambda b, c: (row_blk(b, c), 0)),
        pl.BlockSpec((CHUNK, LANES), lambda b, c: (row_blk(b, c), 0)),
        pl.BlockSpec((1, LANES), lambda b, c: (0, 0)),
    ]
    args = [act, k_t, act, gates, gate_b]
    if final:
        in_specs += [
            pl.BlockSpec((CHUNK, D_MODEL), lambda b, c: (row_blk(b, c), 1)),
            pl.BlockSpec((CHUNK, D_MODEL), lambda b, c: (row_blk(b, c), 0)),
            pl.BlockSpec((1, D_MODEL), lambda b, c: (0, 0)),
        ]
        args += [act, hf, head_g]
    return pl.pallas_call(
        functools.partial(_mlstm_scan_kernel, reverse=reverse, final=final),
        grid=(BATCH, 1 + n_lat_chunks),
        in_specs=in_specs,
        out_specs=pl.BlockSpec((CHUNK, D_MODEL), lambda b, c: (row_blk(b, c), 0)),
        out_shape=jax.ShapeDtypeStruct((NT, D_MODEL), BF16 if final else F32),
        scratch_shapes=[
            pltpu.VMEM((MLSTM_HEADS, MLSTM_DK, MLSTM_DV), F32),
            pltpu.VMEM((MLSTM_HEADS, MLSTM_DK, LANES), F32),
            pltpu.VMEM((MLSTM_HEADS, 1, LANES), F32),
        ],
        compiler_params=_params("arbitrary", "arbitrary"),
        name="mlstm_scan_bwd" if reverse else "mlstm_scan_fwd",
    )(*args)


def _inproj_diff_kernel(x_ref, g_ref, shift_ref, scale_ref, w_ref, qkg_ref, cos_ref, sin_ref,
                        out_ref, h_ref, acc0_ref, acc1_ref):
    j = pl.program_id(1)
    n_qk_tiles = 2 * D_MODEL // TN
    n_tiles = 3 * D_MODEL // TN

    def matmul_into(acc_ref):
        acc_ref[...] = jnp.dot(h_ref[...], w_ref[...], preferred_element_type=F32)

    def norm_rope_from(acc_ref):
        cos = cos_ref[...]
        sin = sin_ref[...]
        for grp in range(TN // LANES):
            sl = slice(grp * LANES, (grp + 1) * LANES)
            xg = acc_ref[:, sl]
            y = xg * lax.rsqrt(jnp.mean(xg * xg, axis=-1, keepdims=True) + EPS) * qkg_ref[:, sl]
            out_ref[:, sl] = (y * cos + pltpu.roll(y, LANES // 2, 1) * sin).astype(BF16)

    def cast_from(acc_ref):
        out_ref[...] = acc_ref[...].astype(BF16)

    @pl.when(j == 0)
    def _():
        h_ref[...] = _norm_modulate(x_ref[...], g_ref[...], shift_ref[...], scale_ref[...]).astype(BF16)
        matmul_into(acc0_ref)

    is_qk_finish = jnp.logical_and(j >= 1, j <= n_qk_tiles)

    @pl.when(jnp.logical_and(is_qk_finish, j % 2 == 1))
    def _():
        matmul_into(acc1_ref)
        norm_rope_from(acc0_ref)

    @pl.when(jnp.logical_and(is_qk_finish, j % 2 == 0))
    def _():
        matmul_into(acc0_ref)
        norm_rope_from(acc1_ref)

    assert n_qk_tiles % 2 == 0 and n_tiles == n_qk_tiles + 2

    @pl.when(j == n_tiles - 1)
    def _():
        matmul_into(acc1_ref)
        cast_from(acc0_ref)

    @pl.when(j == n_tiles)
    def _():
        cast_from(acc1_ref)


def _inproj_diff(xs, g, mods, layer, w_all, w_layer, qkg, cos_t, sin_t):
    n_qk_tiles = 2 * D_MODEL // TN
    n_tiles = 3 * D_MODEL // TN
    lat_tiles = SEQ // TM

    tile = _early_tile(NT // TM, 3)
    rope_tile = _early_tile(NT // TM, n_qk_tiles + 1)

    def rope_blk(i, j):
        t = rope_tile(i, j)
        return (jnp.where(t < N_LAT // TM, t % lat_tiles, lat_tiles), 0)

    return pl.pallas_call(
        _inproj_diff_kernel,
        grid=(NT // TM, n_tiles + 1),
        in_specs=[
            pl.BlockSpec((TM, D_MODEL), lambda i, j: (tile(i, j), 0)),
            pl.BlockSpec((1, D_MODEL), lambda i, j: (0, 0)),
            _mod_spec(layer, 0, tile=tile),
            _mod_spec(layer, 1, tile=tile),
            pl.BlockSpec((None, D_MODEL, TN), lambda i, j: (w_layer, 0, jnp.where(j == n_tiles, 0, j))),
            pl.BlockSpec((1, TN), lambda i, j: (0, jnp.clip(j - 1, 0, n_qk_tiles - 1))),
            pl.BlockSpec((TM, LANES), rope_blk),
            pl.BlockSpec((TM, LANES), rope_blk),
        ],
        out_specs=pl.BlockSpec((TM, TN), lambda i, j: (i, jnp.maximum(j - 1, 0))),
        out_shape=jax.ShapeDtypeStruct((NT, 3 * D_MODEL), BF16),
        scratch_shapes=[
            pltpu.VMEM((TM, D_MODEL), BF16),
            pltpu.VMEM((TM, TN), F32),
            pltpu.VMEM((TM, TN), F32),
        ],
        compiler_params=_params("arbitrary", "arbitrary"),
        name="inproj_diff",
    )(xs, g, mods, mods, w_all, qkg, cos_t, sin_t)


def _attn_kernel(*refs, n_lat_chunks, lam_init):
    if n_lat_chunks:
        (q_ref, kl_ref, vl_ref, kc_ref, vc_ref, lq1_ref, lk1_ref, lq2_ref, lk2_ref, sg_ref,
         out_ref, acc_ref, m_ref, l_ref) = refs
    else:
        (q_ref, kc_ref, vc_ref, lq1_ref, lk1_ref, lq2_ref, lk2_ref, sg_ref, _,
         out_ref, acc_ref, m_ref, l_ref) = refs
    tq = q_ref.shape[0]
    dv = 2 * DIFF_DH
    q = q_ref[...]

    def scores(kblk):
        return jnp.concatenate(
            [lax.dot_general(q[:, t * DIFF_DH:(t + 1) * DIFF_DH], kblk[:, t * DIFF_DH:(t + 1) * DIFF_DH],
                             NT_DIMS, preferred_element_type=F32) for t in range(2)], axis=0)

    s = scores(kc_ref[...])
    m0 = jnp.max(s, axis=-1, keepdims=True)
    p = jnp.exp2(s - m0)
    m_ref[...] = jnp.broadcast_to(m0, m_ref.shape)
    l_ref[...] = jnp.broadcast_to(jnp.sum(p, axis=-1, keepdims=True), l_ref.shape)
    acc_ref[...] = jnp.dot(p.astype(BF16), vc_ref[...], preferred_element_type=F32)

    if n_lat_chunks:
        def body(c, carry):
            off = pl.multiple_of(c * TK, TK)
            s = scores(kl_ref[pl.ds(off, TK), :])
            m_old = m_ref[...]
            m_new = jnp.maximum(m_old, jnp.max(s, axis=-1, keepdims=True))
            p = jnp.exp2(s - _lane_repeat(m_new, TK // LANES, 1))
            alpha = jnp.exp2(m_old - m_new)
            l_ref[...] = alpha * l_ref[...] + jnp.sum(p, axis=-1, keepdims=True)
            acc_ref[...] = (_lane_repeat(alpha, dv // LANES, 1) * acc_ref[...]
                            + jnp.dot(p.astype(BF16), vl_ref[pl.ds(off, TK), :],
                                      preferred_element_type=F32))
            m_ref[...] = m_new
            return carry
        lax.fori_loop(0, n_lat_chunks, body, 0)

    lam = (jnp.exp(jnp.sum(lq1_ref[...] * lk1_ref[...], axis=-1, keepdims=True))
           - jnp.exp(jnp.sum(lq2_ref[...] * lk2_ref[...], axis=-1, keepdims=True)) + lam_init)
    o_all = acc_ref[...] * _lane_repeat(1.0 / l_ref[...], dv // LANES, 1)
    o = o_all[:tq] - lam * o_all[tq:]
    y = o * lax.rsqrt(jnp.mean(o * o, axis=-1, keepdims=True) + EPS) * sg_ref[...]
    out_ref[...] = (y * (1.0 - lam_init)).astype(BF16)


def _attention(qkv, lams, subln_g, lam_init, prev_out, out_rows=NT):
    dv = 2 * DIFF_DH
    kcol = D_MODEL // dv
    vcol = 2 * D_MODEL // dv
    ctx_blk0 = N_LAT // CTX_LEN
    small = [pl.BlockSpec((1, DIFF_DH), lambda b, h, i: (0, 0))] * 4
    small.append(pl.BlockSpec((1, dv), lambda b, h, i: (0, 0)))
    ctx_specs = [
        pl.BlockSpec((CTX_LEN, dv), lambda b, h, i: (ctx_blk0 + b, kcol + h)),
        pl.BlockSpec((CTX_LEN, dv), lambda b, h, i: (ctx_blk0 + b, vcol + h)),
    ]
    if prev_out is None:
        tq, nq = TQ, SEQ // TQ
        q_map = lambda b, h, i: (b * nq + i, h)
        in_specs = [
            pl.BlockSpec((tq, dv), q_map),
            pl.BlockSpec((SEQ, dv), lambda b, h, i: (b, kcol + h)),
            pl.BlockSpec((SEQ, dv), lambda b, h, i: (b, vcol + h)),
        ] + ctx_specs + small
        args = [qkv, qkv, qkv, qkv, qkv] + list(lams) + [subln_g]
        aliases = {}
        n_lat_chunks = SEQ // TK
    else:
        tq, nq = CTX_LEN, 1
        q_map = lambda b, h, i: (ctx_blk0 + b, h)
        in_specs = [pl.BlockSpec((tq, dv), q_map)] + ctx_specs + small
        in_specs.append(pl.BlockSpec(memory_space=pl.ANY))
        args = [qkv, qkv, qkv] + list(lams) + [subln_g, prev_out]
        aliases = {len(args) - 1: 0}
        n_lat_chunks = 0
    return pl.pallas_call(
        functools.partial(_attn_kernel, n_lat_chunks=n_lat_chunks, lam_init=lam_init),
        grid=(BATCH, DIFF_HEADS, nq),
        in_specs=in_specs,
        out_specs=pl.BlockSpec((tq, dv), q_map),
        out_shape=jax.ShapeDtypeStruct((out_rows, D_MODEL), BF16),
        scratch_shapes=[
            pltpu.VMEM((2 * tq, dv), F32),
            pltpu.VMEM((2 * tq, LANES), F32),
            pltpu.VMEM((2 * tq, LANES), F32),
        ],
        input_output_aliases=aliases,
        compiler_params=_params("arbitrary", "arbitrary", "arbitrary"),
        name="diff_attn_ctx" if n_lat_chunks == 0 else "diff_attn",
    )(*args)


def _ffn_kernel(a_ref, wo_ref, x_ref, g_ref, gate1_ref, shift_ref, scale_ref, gate2_ref, wgu_hbm, wd_hbm,
                out_ref, h_ref, wg_buf, wu_buf, wd_buf, sem, *, layer, n_tiles):
    i = pl.program_id(0)
    nh = FFN_HIDDEN // TH
    assert nh % 2 == 1

    def weight_copies(k, slot):
        col = pl.multiple_of(k * TH, TH)
        return (
            pltpu.make_async_copy(wgu_hbm.at[layer, :, pl.ds(col, TH)], wg_buf.at[slot], sem.at[0, slot]),
            pltpu.make_async_copy(wgu_hbm.at[layer, :, pl.ds(FFN_HIDDEN + col, TH)], wu_buf.at[slot],
                                  sem.at[1, slot]),
            pltpu.make_async_copy(wd_hbm.at[layer, pl.ds(col, TH), :], wd_buf.at[slot], sem.at[2, slot]),
        )

    def start(k, slot):
        for cp in weight_copies(k, slot):
            cp.start()

    def wait(k, slot):
        for cp in weight_copies(k, slot):
            cp.wait()

    def hidden_tile(h, base, slot):
        gt = jnp.dot(h, wg_buf[slot], preferred_element_type=F32)
        up = jnp.dot(h, wu_buf[slot], preferred_element_type=F32)
        act = (gt * jax.nn.sigmoid(gt) * up).astype(BF16)
        out_ref[...] = base + gate2_ref[...] * jnp.dot(act, wd_buf[slot], preferred_element_type=F32)

    slot0 = i % 2

    @pl.when(i == 0)
    def _():
        start(0, 0)

    wait(0, slot0)
    start(1, 1 - slot0)
    x1 = x_ref[...] + gate1_ref[...] * jnp.dot(a_ref[...], wo_ref[...], preferred_element_type=F32)
    h = _norm_modulate(x1, g_ref[...], shift_ref[...], scale_ref[...]).astype(BF16)
    h_ref[...] = h
    hidden_tile(h, x1, slot0)

    def body(k, carry):
        slot = (i + k) % 2
        wait(k, slot)

        @pl.when(k + 1 < nh)
        def _():
            start(k + 1, 1 - slot)

        @pl.when(jnp.logical_and(k + 1 == nh, i + 1 < n_tiles))
        def _():
            start(0, 1 - slot)

        hidden_tile(h_ref[...], out_ref[...], slot)
        return carry

    lax.fori_loop(1, nh, body, 0)


def _ffn(a, w_out, w_out_layer, xs, g, mods, layer, w_gu, w_down, n_tiles):
    row_tile = lambda i: (i, 0)
    mod_spec = lambda chunk: pl.BlockSpec((None, None, 1, D_MODEL),
                                          lambda i: (layer, _mod_row(i, TMF), 0, chunk))
    return pl.pallas_call(
        functools.partial(_ffn_kernel, layer=layer, n_tiles=n_tiles),
        grid=(n_tiles,),
        in_specs=[
            pl.BlockSpec((TMF, D_MODEL), row_tile),
            pl.BlockSpec((None, D_MODEL, D_MODEL), lambda i: (w_out_layer, 0, 0),
                         pipeline_mode=pl.Buffered(1)),
            pl.BlockSpec((TMF, D_MODEL), row_tile),
            pl.BlockSpec((1, D_MODEL), lambda i: (0, 0)),
            mod_spec(2),
            mod_spec(3),
            mod_spec(4),
            mod_spec(5),
            pl.BlockSpec(memory_space=pl.ANY),
            pl.BlockSpec(memory_space=pl.ANY),
        ],
        out_specs=pl.BlockSpec((TMF, D_MODEL), row_tile),
        out_shape=jax.ShapeDtypeStruct((n_tiles * TMF, D_MODEL), F32),
        scratch_shapes=[
            pltpu.VMEM((TMF, D_MODEL), BF16),
            pltpu.VMEM((2, D_MODEL, TH), BF16),
            pltpu.VMEM((2, D_MODEL, TH), BF16),
            pltpu.VMEM((2, TH, D_MODEL), BF16),
            pltpu.SemaphoreType.DMA((3, 2)),
        ],
        compiler_params=_params("arbitrary"),
        name="ffn",
    )(a, w_out, xs, g, mods, mods, mods, mods, w_gu, w_down)


def _pair_major(a):
    n_freq = DIFF_DH // 4
    lead = a.shape[:-1]
    a = a.reshape(*lead, a.shape[-1] // DIFF_DH, 2, 2, n_freq)
    return jnp.swapaxes(a, -3, -2).reshape(*lead, -1)


def _rope_tables():
    n_freq = DIFF_DH // 4
    pos = jnp.arange(SEQ)
    freqs = ROPE_BASE ** (-jnp.arange(n_freq, dtype=F32) / n_freq)
    ang = jnp.stack([pos // GRID_W, pos % GRID_W], axis=-1).astype(F32)[:, :, None] * freqs
    cos, sin = jnp.cos(ang), jnp.sin(ang)
    cos_t = jnp.concatenate([cos[:, 0], cos[:, 1], cos[:, 0], cos[:, 1]], axis=-1)
    sin_t = jnp.concatenate([-sin[:, 0], -sin[:, 1], sin[:, 0], sin[:, 1]], axis=-1)
    cos_t = jnp.concatenate([cos_t, jnp.ones((TM, LANES), F32)], axis=0)
    sin_t = jnp.concatenate([sin_t, jnp.zeros((TM, LANES), F32)], axis=0)
    return cos_t, sin_t


def kernel(x, c, ctx, c_ctx, ada_w, ada_b, norm_g, mlstm_w_in, mlstm_gate_b, mlstm_head_g, mlstm_w_out,
           diff_w_in, diff_w_out, diff_q_g, diff_k_g, diff_lq1, diff_lk1, diff_lq2, diff_lk2, diff_subln_g,
           ffn_w_gu, ffn_w_down):
    assert x.shape == (BATCH, SEQ, D_MODEL) and ctx.shape == (BATCH, CTX_LEN, D_MODEL)
    xs = jnp.concatenate([x.reshape(N_LAT, D_MODEL), ctx.reshape(N_CTX, D_MODEL)], axis=0)
    cc = jnp.concatenate([c, c_ctx[None], jnp.zeros((8 - BATCH - 1, D_MODEL), F32)], axis=0)
    mods = _ada(cc, ada_w, ada_b).reshape(DEPTH, 8, 1, 6 * D_MODEL)
    cos_t, sin_t = _rope_tables()
    row = lambda v: v.reshape(1, -1)
    ffn_gu = ffn_w_gu.astype(BF16)
    ffn_down = ffn_w_down.astype(BF16)
    mlstm_in = mlstm_w_in.astype(BF16)
    mlstm_kt = jnp.swapaxes(mlstm_w_in[:, :, MLSTM_QK:2 * MLSTM_QK], 1, 2).astype(BF16)
    mlstm_out = mlstm_w_out.astype(BF16)
    diff_in = jnp.concatenate([_pair_major(diff_w_in[..., :2 * D_MODEL]), diff_w_in[..., 2 * D_MODEL:]],
                              axis=-1).astype(BF16)
    diff_out = diff_w_out.astype(BF16)

    for i in range(DEPTH):
        last = i == DEPTH - 1
        n_rows = N_LAT if last else NT
        j = i // N_MIXERS
        if i % N_MIXERS == 0:
            w_gate = jnp.pad(mlstm_w_in[j, :, MLSTM_MAIN:], ((0, 0), (0, LANES - N_GATES))).astype(BF16)
            gate_b = jnp.pad(mlstm_gate_b[j], (0, LANES - N_GATES)).reshape(1, LANES)
            act, k_t, gates = _inproj_mlstm(xs, row(norm_g[i, 0]), mods, i, mlstm_in, mlstm_kt, j, w_gate)
            hf = _mlstm_scan(act, k_t, gates, gate_b, None, None, reverse=False)
            a = _mlstm_scan(act, k_t, gates, gate_b, row(mlstm_head_g[j]), hf, reverse=True)
            w_out = mlstm_out
        else:
            lam_init = 0.8 - 0.6 * math.exp(-0.3 * i)
            q_gain = _pair_major(diff_q_g[j]) * (DIFF_DH ** -0.5 * math.log2(math.e))
            qkg = jnp.concatenate([jnp.tile(q_gain, 2 * DIFF_HEADS),
                                   jnp.tile(_pair_major(diff_k_g[j]), 2 * DIFF_HEADS)]).reshape(1, 2 * D_MODEL)
            qkv = _inproj_diff(xs, row(norm_g[i, 0]), mods, i, diff_in, j, qkg, cos_t, sin_t)
            lams = [row(diff_lq1[j]), row(diff_lk1[j]), row(diff_lq2[j]), row(diff_lk2[j])]
            a = _attention(qkv, lams, row(diff_subln_g[j]), lam_init, None, n_rows)
            if not last:
                a = _attention(qkv, lams, row(diff_subln_g[j]), lam_init, a)
            w_out = diff_out
        xs = _ffn(a, w_out, j, xs, row(norm_g[i, 1]), mods, i, ffn_gu, ffn_down, n_rows // TMF)
    return xs.reshape(BATCH, SEQ, D_MODEL)
```

```python
import functools
import math

import jax
import jax.numpy as jnp
from jax import lax
from jax.experimental import pallas as pl
from jax.experimental.pallas import tpu as pltpu

F32 = jnp.float32
BF16 = jnp.bfloat16

D_MODEL = 2048
BATCH = 4
SEQ = 4096
DEPTH = 4
GRID_W = 64
CTX_LEN = 256
N_MIXERS = 2

MLSTM_HEADS = 8
MLSTM_DK = D_MODEL // (2 * MLSTM_HEADS)
MLSTM_DV = D_MODEL // MLSTM_HEADS
GATE_CAP = 15.0
MLSTM_QK = MLSTM_HEADS * MLSTM_DK
MLSTM_MAIN = 2 * MLSTM_QK + 2 * D_MODEL
N_GATES = 4 * MLSTM_HEADS

DIFF_HEADS = 8
DIFF_DH = D_MODEL // (2 * DIFF_HEADS)
ROPE_BASE = 10000.0

FFN_HIDDEN = -(-(8 * D_MODEL) // (3 * 256)) * 256
EPS = 1e-6

N_LAT = BATCH * SEQ
N_CTX = BATCH * CTX_LEN
NT = N_LAT + N_CTX

LANES = 128
TM = 1024
TN = 1024
TMF = 512
TH = 512
CHUNK = 256
TQ = 1024
TK = 1024
VMEM_LIMIT = 56 * 1024 * 1024

NT_DIMS = (((1,), (1,)), ((), ()))


def _params(*sem):
    return pltpu.CompilerParams(dimension_semantics=sem, vmem_limit_bytes=VMEM_LIMIT)


def _mod_row(i, tm):
    return jnp.minimum(i // (SEQ // tm), BATCH)


def _mod_spec(layer, k, tm=TM, tile=lambda i, j: i):
    return pl.BlockSpec((None, None, 1, D_MODEL), lambda i, j: (layer, _mod_row(tile(i, j), tm), 0, k))


def _early_tile(n_tiles, from_step):
    return lambda i, j: jnp.minimum(i + jnp.where(j >= from_step, 1, 0), n_tiles - 1)


def _lane_repeat(x, n, axis):
    assert axis == 1
    return jnp.concatenate([x] * n, axis=1) if n > 1 else x


def _norm_modulate(x, g, shift, scale):
    y = x * lax.rsqrt(jnp.mean(x * x, axis=-1, keepdims=True) + EPS) * g
    return y * (1.0 + scale) + shift


def _ada_kernel(c_ref, w_ref, b_ref, o_ref):
    c = c_ref[...]
    a = (c * jax.nn.sigmoid(c)).astype(BF16)
    o_ref[...] = jnp.dot(a, w_ref[...].astype(BF16), preferred_element_type=F32) + b_ref[...]


def _ada(cc, ada_w, ada_b):
    tn = 1024
    return pl.pallas_call(
        _ada_kernel,
        grid=(DEPTH, 6 * D_MODEL // tn),
        in_specs=[
            pl.BlockSpec((8, D_MODEL), lambda l, j: (0, 0)),
            pl.BlockSpec((None, D_MODEL, tn), lambda l, j: (l, 0, j)),
            pl.BlockSpec((None, 1, tn), lambda l, j: (l, 0, j)),
        ],
        out_specs=pl.BlockSpec((None, 8, tn), lambda l, j: (l, 0, j)),
        out_shape=jax.ShapeDtypeStruct((DEPTH, 8, 6 * D_MODEL), F32),
        compiler_params=_params("arbitrary", "arbitrary"),
        name="ada_mod",
    )(cc, ada_w, ada_b.reshape(DEPTH, 1, 6 * D_MODEL))


MLSTM_ACT = MLSTM_MAIN - MLSTM_QK
Q_TILE = MLSTM_ACT // TN - 1


def _inproj_mlstm_kernel(x_ref, g_ref, shift_ref, scale_ref, w_ref, wkt_ref, wg_ref,
                         out_ref, kt_ref, gate_ref, h_ref):
    j = pl.program_id(1)

    @pl.when(j == 0)
    def _():
        h_ref[...] = _norm_modulate(x_ref[...], g_ref[...], shift_ref[...], scale_ref[...]).astype(BF16)
        acc = jnp.dot(h_ref[...], w_ref[...], preferred_element_type=F32)
        out_ref[...] = (acc * MLSTM_DK ** -0.5).astype(BF16)
        gate_ref[...] = jnp.dot(h_ref[...], wg_ref[...], preferred_element_type=F32)

    @pl.when(j == 1)
    def _():
        kt_ref[...] = lax.dot_general(wkt_ref[...], h_ref[...], NT_DIMS,
                                      preferred_element_type=F32).astype(BF16)

    @pl.when(j > 1)
    def _():
        out_ref[...] = jnp.dot(h_ref[...], w_ref[...], preferred_element_type=F32).astype(BF16)


def _inproj_mlstm(xs, g, mods, layer, w_all, wkt_all, w_layer, w_gate):
    assert TN == MLSTM_QK
    tile = _early_tile(NT // TM, 3)
    return pl.pallas_call(
        _inproj_mlstm_kernel,
        grid=(NT // TM, MLSTM_MAIN // TN),
        in_specs=[
            pl.BlockSpec((TM, D_MODEL), lambda i, j: (tile(i, j), 0)),
            pl.BlockSpec((1, D_MODEL), lambda i, j: (0, 0)),
            _mod_spec(layer, 0, tile=tile),
            _mod_spec(layer, 1, tile=tile),
            pl.BlockSpec((None, D_MODEL, TN), lambda i, j: (w_layer, 0, jnp.where(j == 1, 0, j))),
            pl.BlockSpec((None, MLSTM_QK, D_MODEL), lambda i, j: (w_layer, 0, 0),
                         pipeline_mode=pl.Buffered(1)),
            pl.BlockSpec((D_MODEL, LANES), lambda i, j: (0, 0)),
        ],
        out_specs=[
            pl.BlockSpec((TM, TN), lambda i, j: (i, jnp.where(j <= 1, Q_TILE, j - 2))),
            pl.BlockSpec((MLSTM_QK, TM), lambda i, j: (0, i)),
            pl.BlockSpec((TM, LANES), lambda i, j: (i, 0)),
        ],
        out_shape=[
            jax.ShapeDtypeStruct((NT, MLSTM_ACT), BF16),
            jax.ShapeDtypeStruct((MLSTM_QK, NT), BF16),
            jax.ShapeDtypeStruct((NT, LANES), F32),
        ],
        scratch_shapes=[pltpu.VMEM((TM, D_MODEL), BF16)],
        compiler_params=_params("arbitrary", "arbitrary"),
        name="inproj_mlstm",
    )(xs, g, mods, mods, w_all, wkt_all, w_gate)


def _mlstm_scan_kernel(*refs, reverse, final):
    if final:
        (q_ref, kt_ref, v_ref, gate_ref, gb_ref, o_ref, hf_ref, hg_ref,
         out_ref, c_ref, n_ref, m_ref) = refs
    else:
        q_ref, kt_ref, v_ref, gate_ref, gb_ref, out_ref, c_ref, n_ref, m_ref = refs
    L = CHUNK

    @pl.when(pl.program_id(1) == 0)
    def _():
        c_ref[...] = jnp.zeros_like(c_ref)
        n_ref[...] = jnp.zeros_like(n_ref)
        m_ref[...] = jnp.zeros_like(m_ref)

    gg = GATE_CAP * jnp.tanh((gate_ref[...] + gb_ref[...]) / GATE_CAP)
    lsig = jax.nn.log_sigmoid(gg)
    t_idx = lax.broadcasted_iota(jnp.int32, (L, L), 0)
    s_idx = lax.broadcasted_iota(jnp.int32, (L, L), 1)
    mask = (s_idx >= t_idx) if reverse else (s_idx <= t_idx)
    tri = mask.astype(BF16)
    hi = lsig.astype(BF16)
    rest = lsig - hi.astype(F32)
    mid = rest.astype(BF16)
    lo = (rest - mid.astype(F32)).astype(BF16)
    bcum = (jnp.dot(tri, hi, preferred_element_type=F32) + jnp.dot(tri, mid, preferred_element_type=F32)
            + jnp.dot(tri, lo, preferred_element_type=F32))
    bal = pltpu.roll(bcum, LANES - MLSTM_HEADS, 1)
    r_t = (gg - bal).T
    base = 2 * MLSTM_HEADS if reverse else 0
    last = 0 if reverse else L - 1

    for h in range(MLSTM_HEADS):
        col = base + h
        b_rep = jnp.broadcast_to(bal[:, col:col + 1], (L, LANES))
        r_row = r_t[col:col + 1, :]
        m = m_ref[h]
        m11 = m[:, 0:1]
        b_last = b_rep[last:last + 1, 0:1]

        dmat = jnp.where(mask, _lane_repeat(b_rep, L // LANES, 1) + r_row, -jnp.inf)
        inter = b_rep + m
        m_t = jnp.maximum(inter, jnp.max(dmat, axis=-1, keepdims=True))
        w_intra = jnp.exp(dmat - _lane_repeat(m_t, L // LANES, 1))
        w_inter = jnp.exp(inter - m_t)

        qh = q_ref[:, h * MLSTM_DK:(h + 1) * MLSTM_DK]
        kth = kt_ref[h * MLSTM_DK:(h + 1) * MLSTM_DK, :]
        vh = v_ref[:, h * MLSTM_DV:(h + 1) * MLSTM_DV]
        s = jnp.dot(qh, kth, preferred_element_type=F32) * w_intra
        c_old = c_ref[h]
        num = (_lane_repeat(w_inter, MLSTM_DV // LANES, 1)
               * jnp.dot(qh, c_old.astype(BF16), preferred_element_type=F32)
               + jnp.dot(s.astype(BF16), vh, preferred_element_type=F32))
        n_old = n_ref[h]
        qn = jnp.dot(qh, n_old.astype(BF16), preferred_element_type=F32)
        den = w_inter * qn + jnp.sum(s, axis=-1, keepdims=True)
        inv = 1.0 / jnp.maximum(jnp.abs(den), jnp.exp(-m_t))
        hout = num * _lane_repeat(inv, MLSTM_DV // LANES, 1)

        a_row = b_last + r_row
        m_new = jnp.maximum(b_last + m11, jnp.max(a_row, axis=-1, keepdims=True))
        wk = jnp.exp(a_row - m_new)
        dec = jnp.exp(b_last + m11 - m_new)
        kw_t = kth.astype(F32) * wk
        c_ref[h] = dec * c_old + jnp.dot(kw_t.astype(BF16), vh, preferred_element_type=F32)
        n_ref[h] = dec * n_old + jnp.sum(kw_t, axis=-1, keepdims=True)
        m_ref[h] = jnp.broadcast_to(m_new, (1, LANES))

        sl = slice(h * MLSTM_DV, (h + 1) * MLSTM_DV)
        if final:
            hs = hf_ref[:, sl] + hout
            y = hs * lax.rsqrt(jnp.mean(hs * hs, axis=-1, keepdims=True) + EPS) * hg_ref[:, sl]
            out_ref[:, sl] = (y * jax.nn.sigmoid(o_ref[:, sl].astype(F32))).astype(BF16)
        else:
            out_ref[:, sl] = hout


def _mlstm_scan(act, k_t, gates, gate_b, head_g, hf, *, reverse):
    final = hf is not None
    n_lat_chunks = SEQ // CHUNK
    ctx_blk0 = N_LAT // CHUNK

    def row_blk(b, c):
        j = (n_lat_chunks - c) if reverse else (c - 1)
        return jnp.where(c == 0, ctx_blk0 + b, b * n_lat_chunks + j)

    in_specs = [
        pl.BlockSpec((CHUNK, MLSTM_QK), lambda b, c: (row_blk(b, c), Q_TILE)),
        pl.BlockSpec((MLSTM_QK, CHUNK), lambda b, c: (0, row_blk(b, c))),
        pl.BlockSpec((CHUNK, D_MODEL), lambda b, c: (row_blk(b, c), 0)),
        pl.BlockSpec((CHUNK, LANES), lambda b, c: (row_blk(b, c), 0)),
        pl.BlockSpec((1, LANES), lambda b, c: (0, 0)),
    ]
    args = [act, k_t, act, gates, gate_b]
    if final:
        in_specs += [
            pl.BlockSpec((CHUNK, D_MODEL), lambda b, c: (row_blk(b, c), 1)),
            pl.BlockSpec((CHUNK, D_MODEL), lambda b, c: (row_blk(b, c), 0)),
            pl.BlockSpec((1, D_MODEL), lambda b, c: (0, 0)),
        ]
        args += [act, hf, head_g]
    return pl.pallas_call(
        functools.partial(_mlstm_scan_kernel, reverse=reverse, final=final),
        grid=(BATCH, 1 + n_lat_chunks),
        in_specs=in_specs,
        out_specs=pl.BlockSpec((CHUNK, D_MODEL), lambda b, c: (row_blk(b, c), 0)),
        out_shape=jax.ShapeDtypeStruct((NT, D_MODEL), BF16 if final else F32),
        scratch_shapes=[
            pltpu.VMEM((MLSTM_HEADS, MLSTM_DK, MLSTM_DV), F32),
            pltpu.VMEM((MLSTM_HEADS, MLSTM_DK, LANES), F32),
            pltpu.VMEM((MLSTM_HEADS, 1, LANES), F32),
        ],
        compiler_params=_params("arbitrary", "arbitrary"),
        name="mlstm_scan_bwd" if reverse else "mlstm_scan_fwd",
    )(*args)


def _inproj_diff_kernel(x_ref, g_ref, shift_ref, scale_ref, w_ref, qkg_ref, cos_ref, sin_ref,
                        out_ref, h_ref, acc0_ref, acc1_ref):
    j = pl.program_id(1)
    n_qk_tiles = 2 * D_MODEL // TN
    n_tiles = 3 * D_MODEL // TN

    def matmul_into(acc_ref):
        acc_ref[...] = jnp.dot(h_ref[...], w_ref[...], preferred_element_type=F32)

    def norm_rope_from(acc_ref):
        cos = cos_ref[...]
        sin = sin_ref[...]
        for grp in range(TN // LANES):
            sl = slice(grp * LANES, (grp + 1) * LANES)
            xg = acc_ref[:, sl]
            y = xg * lax.rsqrt(jnp.mean(xg * xg, axis=-1, keepdims=True) + EPS) * qkg_ref[:, sl]
            out_ref[:, sl] = (y * cos + pltpu.roll(y, LANES // 2, 1) * sin).astype(BF16)

    def cast_from(acc_ref):
        out_ref[...] = acc_ref[...].astype(BF16)

    @pl.when(j == 0)
    def _():
        h_ref[...] = _norm_modulate(x_ref[...], g_ref[...], shift_ref[...], scale_ref[...]).astype(BF16)
        matmul_into(acc0_ref)

    is_qk_finish = jnp.logical_and(j >= 1, j <= n_qk_tiles)

    @pl.when(jnp.logical_and(is_qk_finish, j % 2 == 1))
    def _():
        matmul_into(acc1_ref)
        norm_rope_from(acc0_ref)

    @pl.when(jnp.logical_and(is_qk_finish, j % 2 == 0))
    def _():
        matmul_into(acc0_ref)
        norm_rope_from(acc1_ref)

    assert n_qk_tiles % 2 == 0 and n_tiles == n_qk_tiles + 2

    @pl.when(j == n_tiles - 1)
    def _():
        matmul_into(acc1_ref)
        cast_from(acc0_ref)

    @pl.when(j == n_tiles)
    def _():
        cast_from(acc1_ref)


def _inproj_diff(xs, g, mods, layer, w_all, w_layer, qkg, cos_t, sin_t):
    n_qk_tiles = 2 * D_MODEL // TN
    n_tiles = 3 * D_MODEL // TN
    lat_tiles = SEQ // TM

    tile = _early_tile(NT // TM, 3)
    rope_tile = _early_tile(NT // TM, n_qk_tiles + 1)

    def rope_blk(i, j):
        t = rope_tile(i, j)
        return (jnp.where(t < N_LAT // TM, t % lat_tiles, lat_tiles), 0)

    return pl.pallas_call(
        _inproj_diff_kernel,
        grid=(NT // TM, n_tiles + 1),
        in_specs=[
            pl.BlockSpec((TM, D_MODEL), lambda i, j: (tile(i, j), 0)),
            pl.BlockSpec((1, D_MODEL), lambda i, j: (0, 0)),
            _mod_spec(layer, 0, tile=tile),
            _mod_spec(layer, 1, tile=tile),
            pl.BlockSpec((None, D_MODEL, TN), lambda i, j: (w_layer, 0, jnp.where(j == n_tiles, 0, j))),
            pl.BlockSpec((1, TN), lambda i, j: (0, jnp.clip(j - 1, 0, n_qk_tiles - 1))),
            pl.BlockSpec((TM, LANES), rope_blk),
            pl.BlockSpec((TM, LANES), rope_blk),
        ],
        out_specs=pl.BlockSpec((TM, TN), lambda i, j: (i, jnp.maximum(j - 1, 0))),
        out_shape=jax.ShapeDtypeStruct((NT, 3 * D_MODEL), BF16),
        scratch_shapes=[
            pltpu.VMEM((TM, D_MODEL), BF16),
            pltpu.VMEM((TM, TN), F32),
            pltpu.VMEM((TM, TN), F32),
        ],
        compiler_params=_params("arbitrary", "arbitrary"),
        name="inproj_diff",
    )(xs, g, mods, mods, w_all, qkg, cos_t, sin_t)


def _attn_kernel(*refs, n_lat_chunks, lam_init):
    if n_lat_chunks:
        (q_ref, kl_ref, vl_ref, kc_ref, vc_ref, lq1_ref, lk1_ref, lq2_ref, lk2_ref, sg_ref,
         out_ref, acc_ref, m_ref, l_ref) = refs
    else:
        (q_ref, kc_ref, vc_ref, lq1_ref, lk1_ref, lq2_ref, lk2_ref, sg_ref, _,
         out_ref, acc_ref, m_ref, l_ref) = refs
    tq = q_ref.shape[0]
    dv = 2 * DIFF_DH
    q = q_ref[...]

    def scores(kblk):
        return jnp.concatenate(
            [lax.dot_general(q[:, t * DIFF_DH:(t + 1) * DIFF_DH], kblk[:, t * DIFF_DH:(t + 1) * DIFF_DH],
                             NT_DIMS, preferred_element_type=F32) for t in range(2)], axis=0)

    blocks = [(scores(kc_ref[...]), vc_ref[...])]
    if n_lat_chunks:
        blocks.append((scores(kl_ref[:TK, :]), vl_ref[:TK, :]))
    m0 = functools.reduce(jnp.maximum, [jnp.max(s, axis=-1, keepdims=True) for s, _ in blocks])
    weights = [jnp.exp2(s - m0) for s, _ in blocks]
    m_ref[...] = jnp.broadcast_to(m0, m_ref.shape)
    l_ref[...] = jnp.broadcast_to(sum(jnp.sum(p, axis=-1, keepdims=True) for p in weights), l_ref.shape)
    acc_ref[...] = sum(jnp.dot(p.astype(BF16), v, preferred_element_type=F32)
                       for p, (_, v) in zip(weights, blocks))

    if n_lat_chunks:
        def body(c, carry):
            off = pl.multiple_of(c * TK, TK)
            s = scores(kl_ref[pl.ds(off, TK), :])
            m_old = m_ref[...]
            m_new = jnp.maximum(m_old, jnp.max(s, axis=-1, keepdims=True))
            p = jnp.exp2(s - _lane_repeat(m_new, TK // LANES, 1))
            alpha = jnp.exp2(m_old - m_new)
            l_ref[...] = alpha * l_ref[...] + jnp.sum(p, axis=-1, keepdims=True)
            acc_ref[...] = (_lane_repeat(alpha, dv // LANES, 1) * acc_ref[...]
                            + jnp.dot(p.astype(BF16), vl_ref[pl.ds(off, TK), :],
                                      preferred_element_type=F32))
            m_ref[...] = m_new
            return carry
        lax.fori_loop(1, n_lat_chunks, body, 0)

    lam = (jnp.exp(jnp.sum(lq1_ref[...] * lk1_ref[...], axis=-1, keepdims=True))
           - jnp.exp(jnp.sum(lq2_ref[...] * lk2_ref[...], axis=-1, keepdims=True)) + lam_init)
    o_all = acc_ref[...] * _lane_repeat(1.0 / l_ref[...], dv // LANES, 1)
    o = o_all[:tq] - lam * o_all[tq:]
    y = o * lax.rsqrt(jnp.mean(o * o, axis=-1, keepdims=True) + EPS) * sg_ref[...]
    out_ref[...] = (y * (1.0 - lam_init)).astype(BF16)


def _attention(qkv, lams, subln_g, lam_init, prev_out, out_rows=NT):
    dv = 2 * DIFF_DH
    kcol = D_MODEL // dv
    vcol = 2 * D_MODEL // dv
    ctx_blk0 = N_LAT // CTX_LEN
    small = [pl.BlockSpec((1, DIFF_DH), lambda b, h, i: (0, 0))] * 4
    small.append(pl.BlockSpec((1, dv), lambda b, h, i: (0, 0)))
    ctx_specs = [
        pl.BlockSpec((CTX_LEN, dv), lambda b, h, i: (ctx_blk0 + b, kcol + h)),
        pl.BlockSpec((CTX_LEN, dv), lambda b, h, i: (ctx_blk0 + b, vcol + h)),
    ]
    if prev_out is None:
        tq, nq = TQ, SEQ // TQ
        q_map = lambda b, h, i: (b * nq + i, h)
        in_specs = [
            pl.BlockSpec((tq, dv), q_map),
            pl.BlockSpec((SEQ, dv), lambda b, h, i: (b, kcol + h)),
            pl.BlockSpec((SEQ, dv), lambda b, h, i: (b, vcol + h)),
        ] + ctx_specs + small
        args = [qkv, qkv, qkv, qkv, qkv] + list(lams) + [subln_g]
        aliases = {}
        n_lat_chunks = SEQ // TK
    else:
        tq, nq = CTX_LEN, 1
        q_map = lambda b, h, i: (ctx_blk0 + b, h)
        in_specs = [pl.BlockSpec((tq, dv), q_map)] + ctx_specs + small
        in_specs.append(pl.BlockSpec(memory_space=pl.ANY))
        args = [qkv, qkv, qkv] + list(lams) + [subln_g, prev_out]
        aliases = {len(args) - 1: 0}
        n_lat_chunks = 0
    return pl.pallas_call(
        functools.partial(_attn_kernel, n_lat_chunks=n_lat_chunks, lam_init=lam_init),
        grid=(BATCH, DIFF_HEADS, nq),
        in_specs=in_specs,
        out_specs=pl.BlockSpec((tq, dv), q_map),
        out_shape=jax.ShapeDtypeStruct((out_rows, D_MODEL), BF16),
        scratch_shapes=[
            pltpu.VMEM((2 * tq, dv), F32),
            pltpu.VMEM((2 * tq, LANES), F32),
            pltpu.VMEM((2 * tq, LANES), F32),
        ],
        input_output_aliases=aliases,
        compiler_params=_params("arbitrary", "arbitrary", "arbitrary"),
        name="diff_attn_ctx" if n_lat_chunks == 0 else "diff_attn",
    )(*args)


def _ffn_kernel(a_ref, wo_ref, x_ref, g_ref, gate1_ref, shift_ref, scale_ref, gate2_ref, wgu_hbm, wd_hbm,
                out_ref, h_ref, wg_buf, wu_buf, wd_buf, sem, *, layer, n_tiles):
    i = pl.program_id(0)
    nh = FFN_HIDDEN // TH
    assert nh % 2 == 1

    def weight_copies(k, slot):
        col = pl.multiple_of(k * TH, TH)
        return (
            pltpu.make_async_copy(wgu_hbm.at[layer, :, pl.ds(col, TH)], wg_buf.at[slot], sem.at[0, slot]),
            pltpu.make_async_copy(wgu_hbm.at[layer, :, pl.ds(FFN_HIDDEN + col, TH)], wu_buf.at[slot],
                                  sem.at[1, slot]),
            pltpu.make_async_copy(wd_hbm.at[layer, pl.ds(col, TH), :], wd_buf.at[slot], sem.at[2, slot]),
        )

    def start(k, slot):
        for cp in weight_copies(k, slot):
            cp.start()

    def wait(k, slot):
        for cp in weight_copies(k, slot):
            cp.wait()

    def hidden_tile(h, base, slot):
        gt = jnp.dot(h, wg_buf[slot], preferred_element_type=F32)
        up = jnp.dot(h, wu_buf[slot], preferred_element_type=F32)
        act = (gt * jax.nn.sigmoid(gt) * up).astype(BF16)
        out_ref[...] = base + gate2_ref[...] * jnp.dot(act, wd_buf[slot], preferred_element_type=F32)

    slot0 = i % 2

    @pl.when(i == 0)
    def _():
        start(0, 0)

    wait(0, slot0)
    start(1, 1 - slot0)
    x1 = x_ref[...] + gate1_ref[...] * jnp.dot(a_ref[...], wo_ref[...], preferred_element_type=F32)
    h = _norm_modulate(x1, g_ref[...], shift_ref[...], scale_ref[...]).astype(BF16)
    h_ref[...] = h
    hidden_tile(h, x1, slot0)

    def body(k, carry):
        slot = (i + k) % 2
        wait(k, slot)

        @pl.when(k + 1 < nh)
        def _():
            start(k + 1, 1 - slot)

        @pl.when(jnp.logical_and(k + 1 == nh, i + 1 < n_tiles))
        def _():
            start(0, 1 - slot)

        hidden_tile(h_ref[...], out_ref[...], slot)
        return carry

    lax.fori_loop(1, nh, body, 0)


def _ffn(a, w_out, w_out_layer, xs, g, mods, layer, w_gu, w_down, n_tiles):
    row_tile = lambda i: (i, 0)
    mod_spec = lambda chunk: pl.BlockSpec((None, None, 1, D_MODEL),
                                          lambda i: (layer, _mod_row(i, TMF), 0, chunk))
    return pl.pallas_call(
        functools.partial(_ffn_kernel, layer=layer, n_tiles=n_tiles),
        grid=(n_tiles,),
        in_specs=[
            pl.BlockSpec((TMF, D_MODEL), row_tile),
            pl.BlockSpec((None, D_MODEL, D_MODEL), lambda i: (w_out_layer, 0, 0),
                         pipeline_mode=pl.Buffered(1)),
            pl.BlockSpec((TMF, D_MODEL), row_tile),
            pl.BlockSpec((1, D_MODEL), lambda i: (0, 0)),
            mod_spec(2),
            mod_spec(3),
            mod_spec(4),
            mod_spec(5),
            pl.BlockSpec(memory_space=pl.ANY),
            pl.BlockSpec(memory_space=pl.ANY),
        ],
        out_specs=pl.BlockSpec((TMF, D_MODEL), row_tile),
        out_shape=jax.ShapeDtypeStruct((n_tiles * TMF, D_MODEL), F32),
        scratch_shapes=[
            pltpu.VMEM((TMF, D_MODEL), BF16),
            pltpu.VMEM((2, D_MODEL, TH), BF16),
            pltpu.VMEM((2, D_MODEL, TH), BF16),
            pltpu.VMEM((2, TH, D_MODEL), BF16),
            pltpu.SemaphoreType.DMA((3, 2)),
        ],
        compiler_params=_params("arbitrary"),
        name="ffn",
    )(a, w_out, xs, g, mods, mods, mods, mods, w_gu, w_down)


def _pair_major(a):
    n_freq = DIFF_DH // 4
    lead = a.shape[:-1]
    a = a.reshape(*lead, a.shape[-1] // DIFF_DH, 2, 2, n_freq)
    return jnp.swapaxes(a, -3, -2).reshape(*lead, -1)


def _rope_tables():
    n_freq = DIFF_DH // 4
    pos = jnp.arange(SEQ)
    freqs = ROPE_BASE ** (-jnp.arange(n_freq, dtype=F32) / n_freq)
    ang = jnp.stack([pos // GRID_W, pos % GRID_W], axis=-1).astype(F32)[:, :, None] * freqs
    cos, sin = jnp.cos(ang), jnp.sin(ang)
    cos_t = jnp.concatenate([cos[:, 0], cos[:, 1], cos[:, 0], cos[:, 1]], axis=-1)
    sin_t = jnp.concatenate([-sin[:, 0], -sin[:, 1], sin[:, 0], sin[:, 1]], axis=-1)
    cos_t = jnp.concatenate([cos_t, jnp.ones((TM, LANES), F32)], axis=0)
    sin_t = jnp.concatenate([sin_t, jnp.zeros((TM, LANES), F32)], axis=0)
    return cos_t, sin_t


def kernel(x, c, ctx, c_ctx, ada_w, ada_b, norm_g, mlstm_w_in, mlstm_gate_b, mlstm_head_g, mlstm_w_out,
           diff_w_in, diff_w_out, diff_q_g, diff_k_g, diff_lq1, diff_lk1, diff_lq2, diff_lk2, diff_subln_g,
           ffn_w_gu, ffn_w_down):
    assert x.shape == (BATCH, SEQ, D_MODEL) and ctx.shape == (BATCH, CTX_LEN, D_MODEL)
    xs = jnp.concatenate([x.reshape(N_LAT, D_MODEL), ctx.reshape(N_CTX, D_MODEL)], axis=0)
    cc = jnp.concatenate([c, c_ctx[None], jnp.zeros((8 - BATCH - 1, D_MODEL), F32)], axis=0)
    mods = _ada(cc, ada_w, ada_b).reshape(DEPTH, 8, 1, 6 * D_MODEL)
    cos_t, sin_t = _rope_tables()
    row = lambda v: v.reshape(1, -1)
    ffn_gu = ffn_w_gu.astype(BF16)
    ffn_down = ffn_w_down.astype(BF16)
    mlstm_in = mlstm_w_in.astype(BF16)
    mlstm_kt = jnp.swapaxes(mlstm_w_in[:, :, MLSTM_QK:2 * MLSTM_QK], 1, 2).astype(BF16)
    mlstm_out = mlstm_w_out.astype(BF16)
    diff_in = jnp.concatenate([_pair_major(diff_w_in[..., :2 * D_MODEL]), diff_w_in[..., 2 * D_MODEL:]],
                              axis=-1).astype(BF16)
    diff_out = diff_w_out.astype(BF16)

    for i in range(DEPTH):
        last = i == DEPTH - 1
        n_rows = N_LAT if last else NT
        j = i // N_MIXERS
        if i % N_MIXERS == 0:
            w_gate = jnp.pad(mlstm_w_in[j, :, MLSTM_MAIN:], ((0, 0), (0, LANES - N_GATES))).astype(BF16)
            gate_b = jnp.pad(mlstm_gate_b[j], (0, LANES - N_GATES)).reshape(1, LANES)
            act, k_t, gates = _inproj_mlstm(xs, row(norm_g[i, 0]), mods, i, mlstm_in, mlstm_kt, j, w_gate)
            hf = _mlstm_scan(act, k_t, gates, gate_b, None, None, reverse=False)
            a = _mlstm_scan(act, k_t, gates, gate_b, row(mlstm_head_g[j]), hf, reverse=True)
            w_out = mlstm_out
        else:
            lam_init = 0.8 - 0.6 * math.exp(-0.3 * i)
            q_gain = _pair_major(diff_q_g[j]) * (DIFF_DH ** -0.5 * math.log2(math.e))
            qkg = jnp.concatenate([jnp.tile(q_gain, 2 * DIFF_HEADS),
                                   jnp.tile(_pair_major(diff_k_g[j]), 2 * DIFF_HEADS)]).reshape(1, 2 * D_MODEL)
            qkv = _inproj_diff(xs, row(norm_g[i, 0]), mods, i, diff_in, j, qkg, cos_t, sin_t)
            lams = [row(diff_lq1[j]), row(diff_lk1[j]), row(diff_lq2[j]), row(diff_lk2[j])]
            a = _attention(qkv, lams, row(diff_subln_g[j]), lam_init, None, n_rows)
            if not last:
                a = _attention(qkv, lams, row(diff_subln_g[j]), lam_init, a)
            w_out = diff_out
        xs = _ffn(a, w_out, j, xs, row(norm_g[i, 1]), mods, i, ffn_gu, ffn_down, n_rows // TMF)
    return xs.reshape(BATCH, SEQ, D_MODEL)
```

```python
import functools
import math

import jax
import jax.numpy as jnp
from jax import lax
from jax.experimental import pallas as pl
from jax.experimental.pallas import tpu as pltpu

F32 = jnp.float32
BF16 = jnp.bfloat16

D_MODEL = 2048
BATCH = 4
SEQ = 4096
DEPTH = 4
GRID_W = 64
CTX_LEN = 256
N_MIXERS = 2

MLSTM_HEADS = 8
MLSTM_DK = D_MODEL // (2 * MLSTM_HEADS)
MLSTM_DV = D_MODEL // MLSTM_HEADS
GATE_CAP = 15.0
MLSTM_QK = MLSTM_HEADS * MLSTM_DK
MLSTM_MAIN = 2 * MLSTM_QK + 2 * D_MODEL
N_GATES = 4 * MLSTM_HEADS

DIFF_HEADS = 8
DIFF_DH = D_MODEL // (2 * DIFF_HEADS)
ROPE_BASE = 10000.0

FFN_HIDDEN = -(-(8 * D_MODEL) // (3 * 256)) * 256
EPS = 1e-6

N_LAT = BATCH * SEQ
N_CTX = BATCH * CTX_LEN
NT = N_LAT + N_CTX

LANES = 128
TM = 1024
TN = 1024
TMF = 512
TH = 512
CHUNK = 256
TQ = 1024
TK = 1024
VMEM_LIMIT = 56 * 1024 * 1024

NT_DIMS = (((1,), (1,)), ((), ()))


def _params(*sem):
    return pltpu.CompilerParams(dimension_semantics=sem, vmem_limit_bytes=VMEM_LIMIT)


def _mod_row(i, tm):
    return jnp.minimum(i // (SEQ // tm), BATCH)


def _mod_spec(layer, k, tm=TM, tile=lambda i, j: i):
    return pl.BlockSpec((None, None, 1, D_MODEL), lambda i, j: (layer, _mod_row(tile(i, j), tm), 0, k))


def _early_tile(n_tiles, from_step):
    return lambda i, j: jnp.minimum(i + jnp.where(j >= from_step, 1, 0), n_tiles - 1)


def _lane_repeat(x, n, axis):
    assert axis == 1
    return jnp.concatenate([x] * n, axis=1) if n > 1 else x


def _norm_modulate(x, g, shift, scale):
    y = x * lax.rsqrt(jnp.mean(x * x, axis=-1, keepdims=True) + EPS) * g
    return y * (1.0 + scale) + shift


def _ada_kernel(c_ref, w_ref, b_ref, o_ref):
    c = c_ref[...]
    a = (c * jax.nn.sigmoid(c)).astype(BF16)
    o_ref[...] = jnp.dot(a, w_ref[...].astype(BF16), preferred_element_type=F32) + b_ref[...]


def _ada(cc, ada_w, ada_b):
    tn = 1024
    return pl.pallas_call(
        _ada_kernel,
        grid=(DEPTH, 6 * D_MODEL // tn),
        in_specs=[
            pl.BlockSpec((8, D_MODEL), lambda l, j: (0, 0)),
            pl.BlockSpec((None, D_MODEL, tn), lambda l, j: (l, 0, j)),
            pl.BlockSpec((None, 1, tn), lambda l, j: (l, 0, j)),
        ],
        out_specs=pl.BlockSpec((None, 8, tn), lambda l, j: (l, 0, j)),
        out_shape=jax.ShapeDtypeStruct((DEPTH, 8, 6 * D_MODEL), F32),
        compiler_params=_params("arbitrary", "arbitrary"),
        name="ada_mod",
    )(cc, ada_w, ada_b.reshape(DEPTH, 1, 6 * D_MODEL))


MLSTM_ACT = MLSTM_MAIN - MLSTM_QK
Q_TILE = MLSTM_ACT // TN - 1


def _inproj_mlstm_kernel(x_ref, g_ref, shift_ref, scale_ref, w_ref, wkt_ref, wg_ref,
                         out_ref, kt_ref, gate_ref, h_ref):
    j = pl.program_id(1)

    @pl.when(j == 0)
    def _():
        h_ref[...] = _norm_modulate(x_ref[...], g_ref[...], shift_ref[...], scale_ref[...]).astype(BF16)
        acc = jnp.dot(h_ref[...], w_ref[...], preferred_element_type=F32)
        out_ref[...] = (acc * MLSTM_DK ** -0.5).astype(BF16)
        gate_ref[...] = jnp.dot(h_ref[...], wg_ref[...], preferred_element_type=F32)

    @pl.when(j == 1)
    def _():
        kt_ref[...] = lax.dot_general(wkt_ref[...], h_ref[...], NT_DIMS,
                                      preferred_element_type=F32).astype(BF16)

    @pl.when(j > 1)
    def _():
        out_ref[...] = jnp.dot(h_ref[...], w_ref[...], preferred_element_type=F32).astype(BF16)


def _inproj_mlstm(xs, g, mods, layer, w_all, wkt_all, w_layer, w_gate):
    assert TN == MLSTM_QK
    tile = _early_tile(NT // TM, 3)
    return pl.pallas_call(
        _inproj_mlstm_kernel,
        grid=(NT // TM, MLSTM_MAIN // TN),
        in_specs=[
            pl.BlockSpec((TM, D_MODEL), lambda i, j: (tile(i, j), 0)),
            pl.BlockSpec((1, D_MODEL), lambda i, j: (0, 0)),
            _mod_spec(layer, 0, tile=tile),
            _mod_spec(layer, 1, tile=tile),
            pl.BlockSpec((None, D_MODEL, TN), lambda i, j: (w_layer, 0, jnp.where(j == 1, 0, j))),
            pl.BlockSpec((None, MLSTM_QK, D_MODEL), lambda i, j: (w_layer, 0, 0),
                         pipeline_mode=pl.Buffered(1)),
            pl.BlockSpec((D_MODEL, LANES), lambda i, j: (0, 0)),
        ],
        out_specs=[
            pl.BlockSpec((TM, TN), lambda i, j: (i, jnp.where(j <= 1, Q_TILE, j - 2))),
            pl.BlockSpec((MLSTM_QK, TM), lambda i, j: (0, i)),
            pl.BlockSpec((TM, LANES), lambda i, j: (i, 0)),
        ],
        out_shape=[
            jax.ShapeDtypeStruct((NT, MLSTM_ACT), BF16),
            jax.ShapeDtypeStruct((MLSTM_QK, NT), BF16),
            jax.ShapeDtypeStruct((NT, LANES), F32),
        ],
        scratch_shapes=[pltpu.VMEM((TM, D_MODEL), BF16)],
        compiler_params=_params("arbitrary", "arbitrary"),
        name="inproj_mlstm",
    )(xs, g, mods, mods, w_all, wkt_all, w_gate)


def _mlstm_scan_kernel(*refs, reverse, final):
    if final:
        (q_ref, kt_ref, v_ref, gate_ref, gb_ref, o_ref, hf_ref, hg_ref,
         out_ref, c_ref, n_ref, m_ref) = refs
    else:
        q_ref, kt_ref, v_ref, gate_ref, gb_ref, out_ref, c_ref, n_ref, m_ref = refs
    L = CHUNK

    @pl.when(pl.program_id(1) == 0)
    def _():
        c_ref[...] = jnp.zeros_like(c_ref)
        n_ref[...] = jnp.zeros_like(n_ref)
        m_ref[...] = jnp.zeros_like(m_ref)

    gg = GATE_CAP * jnp.tanh((gate_ref[...] + gb_ref[...]) / GATE_CAP)
    lsig = jax.nn.log_sigmoid(gg)
    t_idx = lax.broadcasted_iota(jnp.int32, (L, L), 0)
    s_idx = lax.broadcasted_iota(jnp.int32, (L, L), 1)
    mask = (s_idx >= t_idx) if reverse else (s_idx <= t_idx)
    tri = mask.astype(BF16)
    hi = lsig.astype(BF16)
    rest = lsig - hi.astype(F32)
    mid = rest.astype(BF16)
    lo = (rest - mid.astype(F32)).astype(BF16)
    bcum = (jnp.dot(tri, hi, preferred_element_type=F32) + jnp.dot(tri, mid, preferred_element_type=F32)
            + jnp.dot(tri, lo, preferred_element_type=F32))
    bal = pltpu.roll(bcum, LANES - MLSTM_HEADS, 1)
    r_t = (gg - bal).T
    base = 2 * MLSTM_HEADS if reverse else 0
    last = 0 if reverse else L - 1

    for h in range(MLSTM_HEADS):
        col = base + h
        b_rep = jnp.broadcast_to(bal[:, col:col + 1], (L, LANES))
        r_row = r_t[col:col + 1, :]
        m = m_ref[h]
        m11 = m[:, 0:1]
        b_last = b_rep[last:last + 1, 0:1]

        dmat = jnp.where(mask, _lane_repeat(b_rep, L // LANES, 1) + r_row, -jnp.inf)
        inter = b_rep + m
        m_t = jnp.maximum(inter, jnp.max(dmat, axis=-1, keepdims=True))
        w_intra = jnp.exp(dmat - _lane_repeat(m_t, L // LANES, 1))
        w_inter = jnp.exp(inter - m_t)

        qh = q_ref[:, h * MLSTM_DK:(h + 1) * MLSTM_DK]
        kth = kt_ref[h * MLSTM_DK:(h + 1) * MLSTM_DK, :]
        vh = v_ref[:, h * MLSTM_DV:(h + 1) * MLSTM_DV]
        s = jnp.dot(qh, kth, preferred_element_type=F32) * w_intra
        c_old = c_ref[h]
        num = (_lane_repeat(w_inter, MLSTM_DV // LANES, 1)
               * jnp.dot(qh, c_old.astype(BF16), preferred_element_type=F32)
               + jnp.dot(s.astype(BF16), vh, preferred_element_type=F32))
        n_old = n_ref[h]
        qn = jnp.dot(qh, n_old.astype(BF16), preferred_element_type=F32)
        den = w_inter * qn + jnp.sum(s, axis=-1, keepdims=True)
        inv = 1.0 / jnp.maximum(jnp.abs(den), jnp.exp(-m_t))
        hout = num * _lane_repeat(inv, MLSTM_DV // LANES, 1)

        a_row = b_last + r_row
        m_new = jnp.maximum(b_last + m11, jnp.max(a_row, axis=-1, keepdims=True))
        wk = jnp.exp(a_row - m_new)
        dec = jnp.exp(b_last + m11 - m_new)
        kw_t = kth.astype(F32) * wk
        c_ref[h] = dec * c_old + jnp.dot(kw_t.astype(BF16), vh, preferred_element_type=F32)
        n_ref[h] = dec * n_old + jnp.sum(kw_t, axis=-1, keepdims=True)
        m_ref[h] = jnp.broadcast_to(m_new, (1, LANES))

        sl = slice(h * MLSTM_DV, (h + 1) * MLSTM_DV)
        if final:
            hs = hf_ref[:, sl] + hout
            y = hs * lax.rsqrt(jnp.mean(hs * hs, axis=-1, keepdims=True) + EPS) * hg_ref[:, sl]
            out_ref[:, sl] = (y * jax.nn.sigmoid(o_ref[:, sl].astype(F32))).astype(BF16)
        else:
            out_ref[:, sl] = hout


def _mlstm_scan(act, k_t, gates, gate_b, head_g, hf, *, reverse):
    final = hf is not None
    n_lat_chunks = SEQ // CHUNK
    ctx_blk0 = N_LAT // CHUNK

    def row_blk(b, c):
        j = (n_lat_chunks - c) if reverse else (c - 1)
        return jnp.where(c == 0, ctx_blk0 + b, b * n_lat_chunks + j)

    in_specs = [
        pl.BlockSpec((CHUNK, MLSTM_QK), lambda b, c: (row_blk(b, c), Q_TILE)),
        pl.BlockSpec((MLSTM_QK, CHUNK), lambda b, c: (0, row_blk(b, c))),
        pl.BlockSpec((CHUNK, D_MODEL), lambda b, c: (row_blk(b, c), 0)),
        pl.BlockSpec((CHUNK, LANES), lambda b, c: (row_blk(b, c), 0)),
        pl.BlockSpec((1, LANES), lambda b, c: (0, 0)),
    ]
    args = [act, k_t, act, gates, gate_b]
    if final:
        in_specs += [
            pl.BlockSpec((CHUNK, D_MODEL), lambda b, c: (row_blk(b, c), 1)),
            pl.BlockSpec((CHUNK, D_MODEL), lambda b, c: (row_blk(b, c), 0)),
            pl.BlockSpec((1, D_MODEL), lambda b, c: (0, 0)),
        ]
        args += [act, hf, head_g]
    return pl.pallas_call(
        functools.partial(_mlstm_scan_kernel, reverse=reverse, final=final),
        grid=(BATCH, 1 + n_lat_chunks),
        in_specs=in_specs,
        out_specs=pl.BlockSpec((CHUNK, D_MODEL), lambda b, c: (row_blk(b, c), 0)),
        out_shape=jax.ShapeDtypeStruct((NT, D_MODEL), BF16 if final else F32),
        scratch_shapes=[
            pltpu.VMEM((MLSTM_HEADS, MLSTM_DK, MLSTM_DV), F32),
            pltpu.VMEM((MLSTM_HEADS, MLSTM_DK, LANES), F32),
            pltpu.VMEM((MLSTM_HEADS, 1, LANES), F32),
        ],
        compiler_params=_params("arbitrary", "arbitrary"),
        name="mlstm_scan_bwd" if reverse else "mlstm_scan_fwd",
    )(*args)


def _inproj_diff_kernel(x_ref, g_ref, shift_ref, scale_ref, w_ref, qkg_ref, cos_ref, sin_ref,
                        out_ref, h_ref, acc0_ref, acc1_ref):
    j = pl.program_id(1)
    n_qk_tiles = 2 * D_MODEL // TN
    n_tiles = 3 * D_MODEL // TN

    def matmul_into(acc_ref):
        acc_ref[...] = jnp.dot(h_ref[...], w_ref[...], preferred_element_type=F32)

    def norm_rope_from(acc_ref):
        cos = cos_ref[...]
        sin = sin_ref[...]
        for grp in range(TN // LANES):
            sl = slice(grp * LANES, (grp + 1) * LANES)
            xg = acc_ref[:, sl]
            y = xg * lax.rsqrt(jnp.mean(xg * xg, axis=-1, keepdims=True) + EPS) * qkg_ref[:, sl]
            out_ref[:, sl] = (y * cos + pltpu.roll(y, LANES // 2, 1) * sin).astype(BF16)

    def cast_from(acc_ref):
        out_ref[...] = acc_ref[...].astype(BF16)

    @pl.when(j == 0)
    def _():
        h_ref[...] = _norm_modulate(x_ref[...], g_ref[...], shift_ref[...], scale_ref[...]).astype(BF16)
        matmul_into(acc0_ref)

    is_qk_finish = jnp.logical_and(j >= 1, j <= n_qk_tiles)

    @pl.when(jnp.logical_and(is_qk_finish, j % 2 == 1))
    def _():
        matmul_into(acc1_ref)
        norm_rope_from(acc0_ref)

    @pl.when(jnp.logical_and(is_qk_finish, j % 2 == 0))
    def _():
        matmul_into(acc0_ref)
        norm_rope_from(acc1_ref)

    assert n_qk_tiles % 2 == 0 and n_tiles == n_qk_tiles + 2

    @pl.when(j == n_tiles - 1)
    def _():
        matmul_into(acc1_ref)
        cast_from(acc0_ref)

    @pl.when(j == n_tiles)
    def _():
        cast_from(acc1_ref)


def _inproj_diff(xs, g, mods, layer, w_all, w_layer, qkg, cos_t, sin_t):
    n_qk_tiles = 2 * D_MODEL // TN
    n_tiles = 3 * D_MODEL // TN
    lat_tiles = SEQ // TM

    tile = _early_tile(NT // TM, 3)
    rope_tile = _early_tile(NT // TM, n_qk_tiles + 1)

    def rope_blk(i, j):
        t = rope_tile(i, j)
        return (jnp.where(t < N_LAT // TM, t % lat_tiles, lat_tiles), 0)

    return pl.pallas_call(
        _inproj_diff_kernel,
        grid=(NT // TM, n_tiles + 1),
        in_specs=[
            pl.BlockSpec((TM, D_MODEL), lambda i, j: (tile(i, j), 0)),
            pl.BlockSpec((1, D_MODEL), lambda i, j: (0, 0)),
            _mod_spec(layer, 0, tile=tile),
            _mod_spec(layer, 1, tile=tile),
            pl.BlockSpec((None, D_MODEL, TN), lambda i, j: (w_layer, 0, jnp.where(j == n_tiles, 0, j))),
            pl.BlockSpec((1, TN), lambda i, j: (0, jnp.clip(j - 1, 0, n_qk_tiles - 1))),
            pl.BlockSpec((TM, LANES), rope_blk),
            pl.BlockSpec((TM, LANES), rope_blk),
        ],
        out_specs=pl.BlockSpec((TM, TN), lambda i, j: (i, jnp.maximum(j - 1, 0))),
        out_shape=jax.ShapeDtypeStruct((NT, 3 * D_MODEL), BF16),
        scratch_shapes=[
            pltpu.VMEM((TM, D_MODEL), BF16),
            pltpu.VMEM((TM, TN), F32),
            pltpu.VMEM((TM, TN), F32),
        ],
        compiler_params=_params("arbitrary", "arbitrary"),
        name="inproj_diff",
    )(xs, g, mods, mods, w_all, qkg, cos_t, sin_t)


def _attn_kernel(*refs, n_lat_chunks, lam_init, tq):
    if n_lat_chunks:
        (q_ref, kl_ref, vl_ref, kc_ref, vc_ref, lq1_ref, lk1_ref, lq2_ref, lk2_ref, sg_ref,
         out_ref, acc_ref, m_ref, l_ref) = refs
    else:
        (q_ref, kc_ref, vc_ref, lq1_ref, lk1_ref, lq2_ref, lk2_ref, sg_ref, _,
         out_ref, acc_ref, m_ref, l_ref) = refs
    dv = 2 * DIFF_DH
    lam = (jnp.exp(jnp.sum(lq1_ref[...] * lk1_ref[...], axis=-1, keepdims=True))
           - jnp.exp(jnp.sum(lq2_ref[...] * lk2_ref[...], axis=-1, keepdims=True)) + lam_init)

    def query_tile(t, carry):
        rows = pl.ds(pl.multiple_of(t * tq, tq), tq)
        q = q_ref[rows, :]

        def scores(kblk):
            return jnp.concatenate(
                [lax.dot_general(q[:, t2 * DIFF_DH:(t2 + 1) * DIFF_DH], kblk[:, t2 * DIFF_DH:(t2 + 1) * DIFF_DH],
                                 NT_DIMS, preferred_element_type=F32) for t2 in range(2)], axis=0)

        blocks = [(scores(kc_ref[...]), vc_ref[...])]
        if n_lat_chunks:
            blocks.append((scores(kl_ref[:TK, :]), vl_ref[:TK, :]))
        m0 = functools.reduce(jnp.maximum, [jnp.max(s, axis=-1, keepdims=True) for s, _ in blocks])
        weights = [jnp.exp2(s - m0) for s, _ in blocks]
        m_ref[...] = jnp.broadcast_to(m0, m_ref.shape)
        l_ref[...] = jnp.broadcast_to(sum(jnp.sum(p, axis=-1, keepdims=True) for p in weights), l_ref.shape)
        acc_ref[...] = sum(jnp.dot(p.astype(BF16), v, preferred_element_type=F32)
                           for p, (_, v) in zip(weights, blocks))

        if n_lat_chunks:
            def body(c, carry):
                off = pl.multiple_of(c * TK, TK)
                s = scores(kl_ref[pl.ds(off, TK), :])
                m_old = m_ref[...]
                m_new = jnp.maximum(m_old, jnp.max(s, axis=-1, keepdims=True))
                p = jnp.exp2(s - _lane_repeat(m_new, TK // LANES, 1))
                alpha = jnp.exp2(m_old - m_new)
                l_ref[...] = alpha * l_ref[...] + jnp.sum(p, axis=-1, keepdims=True)
                acc_ref[...] = (_lane_repeat(alpha, dv // LANES, 1) * acc_ref[...]
                                + jnp.dot(p.astype(BF16), vl_ref[pl.ds(off, TK), :],
                                          preferred_element_type=F32))
                m_ref[...] = m_new
                return carry
            lax.fori_loop(1, n_lat_chunks, body, 0)

        o_all = acc_ref[...] * _lane_repeat(1.0 / l_ref[...], dv // LANES, 1)
        o = o_all[:tq] - lam * o_all[tq:]
        y = o * lax.rsqrt(jnp.mean(o * o, axis=-1, keepdims=True) + EPS) * sg_ref[...]
        out_ref[rows, :] = (y * (1.0 - lam_init)).astype(BF16)
        return carry

    lax.fori_loop(0, q_ref.shape[0] // tq, query_tile, 0)


def _attention(qkv, lams, subln_g, lam_init, prev_out, out_rows=NT):
    dv = 2 * DIFF_DH
    kcol = D_MODEL // dv
    vcol = 2 * D_MODEL // dv
    ctx_blk0 = N_LAT // CTX_LEN
    small = [pl.BlockSpec((1, DIFF_DH), lambda b, h: (0, 0))] * 4
    small.append(pl.BlockSpec((1, dv), lambda b, h: (0, 0)))
    ctx_specs = [
        pl.BlockSpec((CTX_LEN, dv), lambda b, h: (ctx_blk0 + b, kcol + h)),
        pl.BlockSpec((CTX_LEN, dv), lambda b, h: (ctx_blk0 + b, vcol + h)),
    ]
    if prev_out is None:
        tq, q_rows = TQ, SEQ
        q_map = lambda b, h: (b, h)
        in_specs = [
            pl.BlockSpec((q_rows, dv), q_map),
            pl.BlockSpec((SEQ, dv), lambda b, h: (b, kcol + h)),
            pl.BlockSpec((SEQ, dv), lambda b, h: (b, vcol + h)),
        ] + ctx_specs + small
        args = [qkv, qkv, qkv, qkv, qkv] + list(lams) + [subln_g]
        aliases = {}
        n_lat_chunks = SEQ // TK
    else:
        tq, q_rows = CTX_LEN, CTX_LEN
        q_map = lambda b, h: (ctx_blk0 + b, h)
        in_specs = [pl.BlockSpec((q_rows, dv), q_map)] + ctx_specs + small
        in_specs.append(pl.BlockSpec(memory_space=pl.ANY))
        args = [qkv, qkv, qkv] + list(lams) + [subln_g, prev_out]
        aliases = {len(args) - 1: 0}
        n_lat_chunks = 0
    return pl.pallas_call(
        functools.partial(_attn_kernel, n_lat_chunks=n_lat_chunks, lam_init=lam_init, tq=tq),
        grid=(BATCH, DIFF_HEADS),
        in_specs=in_specs,
        out_specs=pl.BlockSpec((q_rows, dv), q_map),
        out_shape=jax.ShapeDtypeStruct((out_rows, D_MODEL), BF16),
        scratch_shapes=[
            pltpu.VMEM((2 * tq, dv), F32),
            pltpu.VMEM((2 * tq, LANES), F32),
            pltpu.VMEM((2 * tq, LANES), F32),
        ],
        input_output_aliases=aliases,
        compiler_params=_params("arbitrary", "arbitrary"),
        name="diff_attn_ctx" if n_lat_chunks == 0 else "diff_attn",
    )(*args)


def _ffn_kernel(a_ref, wo_ref, x_ref, g_ref, gate1_ref, shift_ref, scale_ref, gate2_ref, wgu_hbm, wd_hbm,
                out_ref, h_ref, wg_buf, wu_buf, wd_buf, sem, *, layer, n_tiles):
    i = pl.program_id(0)
    nh = FFN_HIDDEN // TH
    assert nh % 2 == 1

    def weight_copies(k, slot):
        col = pl.multiple_of(k * TH, TH)
        return (
            pltpu.make_async_copy(wgu_hbm.at[layer, :, pl.ds(col, TH)], wg_buf.at[slot], sem.at[0, slot]),
            pltpu.make_async_copy(wgu_hbm.at[layer, :, pl.ds(FFN_HIDDEN + col, TH)], wu_buf.at[slot],
                                  sem.at[1, slot]),
            pltpu.make_async_copy(wd_hbm.at[layer, pl.ds(col, TH), :], wd_buf.at[slot], sem.at[2, slot]),
        )

    def start(k, slot):
        for cp in weight_copies(k, slot):
            cp.start()

    def wait(k, slot):
        for cp in weight_copies(k, slot):
            cp.wait()

    def hidden_tile(h, base, slot):
        gt = jnp.dot(h, wg_buf[slot], preferred_element_type=F32)
        up = jnp.dot(h, wu_buf[slot], preferred_element_type=F32)
        act = (gt * jax.nn.sigmoid(gt) * up).astype(BF16)
        out_ref[...] = base + gate2_ref[...] * jnp.dot(act, wd_buf[slot], preferred_element_type=F32)

    slot0 = i % 2

    @pl.when(i == 0)
    def _():
        start(0, 0)

    wait(0, slot0)
    start(1, 1 - slot0)
    x1 = x_ref[...] + gate1_ref[...] * jnp.dot(a_ref[...], wo_ref[...], preferred_element_type=F32)
    h = _norm_modulate(x1, g_ref[...], shift_ref[...], scale_ref[...]).astype(BF16)
    h_ref[...] = h
    hidden_tile(h, x1, slot0)

    def body(k, carry):
        slot = (i + k) % 2
        wait(k, slot)

        @pl.when(k + 1 < nh)
        def _():
            start(k + 1, 1 - slot)

        @pl.when(jnp.logical_and(k + 1 == nh, i + 1 < n_tiles))
        def _():
            start(0, 1 - slot)

        hidden_tile(h_ref[...], out_ref[...], slot)
        return carry

    lax.fori_loop(1, nh, body, 0)


def _ffn(a, w_out, w_out_layer, xs, g, mods, layer, w_gu, w_down, n_tiles):
    row_tile = lambda i: (i, 0)
    mod_spec = lambda chunk: pl.BlockSpec((None, None, 1, D_MODEL),
                                          lambda i: (layer, _mod_row(i, TMF), 0, chunk))
    return pl.pallas_call(
        functools.partial(_ffn_kernel, layer=layer, n_tiles=n_tiles),
        grid=(n_tiles,),
        in_specs=[
            pl.BlockSpec((TMF, D_MODEL), row_tile),
            pl.BlockSpec((None, D_MODEL, D_MODEL), lambda i: (w_out_layer, 0, 0),
                         pipeline_mode=pl.Buffered(1)),
            pl.BlockSpec((TMF, D_MODEL), row_tile),
            pl.BlockSpec((1, D_MODEL), lambda i: (0, 0)),
            mod_spec(2),
            mod_spec(3),
            mod_spec(4),
            mod_spec(5),
            pl.BlockSpec(memory_space=pl.ANY),
            pl.BlockSpec(memory_space=pl.ANY),
        ],
        out_specs=pl.BlockSpec((TMF, D_MODEL), row_tile),
        out_shape=jax.ShapeDtypeStruct((n_tiles * TMF, D_MODEL), F32),
        scratch_shapes=[
            pltpu.VMEM((TMF, D_MODEL), BF16),
            pltpu.VMEM((2, D_MODEL, TH), BF16),
            pltpu.VMEM((2, D_MODEL, TH), BF16),
            pltpu.VMEM((2, TH, D_MODEL), BF16),
            pltpu.SemaphoreType.DMA((3, 2)),
        ],
        compiler_params=_params("arbitrary"),
        name="ffn",
    )(a, w_out, xs, g, mods, mods, mods, mods, w_gu, w_down)


def _pair_major(a):
    n_freq = DIFF_DH // 4
    lead = a.shape[:-1]
    a = a.reshape(*lead, a.shape[-1] // DIFF_DH, 2, 2, n_freq)
    return jnp.swapaxes(a, -3, -2).reshape(*lead, -1)


def _rope_tables():
    n_freq = DIFF_DH // 4
    pos = jnp.arange(SEQ)
    freqs = ROPE_BASE ** (-jnp.arange(n_freq, dtype=F32) / n_freq)
    ang = jnp.stack([pos // GRID_W, pos % GRID_W], axis=-1).astype(F32)[:, :, None] * freqs
    cos, sin = jnp.cos(ang), jnp.sin(ang)
    cos_t = jnp.concatenate([cos[:, 0], cos[:, 1], cos[:, 0], cos[:, 1]], axis=-1)
    sin_t = jnp.concatenate([-sin[:, 0], -sin[:, 1], sin[:, 0], sin[:, 1]], axis=-1)
    cos_t = jnp.concatenate([cos_t, jnp.ones((TM, LANES), F32)], axis=0)
    sin_t = jnp.concatenate([sin_t, jnp.zeros((TM, LANES), F32)], axis=0)
    return cos_t, sin_t


def kernel(x, c, ctx, c_ctx, ada_w, ada_b, norm_g, mlstm_w_in, mlstm_gate_b, mlstm_head_g, mlstm_w_out,
           diff_w_in, diff_w_out, diff_q_g, diff_k_g, diff_lq1, diff_lk1, diff_lq2, diff_lk2, diff_subln_g,
           ffn_w_gu, ffn_w_down):
    assert x.shape == (BATCH, SEQ, D_MODEL) and ctx.shape == (BATCH, CTX_LEN, D_MODEL)
    xs = jnp.concatenate([x.reshape(N_LAT, D_MODEL), ctx.reshape(N_CTX, D_MODEL)], axis=0)
    cc = jnp.concatenate([c, c_ctx[None], jnp.zeros((8 - BATCH - 1, D_MODEL), F32)], axis=0)
    mods = _ada(cc, ada_w, ada_b).reshape(DEPTH, 8, 1, 6 * D_MODEL)
    cos_t, sin_t = _rope_tables()
    row = lambda v: v.reshape(1, -1)
    ffn_gu = ffn_w_gu.astype(BF16)
    ffn_down = ffn_w_down.astype(BF16)
    mlstm_in = mlstm_w_in.astype(BF16)
    mlstm_kt = jnp.swapaxes(mlstm_w_in[:, :, MLSTM_QK:2 * MLSTM_QK], 1, 2).astype(BF16)
    mlstm_out = mlstm_w_out.astype(BF16)
    diff_in = jnp.concatenate([_pair_major(diff_w_in[..., :2 * D_MODEL]), diff_w_in[..., 2 * D_MODEL:]],
                              axis=-1).astype(BF16)
    diff_out = diff_w_out.astype(BF16)

    for i in range(DEPTH):
        last = i == DEPTH - 1
        n_rows = N_LAT if last else NT
        j = i // N_MIXERS
        if i % N_MIXERS == 0:
            w_gate = jnp.pad(mlstm_w_in[j, :, MLSTM_MAIN:], ((0, 0), (0, LANES - N_GATES))).astype(BF16)
            gate_b = jnp.pad(mlstm_gate_b[j], (0, LANES - N_GATES)).reshape(1, LANES)
            act, k_t, gates = _inproj_mlstm(xs, row(norm_g[i, 0]), mods, i, mlstm_in, mlstm_kt, j, w_gate)
            hf = _mlstm_scan(act, k_t, gates, gate_b, None, None, reverse=False)
            a = _mlstm_scan(act, k_t, gates, gate_b, row(mlstm_head_g[j]), hf, reverse=True)
            w_out = mlstm_out
        else:
            lam_init = 0.8 - 0.6 * math.exp(-0.3 * i)
            q_gain = _pair_major(diff_q_g[j]) * (DIFF_DH ** -0.5 * math.log2(math.e))
            qkg = jnp.concatenate([jnp.tile(q_gain, 2 * DIFF_HEADS),
                                   jnp.tile(_pair_major(diff_k_g[j]), 2 * DIFF_HEADS)]).reshape(1, 2 * D_MODEL)
            qkv = _inproj_diff(xs, row(norm_g[i, 0]), mods, i, diff_in, j, qkg, cos_t, sin_t)
            lams = [row(diff_lq1[j]), row(diff_lk1[j]), row(diff_lq2[j]), row(diff_lk2[j])]
            a = _attention(qkv, lams, row(diff_subln_g[j]), lam_init, None, n_rows)
            if not last:
                a = _attention(qkv, lams, row(diff_subln_g[j]), lam_init, a)
            w_out = diff_out
        xs = _ffn(a, w_out, j, xs, row(norm_g[i, 1]), mods, i, ffn_gu, ffn_down, n_rows // TMF)
    return xs.reshape(BATCH, SEQ, D_MODEL)
```

```python
import functools
import math

import jax
import jax.numpy as jnp
from jax import lax
from jax.experimental import pallas as pl
from jax.experimental.pallas import tpu as pltpu

F32 = jnp.float32
BF16 = jnp.bfloat16

D_MODEL = 2048
BATCH = 4
SEQ = 4096
DEPTH = 4
GRID_W = 64
CTX_LEN = 256
N_MIXERS = 2

MLSTM_HEADS = 8
MLSTM_DK = D_MODEL // (2 * MLSTM_HEADS)
MLSTM_DV = D_MODEL // MLSTM_HEADS
GATE_CAP = 15.0
MLSTM_QK = MLSTM_HEADS * MLSTM_DK
MLSTM_MAIN = 2 * MLSTM_QK + 2 * D_MODEL
N_GATES = 4 * MLSTM_HEADS

DIFF_HEADS = 8
DIFF_DH = D_MODEL // (2 * DIFF_HEADS)
ROPE_BASE = 10000.0

FFN_HIDDEN = -(-(8 * D_MODEL) // (3 * 256)) * 256
EPS = 1e-6

N_LAT = BATCH * SEQ
N_CTX = BATCH * CTX_LEN
NT = N_LAT + N_CTX

LANES = 128
TM = 1024
TN = 1024
TMF = 512
TH = 512
CHUNK = 256
TQ = 1024
TK = 1024
VMEM_LIMIT = 56 * 1024 * 1024

NT_DIMS = (((1,), (1,)), ((), ()))


def _params(*sem):
    return pltpu.CompilerParams(dimension_semantics=sem, vmem_limit_bytes=VMEM_LIMIT)


def _mod_row(i, tm):
    return jnp.minimum(i // (SEQ // tm), BATCH)


def _mod_spec(layer, k, tm=TM, tile=lambda i, j: i):
    return pl.BlockSpec((None, None, 1, D_MODEL), lambda i, j: (layer, _mod_row(tile(i, j), tm), 0, k))


def _early_tile(n_tiles, from_step):
    return lambda i, j: jnp.minimum(i + jnp.where(j >= from_step, 1, 0), n_tiles - 1)


def _lane_repeat(x, n, axis):
    assert axis == 1
    return jnp.concatenate([x] * n, axis=1) if n > 1 else x


def _norm_modulate(x, g, shift, scale):
    y = x * lax.rsqrt(jnp.mean(x * x, axis=-1, keepdims=True) + EPS) * g
    return y * (1.0 + scale) + shift


def _ada_kernel(c_ref, w_ref, b_ref, o_ref):
    c = c_ref[...]
    a = (c * jax.nn.sigmoid(c)).astype(BF16)
    o_ref[...] = jnp.dot(a, w_ref[...].astype(BF16), preferred_element_type=F32) + b_ref[...]


def _ada(cc, ada_w, ada_b):
    tn = 1024
    return pl.pallas_call(
        _ada_kernel,
        grid=(DEPTH, 6 * D_MODEL // tn),
        in_specs=[
            pl.BlockSpec((8, D_MODEL), lambda l, j: (0, 0)),
            pl.BlockSpec((None, D_MODEL, tn), lambda l, j: (l, 0, j)),
            pl.BlockSpec((None, 1, tn), lambda l, j: (l, 0, j)),
        ],
        out_specs=pl.BlockSpec((None, 8, tn), lambda l, j: (l, 0, j)),
        out_shape=jax.ShapeDtypeStruct((DEPTH, 8, 6 * D_MODEL), F32),
        compiler_params=_params("arbitrary", "arbitrary"),
        name="ada_mod",
    )(cc, ada_w, ada_b.reshape(DEPTH, 1, 6 * D_MODEL))


MLSTM_ACT = MLSTM_MAIN - MLSTM_QK
Q_TILE = MLSTM_ACT // TN - 1


def _inproj_mlstm_kernel(x_ref, g_ref, shift_ref, scale_ref, w_ref, wkt_ref, wg_ref,
                         out_ref, kt_ref, gate_ref, h_ref):
    j = pl.program_id(1)

    @pl.when(j == 0)
    def _():
        h_ref[...] = _norm_modulate(x_ref[...], g_ref[...], shift_ref[...], scale_ref[...]).astype(BF16)
        acc = jnp.dot(h_ref[...], w_ref[...], preferred_element_type=F32)
        out_ref[...] = (acc * MLSTM_DK ** -0.5).astype(BF16)
        gate_ref[...] = jnp.dot(h_ref[...], wg_ref[...], preferred_element_type=F32)

    @pl.when(j == 1)
    def _():
        kt_ref[...] = lax.dot_general(wkt_ref[...], h_ref[...], NT_DIMS,
                                      preferred_element_type=F32).astype(BF16)

    @pl.when(j > 1)
    def _():
        out_ref[...] = jnp.dot(h_ref[...], w_ref[...], preferred_element_type=F32).astype(BF16)


def _inproj_mlstm(xs, g, mods, layer, w_all, wkt_all, w_layer, w_gate):
    assert TN == MLSTM_QK
    tile = _early_tile(NT // TM, 3)
    return pl.pallas_call(
        _inproj_mlstm_kernel,
        grid=(NT // TM, MLSTM_MAIN // TN),
        in_specs=[
            pl.BlockSpec((TM, D_MODEL), lambda i, j: (tile(i, j), 0)),
            pl.BlockSpec((1, D_MODEL), lambda i, j: (0, 0)),
            _mod_spec(layer, 0, tile=tile),
            _mod_spec(layer, 1, tile=tile),
            pl.BlockSpec((None, D_MODEL, TN), lambda i, j: (w_layer, 0, jnp.where(j == 1, 0, j))),
            pl.BlockSpec((None, MLSTM_QK, D_MODEL), lambda i, j: (w_layer, 0, 0),
                         pipeline_mode=pl.Buffered(1)),
            pl.BlockSpec((D_MODEL, LANES), lambda i, j: (0, 0)),
        ],
        out_specs=[
            pl.BlockSpec((TM, TN), lambda i, j: (i, jnp.where(j <= 1, Q_TILE, j - 2))),
            pl.BlockSpec((MLSTM_QK, TM), lambda i, j: (0, i)),
            pl.BlockSpec((TM, LANES), lambda i, j: (i, 0)),
        ],
        out_shape=[
            jax.ShapeDtypeStruct((NT, MLSTM_ACT), BF16),
            jax.ShapeDtypeStruct((MLSTM_QK, NT), BF16),
            jax.ShapeDtypeStruct((NT, LANES), F32),
        ],
        scratch_shapes=[pltpu.VMEM((TM, D_MODEL), BF16)],
        compiler_params=_params("arbitrary", "arbitrary"),
        name="inproj_mlstm",
    )(xs, g, mods, mods, w_all, wkt_all, w_gate)


def _mlstm_scan_kernel(*refs, reverse, final):
    if final:
        (q_ref, kt_ref, v_ref, gate_ref, gb_ref, o_ref, hf_ref, hg_ref,
         out_ref, c_ref, n_ref, m_ref) = refs
    else:
        q_ref, kt_ref, v_ref, gate_ref, gb_ref, out_ref, c_ref, n_ref, m_ref = refs
    L = CHUNK

    @pl.when(pl.program_id(1) == 0)
    def _():
        c_ref[...] = jnp.zeros_like(c_ref)
        n_ref[...] = jnp.zeros_like(n_ref)
        m_ref[...] = jnp.zeros_like(m_ref)

    gg = GATE_CAP * jnp.tanh((gate_ref[...] + gb_ref[...]) / GATE_CAP)
    lsig = jax.nn.log_sigmoid(gg)
    t_idx = lax.broadcasted_iota(jnp.int32, (L, L), 0)
    s_idx = lax.broadcasted_iota(jnp.int32, (L, L), 1)
    mask = (s_idx >= t_idx) if reverse else (s_idx <= t_idx)
    tri = mask.astype(BF16)
    hi = lsig.astype(BF16)
    rest = lsig - hi.astype(F32)
    mid = rest.astype(BF16)
    lo = (rest - mid.astype(F32)).astype(BF16)
    bcum = (jnp.dot(tri, hi, preferred_element_type=F32) + jnp.dot(tri, mid, preferred_element_type=F32)
            + jnp.dot(tri, lo, preferred_element_type=F32))
    bal = pltpu.roll(bcum, LANES - MLSTM_HEADS, 1)
    r_t = (gg - bal).T
    base = 2 * MLSTM_HEADS if reverse else 0
    last = 0 if reverse else L - 1

    for h in range(MLSTM_HEADS):
        col = base + h
        b_rep = jnp.broadcast_to(bal[:, col:col + 1], (L, LANES))
        r_row = r_t[col:col + 1, :]
        m = m_ref[h]
        m11 = m[:, 0:1]
        b_last = b_rep[last:last + 1, 0:1]

        dmat = jnp.where(mask, _lane_repeat(b_rep, L // LANES, 1) + r_row, -jnp.inf)
        inter = b_rep + m
        m_t = jnp.maximum(inter, jnp.max(dmat, axis=-1, keepdims=True))
        w_intra = jnp.exp(dmat - _lane_repeat(m_t, L // LANES, 1))
        w_inter = jnp.exp(inter - m_t)

        qh = q_ref[:, h * MLSTM_DK:(h + 1) * MLSTM_DK]
        kth = kt_ref[h * MLSTM_DK:(h + 1) * MLSTM_DK, :]
        vh = v_ref[:, h * MLSTM_DV:(h + 1) * MLSTM_DV]
        s = jnp.dot(qh, kth, preferred_element_type=F32) * w_intra
        c_old = c_ref[h]
        num = (_lane_repeat(w_inter, MLSTM_DV // LANES, 1)
               * jnp.dot(qh, c_old.astype(BF16), preferred_element_type=F32)
               + jnp.dot(s.astype(BF16), vh, preferred_element_type=F32))
        n_old = n_ref[h]
        qn = jnp.dot(qh, n_old.astype(BF16), preferred_element_type=F32)
        den = w_inter * qn + jnp.sum(s, axis=-1, keepdims=True)
        inv = 1.0 / jnp.maximum(jnp.abs(den), jnp.exp(-m_t))
        hout = num * _lane_repeat(inv, MLSTM_DV // LANES, 1)

        a_row = b_last + r_row
        m_new = jnp.maximum(b_last + m11, jnp.max(a_row, axis=-1, keepdims=True))
        wk = jnp.exp(a_row - m_new)
        dec = jnp.exp(b_last + m11 - m_new)
        kw_t = kth.astype(F32) * wk
        c_ref[h] = dec * c_old + jnp.dot(kw_t.astype(BF16), vh, preferred_element_type=F32)
        n_ref[h] = dec * n_old + jnp.sum(kw_t, axis=-1, keepdims=True)
        m_ref[h] = jnp.broadcast_to(m_new, (1, LANES))

        sl = slice(h * MLSTM_DV, (h + 1) * MLSTM_DV)
        if final:
            hs = hf_ref[:, sl] + hout
            y = hs * lax.rsqrt(jnp.mean(hs * hs, axis=-1, keepdims=True) + EPS) * hg_ref[:, sl]
            out_ref[:, sl] = (y * jax.nn.sigmoid(o_ref[:, sl].astype(F32))).astype(BF16)
        else:
            out_ref[:, sl] = hout


def _mlstm_scan(act, k_t, gates, gate_b, head_g, hf, *, reverse):
    final = hf is not None
    n_lat_chunks = SEQ // CHUNK
    ctx_blk0 = N_LAT // CHUNK

    def row_blk(b, c):
        j = (n_lat_chunks - c) if reverse else (c - 1)
        return jnp.where(c == 0, ctx_blk0 + b, b * n_lat_chunks + j)

    in_specs = [
        pl.BlockSpec((CHUNK, MLSTM_QK), lambda b, c: (row_blk(b, c), Q_TILE)),
        pl.BlockSpec((MLSTM_QK, CHUNK), lambda b, c: (0, row_blk(b, c))),
        pl.BlockSpec((CHUNK, D_MODEL), lambda b, c: (row_blk(b, c), 0)),
        pl.BlockSpec((CHUNK, LANES), lambda b, c: (row_blk(b, c), 0)),
        pl.BlockSpec((1, LANES), lambda b, c: (0, 0)),
    ]
    args = [act, k_t, act, gates, gate_b]
    if final:
        in_specs += [
            pl.BlockSpec((CHUNK, D_MODEL), lambda b, c: (row_blk(b, c), 1)),
            pl.BlockSpec((CHUNK, D_MODEL), lambda b, c: (row_blk(b, c), 0)),
            pl.BlockSpec((1, D_MODEL), lambda b, c: (0, 0)),
        ]
        args += [act, hf, head_g]
    return pl.pallas_call(
        functools.partial(_mlstm_scan_kernel, reverse=reverse, final=final),
        grid=(BATCH, 1 + n_lat_chunks),
        in_specs=in_specs,
        out_specs=pl.BlockSpec((CHUNK, D_MODEL), lambda b, c: (row_blk(b, c), 0)),
        out_shape=jax.ShapeDtypeStruct((NT, D_MODEL), BF16 if final else F32),
        scratch_shapes=[
            pltpu.VMEM((MLSTM_HEADS, MLSTM_DK, MLSTM_DV), F32),
            pltpu.VMEM((MLSTM_HEADS, MLSTM_DK, LANES), F32),
            pltpu.VMEM((MLSTM_HEADS, 1, LANES), F32),
        ],
        compiler_params=_params("arbitrary", "arbitrary"),
        name="mlstm_scan_bwd" if reverse else "mlstm_scan_fwd",
    )(*args)


def _inproj_diff_kernel(x_ref, g_ref, shift_ref, scale_ref, w_ref, qkg_ref, cos_ref, sin_ref,
                        out_ref, h_ref, acc0_ref, acc1_ref):
    j = pl.program_id(1)
    n_qk_tiles = 2 * D_MODEL // TN
    n_tiles = 3 * D_MODEL // TN

    def matmul_into(acc_ref):
        acc_ref[...] = jnp.dot(h_ref[...], w_ref[...], preferred_element_type=F32)

    def norm_rope_from(acc_ref):
        cos = cos_ref[...]
        sin = sin_ref[...]
        for grp in range(TN // LANES):
            sl = slice(grp * LANES, (grp + 1) * LANES)
            xg = acc_ref[:, sl]
            y = xg * lax.rsqrt(jnp.mean(xg * xg, axis=-1, keepdims=True) + EPS) * qkg_ref[:, sl]
            out_ref[:, sl] = (y * cos + pltpu.roll(y, LANES // 2, 1) * sin).astype(BF16)

    def cast_from(acc_ref):
        out_ref[...] = acc_ref[...].astype(BF16)

    @pl.when(j == 0)
    def _():
        h_ref[...] = _norm_modulate(x_ref[...], g_ref[...], shift_ref[...], scale_ref[...]).astype(BF16)
        matmul_into(acc0_ref)

    is_qk_finish = jnp.logical_and(j >= 1, j <= n_qk_tiles)

    @pl.when(jnp.logical_and(is_qk_finish, j % 2 == 1))
    def _():
        matmul_into(acc1_ref)
        norm_rope_from(acc0_ref)

    @pl.when(jnp.logical_and(is_qk_finish, j % 2 == 0))
    def _():
        matmul_into(acc0_ref)
        norm_rope_from(acc1_ref)

    assert n_qk_tiles % 2 == 0 and n_tiles == n_qk_tiles + 2

    @pl.when(j == n_tiles - 1)
    def _():
        matmul_into(acc1_ref)
        cast_from(acc0_ref)

    @pl.when(j == n_tiles)
    def _():
        cast_from(acc1_ref)


def _inproj_diff(xs, g, mods, layer, w_all, w_layer, qkg, cos_t, sin_t):
    n_qk_tiles = 2 * D_MODEL // TN
    n_tiles = 3 * D_MODEL // TN
    lat_tiles = SEQ // TM

    tile = _early_tile(NT // TM, 3)
    rope_tile = _early_tile(NT // TM, n_qk_tiles + 1)

    def rope_blk(i, j):
        t = rope_tile(i, j)
        return (jnp.where(t < N_LAT // TM, t % lat_tiles, lat_tiles), 0)

    return pl.pallas_call(
        _inproj_diff_kernel,
        grid=(NT // TM, n_tiles + 1),
        in_specs=[
            pl.BlockSpec((TM, D_MODEL), lambda i, j: (tile(i, j), 0)),
            pl.BlockSpec((1, D_MODEL), lambda i, j: (0, 0)),
            _mod_spec(layer, 0, tile=tile),
            _mod_spec(layer, 1, tile=tile),
            pl.BlockSpec((None, D_MODEL, TN), lambda i, j: (w_layer, 0, jnp.where(j == n_tiles, 0, j))),
            pl.BlockSpec((1, TN), lambda i, j: (0, jnp.clip(j - 1, 0, n_qk_tiles - 1))),
            pl.BlockSpec((TM, LANES), rope_blk),
            pl.BlockSpec((TM, LANES), rope_blk),
        ],
        out_specs=pl.BlockSpec((TM, TN), lambda i, j: (i, jnp.maximum(j - 1, 0))),
        out_shape=jax.ShapeDtypeStruct((NT, 3 * D_MODEL), BF16),
        scratch_shapes=[
            pltpu.VMEM((TM, D_MODEL), BF16),
            pltpu.VMEM((TM, TN), F32),
            pltpu.VMEM((TM, TN), F32),
        ],
        compiler_params=_params("arbitrary", "arbitrary"),
        name="inproj_diff",
    )(xs, g, mods, mods, w_all, qkg, cos_t, sin_t)


def _attn_kernel(*refs, n_lat_chunks, lam_init, tq):
    if n_lat_chunks:
        (q_ref, kl_ref, vl_ref, kc_ref, vc_ref, lq1_ref, lk1_ref, lq2_ref, lk2_ref, sg_ref,
         out_ref, acc_ref, m_ref, l_ref) = refs
    else:
        (q_ref, kc_ref, vc_ref, lq1_ref, lk1_ref, lq2_ref, lk2_ref, sg_ref,
         out_ref, acc_ref, m_ref, l_ref) = refs
    dv = 2 * DIFF_DH
    lam = (jnp.exp(jnp.sum(lq1_ref[...] * lk1_ref[...], axis=-1, keepdims=True))
           - jnp.exp(jnp.sum(lq2_ref[...] * lk2_ref[...], axis=-1, keepdims=True)) + lam_init)

    def query_tile(t, carry):
        rows = pl.ds(pl.multiple_of(t * tq, tq), tq)
        q = q_ref[rows, :]

        def scores(kblk):
            return jnp.concatenate(
                [lax.dot_general(q[:, t2 * DIFF_DH:(t2 + 1) * DIFF_DH], kblk[:, t2 * DIFF_DH:(t2 + 1) * DIFF_DH],
                                 NT_DIMS, preferred_element_type=F32) for t2 in range(2)], axis=0)

        blocks = [(scores(kc_ref[...]), vc_ref[...])]
        if n_lat_chunks:
            blocks.append((scores(kl_ref[:TK, :]), vl_ref[:TK, :]))
        m0 = functools.reduce(jnp.maximum, [jnp.max(s, axis=-1, keepdims=True) for s, _ in blocks])
        weights = [jnp.exp2(s - m0) for s, _ in blocks]
        m_ref[...] = jnp.broadcast_to(m0, m_ref.shape)
        l_ref[...] = jnp.broadcast_to(sum(jnp.sum(p, axis=-1, keepdims=True) for p in weights), l_ref.shape)
        acc_ref[...] = sum(jnp.dot(p.astype(BF16), v, preferred_element_type=F32)
                           for p, (_, v) in zip(weights, blocks))

        if n_lat_chunks:
            def body(c, carry):
                off = pl.multiple_of(c * TK, TK)
                s = scores(kl_ref[pl.ds(off, TK), :])
                m_old = m_ref[...]
                m_new = jnp.maximum(m_old, jnp.max(s, axis=-1, keepdims=True))
                p = jnp.exp2(s - _lane_repeat(m_new, TK // LANES, 1))
                alpha = jnp.exp2(m_old - m_new)
                l_ref[...] = alpha * l_ref[...] + jnp.sum(p, axis=-1, keepdims=True)
                acc_ref[...] = (_lane_repeat(alpha, dv // LANES, 1) * acc_ref[...]
                                + jnp.dot(p.astype(BF16), vl_ref[pl.ds(off, TK), :],
                                          preferred_element_type=F32))
                m_ref[...] = m_new
                return carry
            lax.fori_loop(1, n_lat_chunks, body, 0)

        o_all = acc_ref[...] * _lane_repeat(1.0 / l_ref[...], dv // LANES, 1)
        o = o_all[:tq] - lam * o_all[tq:]
        y = o * lax.rsqrt(jnp.mean(o * o, axis=-1, keepdims=True) + EPS) * sg_ref[...]
        out_ref[rows, :] = (y * (1.0 - lam_init)).astype(BF16)
        return carry

    lax.fori_loop(0, q_ref.shape[0] // tq, query_tile, 0)


def _attention(qkv, lams, subln_g, lam_init, ctx_queries):
    dv = 2 * DIFF_DH
    kcol = D_MODEL // dv
    vcol = 2 * D_MODEL // dv
    ctx_blk0 = N_LAT // CTX_LEN
    small = [pl.BlockSpec((1, DIFF_DH), lambda b, h: (0, 0))] * 4
    small.append(pl.BlockSpec((1, dv), lambda b, h: (0, 0)))
    ctx_specs = [
        pl.BlockSpec((CTX_LEN, dv), lambda b, h: (ctx_blk0 + b, kcol + h)),
        pl.BlockSpec((CTX_LEN, dv), lambda b, h: (ctx_blk0 + b, vcol + h)),
    ]
    if not ctx_queries:
        tq, q_rows = TQ, SEQ
        q_map = out_map = lambda b, h: (b, h)
        in_specs = [
            pl.BlockSpec((q_rows, dv), q_map),
            pl.BlockSpec((SEQ, dv), lambda b, h: (b, kcol + h)),
            pl.BlockSpec((SEQ, dv), lambda b, h: (b, vcol + h)),
        ] + ctx_specs + small
        args = [qkv, qkv, qkv, qkv, qkv] + list(lams) + [subln_g]
        n_lat_chunks = SEQ // TK
    else:
        tq, q_rows = CTX_LEN, CTX_LEN
        q_map = lambda b, h: (ctx_blk0 + b, h)
        out_map = lambda b, h: (b, h)
        in_specs = [pl.BlockSpec((q_rows, dv), q_map)] + ctx_specs + small
        args = [qkv, qkv, qkv] + list(lams) + [subln_g]
        n_lat_chunks = 0
    return pl.pallas_call(
        functools.partial(_attn_kernel, n_lat_chunks=n_lat_chunks, lam_init=lam_init, tq=tq),
        grid=(BATCH, DIFF_HEADS),
        in_specs=in_specs,
        out_specs=pl.BlockSpec((q_rows, dv), out_map),
        out_shape=jax.ShapeDtypeStruct((N_CTX if ctx_queries else N_LAT, D_MODEL), BF16),
        scratch_shapes=[
            pltpu.VMEM((2 * tq, dv), F32),
            pltpu.VMEM((2 * tq, LANES), F32),
            pltpu.VMEM((2 * tq, LANES), F32),
        ],
        compiler_params=_params("arbitrary", "arbitrary"),
        name="diff_attn_ctx" if n_lat_chunks == 0 else "diff_attn",
    )(*args)


def _ffn_kernel(*refs, layer, n_tiles, has_ctx):
    if has_ctx:
        a_ref, actx_ref, *refs = refs
    else:
        a_ref, *refs = refs
    (wo_ref, x_ref, g_ref, gate1_ref, shift_ref, scale_ref, gate2_ref, wgu_hbm, wd_hbm,
     out_ref, h_ref, wg_buf, wu_buf, wd_buf, sem) = refs
    i = pl.program_id(0)
    nh = FFN_HIDDEN // TH
    assert nh % 2 == 1

    def weight_copies(k, slot):
        col = pl.multiple_of(k * TH, TH)
        return (
            pltpu.make_async_copy(wgu_hbm.at[layer, :, pl.ds(col, TH)], wg_buf.at[slot], sem.at[0, slot]),
            pltpu.make_async_copy(wgu_hbm.at[layer, :, pl.ds(FFN_HIDDEN + col, TH)], wu_buf.at[slot],
                                  sem.at[1, slot]),
            pltpu.make_async_copy(wd_hbm.at[layer, pl.ds(col, TH), :], wd_buf.at[slot], sem.at[2, slot]),
        )

    def start(k, slot):
        for cp in weight_copies(k, slot):
            cp.start()

    def wait(k, slot):
        for cp in weight_copies(k, slot):
            cp.wait()

    def hidden_tile(h, base, slot):
        gt = jnp.dot(h, wg_buf[slot], preferred_element_type=F32)
        up = jnp.dot(h, wu_buf[slot], preferred_element_type=F32)
        act = (gt * jax.nn.sigmoid(gt) * up).astype(BF16)
        out_ref[...] = base + gate2_ref[...] * jnp.dot(act, wd_buf[slot], preferred_element_type=F32)

    slot0 = i % 2

    @pl.when(i == 0)
    def _():
        start(0, 0)

    wait(0, slot0)
    start(1, 1 - slot0)
    a = a_ref[...]
    if has_ctx:
        a = jnp.where(i >= N_LAT // TMF, actx_ref[...], a)
    x1 = x_ref[...] + gate1_ref[...] * jnp.dot(a, wo_ref[...], preferred_element_type=F32)
    h = _norm_modulate(x1, g_ref[...], shift_ref[...], scale_ref[...]).astype(BF16)
    h_ref[...] = h
    hidden_tile(h, x1, slot0)

    def body(k, carry):
        slot = (i + k) % 2
        wait(k, slot)

        @pl.when(k + 1 < nh)
        def _():
            start(k + 1, 1 - slot)

        @pl.when(jnp.logical_and(k + 1 == nh, i + 1 < n_tiles))
        def _():
            start(0, 1 - slot)

        hidden_tile(h_ref[...], out_ref[...], slot)
        return carry

    lax.fori_loop(1, nh, body, 0)


def _ffn(a, a_ctx, w_out, w_out_layer, xs, g, mods, layer, w_gu, w_down, n_tiles):
    row_tile = lambda i: (i, 0)
    mod_spec = lambda chunk: pl.BlockSpec((None, None, 1, D_MODEL),
                                          lambda i: (layer, _mod_row(i, TMF), 0, chunk))
    lat_tiles = N_LAT // TMF
    if a_ctx is None:
        a_specs, a_args = [pl.BlockSpec((TMF, D_MODEL), row_tile)], [a]
    else:
        a_specs = [
            pl.BlockSpec((TMF, D_MODEL), lambda i: (jnp.minimum(i, lat_tiles - 1), 0)),
            pl.BlockSpec((TMF, D_MODEL), lambda i: (jnp.maximum(i - lat_tiles, 0), 0),
                         pipeline_mode=pl.Buffered(1)),
        ]
        a_args = [a, a_ctx]
    return pl.pallas_call(
        functools.partial(_ffn_kernel, layer=layer, n_tiles=n_tiles, has_ctx=a_ctx is not None),
        grid=(n_tiles,),
        in_specs=a_specs + [
            pl.BlockSpec((None, D_MODEL, D_MODEL), lambda i: (w_out_layer, 0, 0),
                         pipeline_mode=pl.Buffered(1)),
            pl.BlockSpec((TMF, D_MODEL), row_tile),
            pl.BlockSpec((1, D_MODEL), lambda i: (0, 0)),
            mod_spec(2),
            mod_spec(3),
            mod_spec(4),
            mod_spec(5),
            pl.BlockSpec(memory_space=pl.ANY),
            pl.BlockSpec(memory_space=pl.ANY),
        ],
        out_specs=pl.BlockSpec((TMF, D_MODEL), row_tile),
        out_shape=jax.ShapeDtypeStruct((n_tiles * TMF, D_MODEL), F32),
        scratch_shapes=[
            pltpu.VMEM((TMF, D_MODEL), BF16),
            pltpu.VMEM((2, D_MODEL, TH), BF16),
            pltpu.VMEM((2, D_MODEL, TH), BF16),
            pltpu.VMEM((2, TH, D_MODEL), BF16),
            pltpu.SemaphoreType.DMA((3, 2)),
        ],
        compiler_params=_params("arbitrary"),
        name="ffn",
    )(*a_args, w_out, xs, g, mods, mods, mods, mods, w_gu, w_down)


def _pair_major(a):
    n_freq = DIFF_DH // 4
    lead = a.shape[:-1]
    a = a.reshape(*lead, a.shape[-1] // DIFF_DH, 2, 2, n_freq)
    return jnp.swapaxes(a, -3, -2).reshape(*lead, -1)


def _rope_tables():
    n_freq = DIFF_DH // 4
    pos = jnp.arange(SEQ)
    freqs = ROPE_BASE ** (-jnp.arange(n_freq, dtype=F32) / n_freq)
    ang = jnp.stack([pos // GRID_W, pos % GRID_W], axis=-1).astype(F32)[:, :, None] * freqs
    cos, sin = jnp.cos(ang), jnp.sin(ang)
    cos_t = jnp.concatenate([cos[:, 0], cos[:, 1], cos[:, 0], cos[:, 1]], axis=-1)
    sin_t = jnp.concatenate([-sin[:, 0], -sin[:, 1], sin[:, 0], sin[:, 1]], axis=-1)
    cos_t = jnp.concatenate([cos_t, jnp.ones((TM, LANES), F32)], axis=0)
    sin_t = jnp.concatenate([sin_t, jnp.zeros((TM, LANES), F32)], axis=0)
    return cos_t, sin_t


def kernel(x, c, ctx, c_ctx, ada_w, ada_b, norm_g, mlstm_w_in, mlstm_gate_b, mlstm_head_g, mlstm_w_out,
           diff_w_in, diff_w_out, diff_q_g, diff_k_g, diff_lq1, diff_lk1, diff_lq2, diff_lk2, diff_subln_g,
           ffn_w_gu, ffn_w_down):
    assert x.shape == (BATCH, SEQ, D_MODEL) and ctx.shape == (BATCH, CTX_LEN, D_MODEL)
    xs = jnp.concatenate([x.reshape(N_LAT, D_MODEL), ctx.reshape(N_CTX, D_MODEL)], axis=0)
    cc = jnp.concatenate([c, c_ctx[None], jnp.zeros((8 - BATCH - 1, D_MODEL), F32)], axis=0)
    mods = _ada(cc, ada_w, ada_b).reshape(DEPTH, 8, 1, 6 * D_MODEL)
    cos_t, sin_t = _rope_tables()
    row = lambda v: v.reshape(1, -1)
    ffn_gu = ffn_w_gu.astype(BF16)
    ffn_down = ffn_w_down.astype(BF16)
    mlstm_in = mlstm_w_in.astype(BF16)
    mlstm_kt = jnp.swapaxes(mlstm_w_in[:, :, MLSTM_QK:2 * MLSTM_QK], 1, 2).astype(BF16)
    mlstm_out = mlstm_w_out.astype(BF16)
    diff_in = jnp.concatenate([_pair_major(diff_w_in[..., :2 * D_MODEL]), diff_w_in[..., 2 * D_MODEL:]],
                              axis=-1).astype(BF16)
    diff_out = diff_w_out.astype(BF16)

    for i in range(DEPTH):
        last = i == DEPTH - 1
        n_rows = N_LAT if last else NT
        j = i // N_MIXERS
        a_ctx = None
        if i % N_MIXERS == 0:
            w_gate = jnp.pad(mlstm_w_in[j, :, MLSTM_MAIN:], ((0, 0), (0, LANES - N_GATES))).astype(BF16)
            gate_b = jnp.pad(mlstm_gate_b[j], (0, LANES - N_GATES)).reshape(1, LANES)
            act, k_t, gates = _inproj_mlstm(xs, row(norm_g[i, 0]), mods, i, mlstm_in, mlstm_kt, j, w_gate)
            hf = _mlstm_scan(act, k_t, gates, gate_b, None, None, reverse=False)
            a = _mlstm_scan(act, k_t, gates, gate_b, row(mlstm_head_g[j]), hf, reverse=True)
            w_out = mlstm_out
        else:
            lam_init = 0.8 - 0.6 * math.exp(-0.3 * i)
            q_gain = _pair_major(diff_q_g[j]) * (DIFF_DH ** -0.5 * math.log2(math.e))
            qkg = jnp.concatenate([jnp.tile(q_gain, 2 * DIFF_HEADS),
                                   jnp.tile(_pair_major(diff_k_g[j]), 2 * DIFF_HEADS)]).reshape(1, 2 * D_MODEL)
            qkv = _inproj_diff(xs, row(norm_g[i, 0]), mods, i, diff_in, j, qkg, cos_t, sin_t)
            lams = [row(diff_lq1[j]), row(diff_lk1[j]), row(diff_lq2[j]), row(diff_lk2[j])]
            a = _attention(qkv, lams, row(diff_subln_g[j]), lam_init, False)
            if not last:
                a_ctx = _attention(qkv, lams, row(diff_subln_g[j]), lam_init, True)
            w_out = diff_out
        xs = _ffn(a, a_ctx, w_out, j, xs, row(norm_g[i, 1]), mods, i, ffn_gu, ffn_down, n_rows // TMF)
    return xs.reshape(BATCH, SEQ, D_MODEL)
```
